```python
import math
import jax, jax.numpy as jnp
from jax import lax
import numpy as np

D_MODEL = 1024
BATCH = 8
SEQ = 2048
DEPTH = 2

HEAD_DIM = 64
NORM_EPS = 1e-6
NEG_INF = -1e30
SWA_PATTERNS = ((128, 1), (512, 4), (2048, 16))
SWA_HEADS_PER_GROUP = 4
SWA_HEADS = SWA_HEADS_PER_GROUP * len(SWA_PATTERNS)
SWA_WIDTH = SWA_HEADS * HEAD_DIM
SWA_OUT_WIDTH = SWA_HEADS_PER_GROUP * HEAD_DIM
RWKV_HEADS = 12
RWKV_WIDTH = RWKV_HEADS * HEAD_DIM
DECAY_LORA = 64
ICLR_LORA = 64
GATE_LORA = 128
RWKV_STREAM = 3 * RWKV_WIDTH + 2 * DECAY_LORA + 2 * ICLR_LORA + GATE_LORA
RWKV_GN_EPS = 64e-5
DIFF_HEADS = 6
DIFF_WIDTH = DIFF_HEADS * 2 * HEAD_DIM
DIFF_Q_BLOCK = 128
DIFF_SUBLN_EPS = 1e-5
REL_BUCKETS = 32
REL_MAX_DIST = 128
REL_HEADS = SWA_HEADS + DIFF_HEADS
IN_COLS = 3 * SWA_WIDTH + RWKV_STREAM + 3 * DIFF_WIDTH + 3 * D_MODEL
N_EXPERTS = 32
TOP_K = 4
D_EXPERT = 1024
SWIGLU_LIMIT = 7.0
SWIGLU_ALPHA = 1.702

kernel_name = 'hybrid_gated_dilated_rwkv7_diffattn_moe_encoder'


def _split(t, sizes):
    out, off = [], 0
    for s in sizes:
        out.append(t[..., off:off + s])
        off += s
    return out


def _rmsnorm(x, g, eps=NORM_EPS):
    xf = x.astype(jnp.float32)
    y = xf * lax.rsqrt(jnp.mean(xf * xf, axis=-1, keepdims=True) + eps)
    return (y * g.astype(jnp.float32)).astype(x.dtype)


def _t5_bucket(rel):
    nb = REL_BUCKETS // 2
    max_exact = nb // 2
    ret = jnp.where(rel > 0, nb, 0)
    n = jnp.abs(rel)
    nf = jnp.maximum(n, 1).astype(jnp.float32)
    large = max_exact + (jnp.log(nf / max_exact) / math.log(REL_MAX_DIST / max_exact) * (nb - max_exact)).astype(jnp.int32)
    large = jnp.minimum(large, nb - 1)
    return ret + jnp.where(n < max_exact, n, large)


def _dilated_window_attn(q, k, v, bias, dilation, half):
    B, T, H, Dh = q.shape
    L = T // dilation
    nb = -(-L // half)
    Lp = nb * half
    Z = B * dilation

    def to_strided(t):
        return t.reshape(B, L, dilation, H, Dh).transpose(0, 2, 3, 1, 4).reshape(Z, H, L, Dh)

    qs = jnp.pad(to_strided(q), ((0, 0), (0, 0), (0, Lp - L), (0, 0))).reshape(Z, H, nb, half, Dh)

    def windows(t):
        t = jnp.pad(to_strided(t), ((0, 0), (0, 0), (half, half + Lp - L), (0, 0))).reshape(Z, H, nb + 2, half, Dh)
        return jnp.concatenate([t[:, :, 0:nb], t[:, :, 1:nb + 1], t[:, :, 2:nb + 2]], axis=3)

    kw, vw = windows(k), windows(v)
    s = jnp.einsum('zhnqd,zhnkd->zhnqk', qs, kw).astype(jnp.float32) * (Dh ** -0.5)
    a_idx = jnp.arange(half)[:, None]
    c_idx = jnp.arange(3 * half)[None, :]
    j = c_idx - half - a_idx
    key_m = jnp.arange(nb)[:, None] * half - half + jnp.arange(3 * half)[None, :]
    mask = (jnp.abs(j) <= half)[None] & ((key_m >= 0) & (key_m < L))[:, None, :]
    b = bias[:, jnp.clip(j + half, 0, 2 * half)].astype(jnp.float32)
    s = jnp.where(mask, s + b[:, None], NEG_INF)
    lse = jax.nn.logsumexp(s, axis=-1)
    p = jnp.exp(s - lse[..., None])
    o = jnp.einsum('zhnqk,zhnkd->zhnqd', p.astype(v.dtype), vw)
    o = o.reshape(B, dilation, H, Lp, Dh)[:, :, :, :L].transpose(0, 3, 1, 2, 4).reshape(B, T, H, Dh)
    lse = lse.reshape(B, dilation, H, Lp)[..., :L].transpose(0, 3, 1, 2).reshape(B, T, H)
    return o, lse


def _centred_shift(p, mu):
    prev = jnp.pad(p[:, :-1], ((0, 0), (1, 0), (0, 0)))
    nxt = jnp.pad(p[:, 1:], ((0, 0), (0, 1), (0, 0)))
    return p + mu[0] * (prev - p) + mu[1] * (nxt - p)


def _wkv7_scan(r, w, k, v, kk, a, reverse):
    B, T, H, N = r.shape

    def step(S, inp):
        r_t, w_t, k_t, v_t, kk_t, a_t = inp
        sa = jnp.einsum('bhvk,bhk->bhv', S, -kk_t)
        S = S * w_t[:, :, None, :] + sa[..., None] * (kk_t * a_t)[:, :, None, :] + v_t[..., None] * k_t[:, :, None, :]
        return S, jnp.einsum('bhvk,bhk->bhv', S, r_t)

    xs = tuple(jnp.moveaxis(t.astype(jnp.float32), 1, 0) for t in (r, w, k, v, kk, a))
    S0 = jnp.zeros((B, H, N, N), jnp.float32)
    _, y = lax.scan(step, S0, xs, reverse=reverse)
    return jnp.moveaxis(y, 0, 1)


def _rwkv7_bidir(stream, mu, w0, w_up, a0, a_up, g_up, k_k, k_a, r_k, ln_g, ln_b):
    B, T, _ = stream.shape
    H, N = RWKV_HEADS, HEAD_DIM
    s = _centred_shift(stream, mu)
    r, k, v, wd, ad, gd = _split(s, (RWKV_WIDTH, RWKV_WIDTH, RWKV_WIDTH, 2 * DECAY_LORA, 2 * ICLR_LORA, GATE_LORA))
    wd = wd.reshape(B, T, 2, DECAY_LORA)
    ad = ad.reshape(B, T, 2, ICLR_LORA)

    def heads(t):
        return t.reshape(B, T, H, N)

    rh, vh = heads(r), heads(v)
    y = jnp.zeros((B, T, H, N), jnp.float32)
    bonus = jnp.zeros((B, T, H, N), jnp.float32)
    for di, rev in enumerate((False, True)):
        w_log = -jax.nn.softplus(-(w0[di] + jnp.tanh(wd[:, :, di]) @ w_up[di])) - 0.5
        decay = jnp.exp(-jnp.exp(w_log.astype(jnp.float32)))
        a = jax.nn.sigmoid(a0[di] + ad[:, :, di] @ a_up[di])
        kk = heads(k * k_k[di]).astype(jnp.float32)
        kk = kk * lax.rsqrt(jnp.maximum(jnp.sum(kk * kk, axis=-1, keepdims=True), 1e-24))
        kd = heads(k * (1 + (a - 1) * k_a[di]))
        y = y + _wkv7_scan(rh, heads(decay), kd, vh, kk, heads(a), rev)
        bonus = bonus + jnp.sum((rh * kd * r_k).astype(jnp.float32), axis=-1, keepdims=True) * vh.astype(jnp.float32)
    mean = jnp.mean(y, axis=-1, keepdims=True)
    var = jnp.mean(jnp.square(y - mean), axis=-1, keepdims=True)
    y = ((y - mean) * lax.rsqrt(var + RWKV_GN_EPS)).reshape(B, T, RWKV_WIDTH) * ln_g.astype(jnp.float32) + ln_b.astype(jnp.float32)
    g = jax.nn.sigmoid(gd) @ g_up
    return ((y + bonus.reshape(B, T, RWKV_WIDTH)) * g.astype(jnp.float32)).astype(stream.dtype)


def _diff_attention(q, k, v, table, lam, lambda_init, subln_g):
    B, T, H, _, Dh = q.shape
    nqb = T // DIFF_Q_BLOCK
    qb = jnp.moveaxis(q.reshape(B, nqb, DIFF_Q_BLOCK, H, 2, Dh), 1, 0)
    kpos = jnp.arange(T)

    def block(args):
        q_blk, i = args
        qpos = i * DIFF_Q_BLOCK + jnp.arange(DIFF_Q_BLOCK)
        bias = table[_t5_bucket(kpos[None, :] - qpos[:, None])]
        s = jnp.einsum('bqhcd,bkhcd->bchqk', q_blk, k).astype(jnp.float32) * (Dh ** -0.5)
        p = jax.nn.softmax(s + jnp.transpose(bias, (2, 0, 1)).astype(jnp.float32)[None, None], axis=-1)
        attn = p[:, 0] - lam * p[:, 1]
        return jnp.einsum('bhqk,bkhd->bqhd', attn.astype(v.dtype), v)

    o = lax.map(block, (qb, jnp.arange(nqb)))
    o = jnp.moveaxis(o, 0, 1).reshape(B, T, H, 2 * Dh)
    o = _rmsnorm(o, subln_g, DIFF_SUBLN_EPS) * (1.0 - lambda_init)
    return o.reshape(B, T, DIFF_WIDTH)


def _moe(h, router_w, router_b, w1, b1, w2, b2):
    B, T, D = h.shape
    ht = h.reshape(B * T, D)
    logits = (ht @ router_w + router_b).astype(jnp.float32)
    top_v, top_i = lax.top_k(logits, TOP_K)
    probs = jax.nn.softmax(top_v, axis=-1)
    gates = jnp.sum(jax.nn.one_hot(top_i, N_EXPERTS, dtype=jnp.float32) * probs[..., None], axis=1)

    def expert(acc, params):
        w1e, b1e, w2e, b2e, ge = params
        hh = ht @ w1e + b1e
        x_glu = jnp.minimum(hh[:, ::2], SWIGLU_LIMIT)
        x_lin = jnp.clip(hh[:, 1::2], -SWIGLU_LIMIT, SWIGLU_LIMIT)
        act = x_glu * jax.nn.sigmoid(SWIGLU_ALPHA * x_glu) * (x_lin + 1)
        out = act @ w2e + b2e
        return acc + ge[:, None] * out.astype(jnp.float32), None

    acc0 = jnp.zeros((B * T, D), jnp.float32)
    acc, _ = lax.scan(expert, acc0, (w1, b1, w2, b2, gates.T))
    return acc.reshape(B, T, D).astype(h.dtype)


def setup_inputs(seed: int = 0) -> dict:
    key = jax.random.key(seed)
    ks = iter(jax.random.split(key, 40))
    L, D, W, F, E = DEPTH, D_MODEL, RWKV_WIDTH, D_EXPERT, N_EXPERTS

    def nrm(shape, scale):
        return jax.random.normal(next(ks), shape, jnp.float32) * scale

    def uni(shape, lo, hi):
        return jax.random.uniform(next(ks), shape, jnp.float32, lo, hi)

    return {
        'x': nrm((BATCH, SEQ, D), 1.0),
        'c': nrm((BATCH, D), 1.0),
        'w_mod': nrm((L, D, 6 * D), 0.3 * D ** -0.5),
        'b_mod': nrm((L, 6 * D), 0.1),
        'norm1_g': 1.0 + nrm((L, D), 0.05),
        'norm2_g': 1.0 + nrm((L, D), 0.05),
        'w_in': nrm((L, D, IN_COLS), D ** -0.5),
        'rwkv_mu': uni((L, 2, RWKV_STREAM), 0.0, 0.5),
        'rwkv_w0': uni((L, 2, W), -6.0, 1.0),
        'rwkv_w_up': nrm((L, 2, DECAY_LORA, W), 0.5 * DECAY_LORA ** -0.5),
        'rwkv_a0': nrm((L, 2, W), 0.5),
        'rwkv_a_up': nrm((L, 2, ICLR_LORA, W), 0.5 * ICLR_LORA ** -0.5),
        'rwkv_g_up': nrm((L, GATE_LORA, W), GATE_LORA ** -0.5),
        'rwkv_k_k': 0.85 + nrm((L, 2, W), 0.05),
        'rwkv_k_a': 1.0 + nrm((L, 2, W), 0.05),
        'rwkv_r_k': nrm((L, RWKV_HEADS, HEAD_DIM), 0.1),
        'rwkv_ln_g': 1.0 + nrm((L, W), 0.05),
        'rwkv_ln_b': nrm((L, W), 0.02),
        'diff_lambda': nrm((L, 4, HEAD_DIM), 0.1),
        'diff_subln_g': 1.0 + nrm((L, 2 * HEAD_DIM), 0.05),
        'rel_bias': nrm((REL_BUCKETS, REL_HEADS), 0.5),
        'w_branch_a': nrm((L, SWA_OUT_WIDTH, D), SWA_OUT_WIDTH ** -0.5),
        'w_branch_b': nrm((L, RWKV_WIDTH, D), RWKV_WIDTH ** -0.5),
        'w_branch_c': nrm((L, DIFF_WIDTH, D), DIFF_WIDTH ** -0.5),
        'w_out': nrm((L, D, D), D ** -0.5),
        'router_w': nrm((L, D, E), D ** -0.5),
        'router_b': nrm((L, E), 0.01),
        'moe_w1': nrm((L, E, D, 2 * F), D ** -0.5),
        'moe_b1': nrm((L, E, 2 * F), 0.01),
        'moe_w2': nrm((L, E, F, D), F ** -0.5),
        'moe_b2': nrm((L, E, D), 0.01),
        'final_norm_g': 1.0 + nrm((D,), 0.05),
    }


def reference(x, c, w_mod, b_mod, norm1_g, norm2_g, w_in, rwkv_mu, rwkv_w0, rwkv_w_up, rwkv_a0, rwkv_a_up, rwkv_g_up, rwkv_k_k, rwkv_k_a, rwkv_r_k, rwkv_ln_g, rwkv_ln_b, diff_lambda, diff_subln_g, rel_bias, w_branch_a, w_branch_b, w_branch_c, w_out, router_w, router_b, moe_w1, moe_b1, moe_w2, moe_b2, final_norm_g):
    B, T, D = x.shape
    n_groups = len(SWA_PATTERNS)
    bias_a = []
    for g, (window, dil) in enumerate(SWA_PATTERNS):
        half = window // (2 * dil)
        offs = jnp.arange(-half, half + 1) * dil
        bias_a.append(rel_bias[_t5_bucket(offs)][:, g * SWA_HEADS_PER_GROUP:(g + 1) * SWA_HEADS_PER_GROUP].T)
    table_c = rel_bias[:, SWA_HEADS:]
    cond = jax.nn.silu(c)

    for l in range(DEPTH):
        mod = (cond @ w_mod[l] + b_mod[l])[:, None, :]
        sh1, sc1, gt1, sh2, sc2, gt2 = jnp.split(mod, 6, axis=-1)

        h = _rmsnorm(x, norm1_g[l]) * (1 + sc1) + sh1
        proj = h @ w_in[l]
        a_qkv, b_stream, c_qkv, gate_logits = _split(proj, (3 * SWA_WIDTH, RWKV_STREAM, 3 * DIFF_WIDTH, 3 * D))

        aq, ak, av = [t.reshape(B, T, n_groups, SWA_HEADS_PER_GROUP, HEAD_DIM) for t in _split(a_qkv, (SWA_WIDTH, SWA_WIDTH, SWA_WIDTH))]
        outs, lses = [], []
        for g, (window, dil) in enumerate(SWA_PATTERNS):
            o_g, lse_g = _dilated_window_attn(aq[:, :, g], ak[:, :, g], av[:, :, g], bias_a[g], dil, window // (2 * dil))
            outs.append(o_g.astype(jnp.float32))
            lses.append(lse_g)
        wts = jax.nn.softmax(jnp.stack(lses, axis=0), axis=0)
        o_a = jnp.sum(wts[..., None] * jnp.stack(outs, axis=0), axis=0).reshape(B, T, SWA_OUT_WIDTH).astype(x.dtype)

        o_b = _rwkv7_bidir(b_stream, rwkv_mu[l], rwkv_w0[l], rwkv_w_up[l], rwkv_a0[l], rwkv_a_up[l], rwkv_g_up[l], rwkv_k_k[l], rwkv_k_a[l], rwkv_r_k[l], rwkv_ln_g[l], rwkv_ln_b[l])

        cq, ck, cv = _split(c_qkv, (DIFF_WIDTH, DIFF_WIDTH, DIFF_WIDTH))
        dl = diff_lambda[l].astype(jnp.float32)
        lambda_init = 0.8 - 0.6 * math.exp(-0.3 * l)
        lam = jnp.exp(jnp.sum(dl[0] * dl[1])) - jnp.exp(jnp.sum(dl[2] * dl[3])) + lambda_init
        o_c = _diff_attention(cq.reshape(B, T, DIFF_HEADS, 2, HEAD_DIM), ck.reshape(B, T, DIFF_HEADS, 2, HEAD_DIM), cv.reshape(B, T, DIFF_HEADS, 2 * HEAD_DIM), table_c, lam, lambda_init, diff_subln_g[l])

        g_a, g_b, g_c = jnp.split(jax.nn.sigmoid(gate_logits), 3, axis=-1)
        merged = g_a * (o_a @ w_branch_a[l]) + g_b * (o_b @ w_branch_b[l]) + g_c * (o_c @ w_branch_c[l])
        x = x + gt1 * (merged @ w_out[l])

        h2 = _rmsnorm(x, norm2_g[l]) * (1 + sc2) + sh2
        x = x + gt2 * _moe(h2, router_w[l], router_b[l], moe_w1[l], moe_b1[l], moe_w2[l], moe_b2[l])

    return _rmsnorm(x, final_norm_g)
```

```python
import functools
import math

import jax
import jax.numpy as jnp
from jax import lax
from jax.experimental import pallas as pl
from jax.experimental.pallas import tpu as pltpu

F32 = jnp.float32
BF16 = jnp.bfloat16
HIGHEST = lax.Precision.HIGHEST

D_MODEL = 1024
HEAD_DIM = 64
NORM_EPS = 1e-6
NEG_INF = -1e30

SWA_PATTERNS = ((128, 1), (512, 4), (2048, 16))
SWA_GROUP_HEADS = 4
SWA_GROUP_WIDTH = SWA_GROUP_HEADS * HEAD_DIM
SWA_WIDTH = len(SWA_PATTERNS) * SWA_GROUP_WIDTH
SWA_HALF = 64
SWA_QUERY_BLOCK = 128

RWKV_HEADS = 12
RWKV_WIDTH = RWKV_HEADS * HEAD_DIM
RWKV_LORA = 384
RWKV_GN_EPS = 64e-5
RWKV_CHUNK = 64

DIFF_HEADS = 6
DIFF_WIDTH = DIFF_HEADS * 2 * HEAD_DIM
DIFF_SUBLN_EPS = 1e-5
DIFF_Q_TILE = 256

REL_BUCKETS = 32
REL_MAX_DIST = 128

N_EXPERTS = 32
TOP_K = 4
D_EXPERT = 1024
SWIGLU_LIMIT = 7.0
SWIGLU_ALPHA = 1.702
MOE_ROW_TILE = 256

COL_GATES = 0
COL_A = 3 * D_MODEL
COL_C = COL_A + 3 * SWA_WIDTH
COL_R = COL_C + 3 * DIFF_WIDTH
COL_LORA = COL_R + 3 * RWKV_WIDTH
PROJ_COLS = COL_LORA + RWKV_LORA
PROJ_COLS_PADDED = 10752
PROJ_COL_TILE = 1536
PROJ_ROW_TILE = 1024

VMEM_LIMIT = 48 * 1024 * 1024


def _params(*sem):
    return pltpu.CompilerParams(dimension_semantics=sem, vmem_limit_bytes=VMEM_LIMIT)


def _sigmoid(x):
    return 1.0 / (1.0 + jnp.exp(-x))


def _dot(a, b, dims=((1,), (0,)), precision=None):
    return lax.dot_general(a, b, (dims, ((), ())), precision=precision, preferred_element_type=F32)


def _dot_nt(a, b, precision=None):
    return _dot(a, b, ((1,), (1,)), precision)


def _dot_tn(a, b, precision=None):
    return _dot(a, b, ((0,), (0,)), precision)


def _mod_kernel(c_ref, w_ref, b_ref, o_ref):
    c = c_ref[...]
    cond = c * _sigmoid(c)
    o_ref[...] = _dot(cond, w_ref[...], precision=HIGHEST) + b_ref[...]


def _modulation(c, w_mod, b_mod):
    n_layers, d, n = w_mod.shape
    batch = c.shape[0]
    tn = 1536
    out = pl.pallas_call(
        _mod_kernel,
        grid=(n_layers, n // tn),
        in_specs=[
            pl.BlockSpec((batch, d), lambda l, j: (0, 0)),
            pl.BlockSpec((None, d, tn), lambda l, j: (l, 0, j)),
            pl.BlockSpec((None, 1, tn), lambda l, j: (l, 0, j)),
        ],
        out_specs=pl.BlockSpec((None, batch, tn), lambda l, j: (l, 0, j)),
        out_shape=jax.ShapeDtypeStruct((n_layers, batch, n), F32),
        compiler_params=_params("parallel", "parallel"),
    )(c, w_mod, b_mod.reshape(n_layers, 1, n))
    return out.reshape(n_layers, batch, 6, 1, d)


def _modulated_norm(x, g, scale, shift):
    y = x * lax.rsqrt(jnp.mean(x * x, axis=-1, keepdims=True) + NORM_EPS) * g
    return y * (1.0 + scale) + shift


def _norm_proj_kernel(x_ref, g_ref, mod_ref, w_ref, o_ref, h_ref):
    @pl.when(pl.program_id(2) == 0)
    def _():
        h_ref[...] = _modulated_norm(x_ref[...], g_ref[...], mod_ref[1], mod_ref[0]).astype(BF16)

    o_ref[...] = _dot(h_ref[...], w_ref[...]).astype(o_ref.dtype)


def _norm_proj(x, g, mod_l, w):
    batch, seq, d = x.shape
    n = w.shape[1]
    tm, tn = min(PROJ_ROW_TILE, seq), PROJ_COL_TILE
    return pl.pallas_call(
        _norm_proj_kernel,
        grid=(batch, seq // tm, n // tn),
        in_specs=[
            pl.BlockSpec((None, tm, d), lambda b, i, j: (b, i, 0)),
            pl.BlockSpec((1, d), lambda b, i, j: (0, 0)),
            pl.BlockSpec((None, 6, 1, d), lambda b, i, j: (b, 0, 0, 0)),
            pl.BlockSpec((d, tn), lambda b, i, j: (0, j)),
        ],
        out_specs=pl.BlockSpec((None, tm, tn), lambda b, i, j: (b, i, j)),
        out_shape=jax.ShapeDtypeStruct((batch, seq, n), BF16),
        scratch_shapes=[pltpu.VMEM((tm, d), BF16)],
        compiler_params=_params("parallel", "parallel", "arbitrary"),
    )(x, g.reshape(1, d), mod_l, w)


def _swa_kernel(q_ref, k_ref, v_ref, bias_ref, o_ref, lse_ref, kpad_ref, vpad_ref, *, length):
    half, qb, hd = SWA_HALF, SWA_QUERY_BLOCK, HEAD_DIM
    win = qb + 2 * half
    zeros = jnp.zeros((half, SWA_GROUP_WIDTH), BF16)
    for pad_ref, src_ref in ((kpad_ref, k_ref), (vpad_ref, v_ref)):
        pad_ref[0:half, :] = zeros
        pad_ref[half + length:half + length + half, :] = zeros
        pad_ref[half:half + length, :] = src_ref[...]

    def block(n, carry):
        q0 = pl.multiple_of(n * qb, qb)
        q = q_ref[pl.ds(q0, qb), :]
        kw = kpad_ref[pl.ds(q0, win), :]
        vw = vpad_ref[pl.ds(q0, win), :]
        key_pos = q0 - half + lax.broadcasted_iota(jnp.int32, (1, win), 1)
        in_range = (key_pos >= 0) & (key_pos < length)
        outs, lses = [], []
        for h in range(SWA_GROUP_HEADS):
            sl = slice(h * hd, (h + 1) * hd)
            s = _dot_nt(q[:, sl], kw[:, sl]) * (hd ** -0.5) + bias_ref[h]
            s = jnp.where(in_range, s, NEG_INF)
            m = jnp.max(s, axis=-1, keepdims=True)
            p = jnp.exp(s - m)
            l = jnp.sum(p, axis=-1, keepdims=True)
            outs.append(_dot(p.astype(BF16), vw[:, sl]) / l)
            lses.append(jnp.broadcast_to(m + jnp.log(l), (qb, hd)))
        o_ref[pl.ds(q0, qb), :] = jnp.concatenate(outs, axis=-1)
        lse_ref[pl.ds(q0, qb), :] = jnp.concatenate(lses, axis=-1)
        return carry

    lax.fori_loop(0, length // qb, block, 0)


def _swa_group(proj, bias_tile, group, dilation):
    batch, seq, cols = proj.shape
    length = seq // dilation
    gw = SWA_GROUP_WIDTH
    blocks_per_row = cols // gw
    base = COL_A // gw + group
    view = proj.reshape(batch, length, dilation * cols)

    def spec(which):
        return pl.BlockSpec((None, length, gw), lambda b, r: (b, 0, r * blocks_per_row + base + 3 * which))

    out_spec = pl.BlockSpec((None, length, gw), lambda b, r: (b, 0, r))
    out_shape = jax.ShapeDtypeStruct((batch, length, dilation * gw), F32)
    o, lse = pl.pallas_call(
        functools.partial(_swa_kernel, length=length),
        grid=(batch, dilation),
        in_specs=[spec(0), spec(1), spec(2),
                  pl.BlockSpec(bias_tile.shape, lambda b, r: (0, 0, 0))],
        out_specs=[out_spec, out_spec],
        out_shape=[out_shape, out_shape],
        scratch_shapes=[pltpu.VMEM((length + 2 * SWA_HALF, gw), BF16)] * 2,
        compiler_params=_params("parallel", "parallel"),
    )(view, view, view, bias_tile)
    return o.reshape(batch, seq, gw), lse.reshape(batch, seq, gw)


def _diff_kernel(q_ref, k_ref, v_ref, seg_ref, dl_ref, g_ref, o_ref, bias_ref, *, lambda_init):
    hd = HEAD_DIM
    tq, seq = bias_ref.shape

    @pl.when(pl.program_id(2) == 0)
    def _():
        seg = jnp.broadcast_to(seg_ref[...], (tq, seq + tq))
        bias_ref[...] = pltpu.roll(seg, seq, 1, stride=1, stride_axis=0)[:, :seq]

    dl = dl_ref[...]
    lam = (jnp.exp(jnp.sum(dl[0:1] * dl[1:2], axis=-1, keepdims=True))
           - jnp.exp(jnp.sum(dl[2:3] * dl[3:4], axis=-1, keepdims=True)) + lambda_init)
    q = q_ref[...]
    k = k_ref[...]
    bias = bias_ref[...]
    probs = []
    for comp in range(2):
        sl = slice(comp * hd, (comp + 1) * hd)
        s = _dot_nt(q[:, sl], k[:, sl]) * (hd ** -0.5) + bias
        p = jnp.exp(s - jnp.max(s, axis=-1, keepdims=True))
        probs.append(p * (1.0 / jnp.sum(p, axis=-1, keepdims=True)))
    attn = probs[0] - lam * probs[1]
    o = _dot(attn.astype(BF16), v_ref[...])
    o = o * lax.rsqrt(jnp.mean(o * o, axis=-1, keepdims=True) + DIFF_SUBLN_EPS) * g_ref[...]
    o_ref[...] = (o * (1.0 - lambda_init)).astype(o_ref.dtype)


def _diff_attention(proj, bias_segs, diff_lambda_l, subln_g, lambda_init):
    batch, seq, _ = proj.shape
    hw = 2 * HEAD_DIM
    tq = min(DIFF_Q_TILE, seq)
    base = COL_C // hw
    return pl.pallas_call(
        functools.partial(_diff_kernel, lambda_init=lambda_init),
        grid=(DIFF_HEADS, seq // tq, batch),
        in_specs=[
            pl.BlockSpec((None, tq, hw), lambda h, i, b: (b, i, base + h)),
            pl.BlockSpec((None, seq, hw), lambda h, i, b: (b, 0, base + DIFF_HEADS + h)),
            pl.BlockSpec((None, seq, hw), lambda h, i, b: (b, 0, base + 2 * DIFF_HEADS + h)),
            pl.BlockSpec((None, None, 1, seq + tq), lambda h, i, b: (h, i, 0, 0)),
            pl.BlockSpec((4, HEAD_DIM), lambda h, i, b: (0, 0)),
            pl.BlockSpec((1, hw), lambda h, i, b: (0, 0)),
        ],
        out_specs=pl.BlockSpec((None, tq, hw), lambda h, i, b: (b, i, h)),
        out_shape=jax.ShapeDtypeStruct((batch, seq, DIFF_WIDTH), BF16),
        scratch_shapes=[pltpu.VMEM((tq, seq), F32)],
        compiler_params=_params("parallel", "parallel", "arbitrary"),
    )(proj, proj, proj, bias_segs, diff_lambda_l, subln_g.reshape(1, hw))


def _shifted(cur_ref, prev_ref, next_ref, mu, first, last):
    x = cur_ref[...].astype(F32)
    rows = x.shape[0]
    halo = prev_ref.shape[0]
    before = jnp.where(first, 0.0, prev_ref[halo - 1:halo, :].astype(F32))
    after = jnp.where(last, 0.0, next_ref[0:1, :].astype(F32))
    row = lax.broadcasted_iota(jnp.int32, (rows, 1), 0)
    prev = jnp.where(row == 0, before, pltpu.roll(x, 1, axis=0))
    nxt = jnp.where(row == rows - 1, after, pltpu.roll(x, rows - 1, axis=0))
    return x + mu[0:1] * (prev - x) + mu[1:2] * (nxt - x)


def _rwkv_prep_kernel(r_ref, rp_ref, rn_ref, k_ref, kp_ref, kn_ref, v_ref, vp_ref, vn_ref,
                      lo_ref, lop_ref, lon_ref, mu_ref, w0_ref, wup_ref, a0_ref, aup_ref, gup_ref,
                      kk_scale_ref, ka_ref, rk_ref, gsum_ref,
                      r_out, v_out, g_out, bonus_out, kk_out, bb_out, kd_out, lw_out):
    w = RWKV_WIDTH
    first = pl.program_id(1) == 0
    last = pl.program_id(1) == pl.num_programs(1) - 1
    mu = mu_ref[...]
    r = _shifted(r_ref, rp_ref, rn_ref, mu[:, 0:w], first, last)
    k = _shifted(k_ref, kp_ref, kn_ref, mu[:, w:2 * w], first, last)
    v = _shifted(v_ref, vp_ref, vn_ref, mu[:, 2 * w:3 * w], first, last)
    lora = _shifted(lo_ref, lop_ref, lon_ref, mu[:, 3 * w:3 * w + RWKV_LORA], first, last)
    decay_in = jnp.tanh(lora[:, 0:128])
    iclr_in = lora[:, 128:256]
    gsum = gsum_ref[...]
    r_out[...] = r
    v_out[...] = v
    g_out[...] = _dot(_sigmoid(lora[:, 256:384]), gup_ref[...], precision=HIGHEST)
    bonus = jnp.zeros_like(r)
    for di in range(2):
        z = w0_ref[di:di + 1, :] + _dot(decay_in, wup_ref[di], precision=HIGHEST)
        u = -z
        softplus = jnp.maximum(u, 0.0) + jnp.log(1.0 + jnp.exp(-jnp.abs(u)))
        lw_out[di] = -jnp.exp(-softplus - 0.5)
        a = _sigmoid(a0_ref[di:di + 1, :] + _dot(iclr_in, aup_ref[di], precision=HIGHEST))
        kk = k * kk_scale_ref[di:di + 1, :]
        kk = kk * lax.rsqrt(jnp.maximum(_dot(kk * kk, gsum, precision=HIGHEST), 1e-24))
        kd = k * (1.0 + (a - 1.0) * ka_ref[di:di + 1, :])
        kk_out[di] = kk
        bb_out[di] = kk * a
        kd_out[di] = kd
        bonus = bonus + _dot(r * kd * rk_ref[...], gsum, precision=HIGHEST) * v
    bonus_out[...] = bonus


def _rwkv_prep(proj, mu, w0, wup2, a0, aup2, g_up, k_k, k_a, r_k, gsum):
    batch, seq, _ = proj.shape
    w = RWKV_WIDTH
    tt = min(256, seq)
    halo = 16
    hb = tt // halo
    last_halo = seq // halo - 1

    def cur(width, blk):
        return pl.BlockSpec((None, tt, width), lambda b, i: (b, i, blk))

    def prev(width, blk):
        return pl.BlockSpec((None, halo, width), lambda b, i: (b, jnp.maximum(i * hb - 1, 0), blk))

    def nxt(width, blk):
        return pl.BlockSpec((None, halo, width), lambda b, i: (b, jnp.minimum((i + 1) * hb, last_halo), blk))

    def full(shape):
        return pl.BlockSpec(shape, lambda b, i: (0,) * len(shape))

    in_specs = []
    for blk in (COL_R // w, COL_R // w + 1, COL_R // w + 2):
        in_specs += [cur(w, blk), prev(w, blk), nxt(w, blk)]
    lb = COL_LORA // RWKV_LORA
    in_specs += [cur(RWKV_LORA, lb), prev(RWKV_LORA, lb), nxt(RWKV_LORA, lb)]
    in_specs += [full(mu.shape), full(w0.shape), full(wup2.shape), full(a0.shape), full(aup2.shape),
                 full(g_up.shape), full(k_k.shape), full(k_a.shape), full((1, w)), full(gsum.shape)]
    shared = pl.BlockSpec((None, tt, w), lambda b, i: (b, i, 0))
    per_dir = pl.BlockSpec((2, None, tt, w), lambda b, i: (0, b, i, 0))
    shared_shape = jax.ShapeDtypeStruct((batch, seq, w), F32)
    per_dir_shape = jax.ShapeDtypeStruct((2, batch, seq, w), F32)
    return pl.pallas_call(
        _rwkv_prep_kernel,
        grid=(batch, seq // tt),
        in_specs=in_specs,
        out_specs=[shared] * 4 + [per_dir] * 4,
        out_shape=[shared_shape] * 4 + [per_dir_shape] * 4,
        compiler_params=_params("parallel", "parallel"),
    )(*([proj] * 12), mu, w0, wup2, a0, aup2, g_up, k_k, k_a, r_k.reshape(1, w), gsum)


def _rwkv_scan_kernel(r_ref, v_ref, kk_ref, bb_ref, kd_ref, lw_ref, y_ref, state_ref):
    chunk, hd = RWKV_CHUNK, HEAD_DIM
    forward = pl.program_id(1) == 0

    @pl.when(pl.program_id(2) == 0)
    def _():
        state_ref[...] = jnp.zeros_like(state_ref)

    ti = lax.broadcasted_iota(jnp.int32, (chunk, chunk), 0)
    tj = lax.broadcasted_iota(jnp.int32, (chunk, chunk), 1)
    lag = jnp.where(forward, ti - tj, tj - ti)
    before = lag > 0
    upto = lag >= 0
    eye = (ti == tj).astype(F32)
    same16 = (ti // 16) == (tj // 16)
    same32 = (ti // 32) == (tj // 32)

    lw = lw_ref[...]
    cum = _dot(upto.astype(F32), lw, precision=HIGHEST)
    total = jnp.sum(lw, axis=0, keepdims=True)
    p_incl = jnp.exp(cum)
    p_excl = jnp.exp(cum - lw)
    p_inv = jnp.exp(-cum)
    p_rest = jnp.exp(total - cum)
    p_total = jnp.exp(total)
    kk, bb, kd = kk_ref[...], bb_ref[...], kd_ref[...]
    a_all = -kk * p_excl
    b_all = bb * p_inv
    k_all = kd * p_inv
    r_all = r_ref[...] * p_incl
    b_end = bb * p_rest
    k_end = kd * p_rest
    v_all = v_ref[...]

    outs = []
    for h in range(RWKV_HEADS):
        sl = slice(h * hd, (h + 1) * hd)
        a_t, b_t, k_t, r_t, v_t = a_all[:, sl], b_all[:, sl], k_all[:, sl], r_all[:, sl], v_all[:, sl]
        m_ab = jnp.where(before, _dot_nt(a_t, b_t, HIGHEST), 0.0)
        m_ak = jnp.where(before, _dot_nt(a_t, k_t, HIGHEST), 0.0)
        m_rb = jnp.where(upto, _dot_nt(r_t, b_t, HIGHEST), 0.0)
        m_rk = jnp.where(upto, _dot_nt(r_t, k_t, HIGHEST), 0.0)
        x = jnp.where(same16, m_ab, 0.0)
        inv = eye + x
        for _ in range(3):
            x = _dot(x, x, precision=HIGHEST)
            inv = inv + _dot(inv, x, precision=HIGHEST)
        for off in (jnp.where(same32 & ~same16, m_ab, 0.0), jnp.where(~same32, m_ab, 0.0)):
            inv = inv + _dot(inv, _dot(off, inv, precision=HIGHEST), precision=HIGHEST)
        state = state_ref[h]
        rhs = _dot_nt(a_t, state, HIGHEST) + _dot(m_ak, v_t, precision=HIGHEST)
        u = _dot(inv, rhs, precision=HIGHEST)
        outs.append(_dot_nt(r_t, state, HIGHEST) + _dot(m_rb, u, precision=HIGHEST)
                    + _dot(m_rk, v_t, precision=HIGHEST))
        state_ref[h] = (state * p_total[:, sl] + _dot_tn(u, b_end[:, sl], HIGHEST)
                        + _dot_tn(v_t, k_end[:, sl], HIGHEST))
    y_ref[...] = jnp.concatenate(outs, axis=-1)


def _rwkv_scan(r, v, kk, bb, kd, lw):
    batch, seq, w = r.shape
    chunk = RWKV_CHUNK
    nc = seq // chunk

    def cidx(d, c):
        return jnp.where(d == 0, c, nc - 1 - c)

    shared = pl.BlockSpec((None, chunk, w), lambda b, d, c: (b, cidx(d, c), 0))
    per_dir = pl.BlockSpec((None, None, chunk, w), lambda b, d, c: (d, b, cidx(d, c), 0))
    return pl.pallas_call(
        _rwkv_scan_kernel,
        grid=(batch, 2, nc),
        in_specs=[shared, shared, per_dir, per_dir, per_dir, per_dir],
        out_specs=per_dir,
        out_shape=jax.ShapeDtypeStruct((2, batch, seq, w), F32),
        scratch_shapes=[pltpu.VMEM((RWKV_HEADS, HEAD_DIM, HEAD_DIM), F32)],
        compiler_params=_params("parallel", "parallel", "arbitrary"),
    )(r, v, kk, bb, kd, lw)


def _rwkv_post_kernel(y_ref, bonus_ref, g_ref, lng_ref, lnb_ref, gsum_ref, o_ref):
    gsum = gsum_ref[...]
    y = y_ref[0] + y_ref[1]
    mean = _dot(y, gsum, precision=HIGHEST) * (1.0 / HEAD_DIM)
    yc = y - mean
    var = _dot(yc * yc, gsum, precision=HIGHEST) * (1.0 / HEAD_DIM)
    yn = yc * lax.rsqrt(var + RWKV_GN_EPS) * lng_ref[...] + lnb_ref[...]
    o_ref[...] = ((yn + bonus_ref[...]) * g_ref[...]).astype(o_ref.dtype)


def _rwkv_post(y, bonus, g, ln_g, ln_b, gsum):
    _, batch, seq, w = y.shape
    tt = min(512, seq)
    shared = pl.BlockSpec((None, tt, w), lambda b, i: (b, i, 0))
    row = pl.BlockSpec((1, w), lambda b, i: (0, 0))
    return pl.pallas_call(
        _rwkv_post_kernel,
        grid=(batch, seq // tt),
        in_specs=[pl.BlockSpec((2, None, tt, w), lambda b, i: (0, b, i, 0)), shared, shared, row, row,
                  pl.BlockSpec(gsum.shape, lambda b, i: (0, 0))],
        out_specs=shared,
        out_shape=jax.ShapeDtypeStruct((batch, seq, w), BF16),
        compiler_params=_params("parallel", "parallel"),
    )(y, bonus, g, ln_g.reshape(1, w), ln_b.reshape(1, w), gsum)


def _merge_kernel(ga_ref, gb_ref, gc_ref, oa0_ref, oa1_ref, oa2_ref, la0_ref, la1_ref, la2_ref,
                  ob_ref, oc_ref, x_ref, wa_ref, wb_ref, wc_ref, wo_ref, mod_ref, g2_ref, rw_ref, rb_ref,
                  x_out, h_out, idx_out, prob_out):
    lses = [la0_ref[...], la1_ref[...], la2_ref[...]]
    m = jnp.maximum(jnp.maximum(lses[0], lses[1]), lses[2])
    es = [jnp.exp(l - m) for l in lses]
    inv = 1.0 / (es[0] + es[1] + es[2])
    o_a = (es[0] * oa0_ref[...] + es[1] * oa1_ref[...] + es[2] * oa2_ref[...]) * inv
    merged = (_sigmoid(ga_ref[...].astype(F32)) * _dot(o_a.astype(BF16), wa_ref[...])
              + _sigmoid(gb_ref[...].astype(F32)) * _dot(ob_ref[...], wb_ref[...])
              + _sigmoid(gc_ref[...].astype(F32)) * _dot(oc_ref[...], wc_ref[...]))
    x = x_ref[...] + mod_ref[2] * _dot(merged.astype(BF16), wo_ref[...])
    x_out[...] = x
    h = _modulated_norm(x, g2_ref[...], mod_ref[4], mod_ref[3])
    h_out[...] = h.astype(h_out.dtype)

    logits = _dot(h, rw_ref[...], precision=HIGHEST) + rb_ref[...]
    lane = lax.broadcasted_iota(jnp.int32, logits.shape, 1)
    work = logits
    vals, idxs = [], []
    for _ in range(TOP_K):
        top = jnp.max(work, axis=-1, keepdims=True)
        first = jnp.min(jnp.where(work == top, lane, N_EXPERTS), axis=-1, keepdims=True)
        vals.append(top)
        idxs.append(first)
        work = jnp.where(lane == first, -jnp.inf, work)
    exps = [jnp.exp(t - vals[0]) for t in vals]
    denom = exps[0] + exps[1] + exps[2] + exps[3]
    idx_out[...] = jnp.concatenate(idxs, axis=-1)
    prob_out[...] = jnp.concatenate([e / denom for e in exps], axis=-1)


def _merge(proj, swa_outs, o_b, o_c, x, wa, wb, wc, wo, mod_l, g2, router_w, router_b):
    batch, seq, d = x.shape
    tm = min(512, seq)
    gw = SWA_GROUP_WIDTH

    def rows(width, blk=0):
        return pl.BlockSpec((None, tm, width), lambda b, i: (b, i, blk))

    def full(shape):
        return pl.BlockSpec(shape, lambda b, i: (0,) * len(shape))

    o_list = [o for o, _ in swa_outs]
    l_list = [l for _, l in swa_outs]
    return pl.pallas_call(
        _merge_kernel,
        grid=(batch, seq // tm),
        in_specs=[rows(d, 0), rows(d, 1), rows(d, 2)] + [rows(gw)] * 6
        + [rows(RWKV_WIDTH), rows(DIFF_WIDTH), rows(d), full(wa.shape), full(wb.shape), full(wc.shape),
           full(wo.shape), pl.BlockSpec((None, 6, 1, d), lambda b, i: (b, 0, 0, 0)), full((1, d)),
           full(router_w.shape), full((1, N_EXPERTS))],
        out_specs=[rows(d), rows(d), rows(TOP_K), rows(TOP_K)],
        out_shape=[jax.ShapeDtypeStruct((batch, seq, d), F32), jax.ShapeDtypeStruct((batch, seq, d), BF16),
                   jax.ShapeDtypeStruct((batch, seq, TOP_K), jnp.int32),
                   jax.ShapeDtypeStruct((batch, seq, TOP_K), F32)],
        compiler_params=_params("parallel", "parallel"),
    )(proj, proj, proj, *o_list, *l_list, o_b, o_c, x, wa, wb, wc, wo, mod_l, g2.reshape(1, d),
      router_w, router_b.reshape(1, N_EXPERTS))


def _expert_kernel(tile_expert_ref, tile_valid_ref, x_ref, w1_ref, b1_ref, w2_ref, b2_ref, o_ref):
    t = pl.program_id(0)

    @pl.when(tile_valid_ref[t] != 0)
    def _():
        x = x_ref[...]
        glu = jnp.minimum(_dot(x, w1_ref[0]) + b1_ref[0], SWIGLU_LIMIT)
        lin = jnp.clip(_dot(x, w1_ref[1]) + b1_ref[1], -SWIGLU_LIMIT, SWIGLU_LIMIT)
        act = glu * _sigmoid(SWIGLU_ALPHA * glu) * (lin + 1.0)
        o_ref[...] = _dot(act.astype(BF16), w2_ref[...]) + b2_ref[...]

    @pl.when(tile_valid_ref[t] == 0)
    def _():
        o_ref[...] = jnp.zeros_like(o_ref)


def _experts(xg, tile_expert, tile_valid, w1, b1, w2, b2, layer):
    rows, d = xg.shape
    f = w2.shape[2]
    tm = MOE_ROW_TILE
    grid_spec = pltpu.PrefetchScalarGridSpec(
        num_scalar_prefetch=2,
        grid=(rows // tm,),
        in_specs=[
            pl.BlockSpec((tm, d), lambda t, te, tv: (t, 0)),
            pl.BlockSpec((None, None, 2, d, f), lambda t, te, tv: (layer, te[t], 0, 0, 0)),
            pl.BlockSpec((None, None, 2, 1, f), lambda t, te, tv: (layer, te[t], 0, 0, 0)),
            pl.BlockSpec((None, None, f, d), lambda t, te, tv: (layer, te[t], 0, 0)),
            pl.BlockSpec((None, None, 1, d), lambda t, te, tv: (layer, te[t], 0, 0)),
        ],
        out_specs=pl.BlockSpec((tm, d), lambda t, te, tv: (t, 0)),
    )
    return pl.pallas_call(
        _expert_kernel,
        grid_spec=grid_spec,
        out_shape=jax.ShapeDtypeStruct((rows, d), F32),
        compiler_params=_params("arbitrary"),
    )(tile_expert, tile_valid, xg, w1, b1, w2, b2)


def _combine_kernel(y_ref, p_ref, x_ref, mod_ref, g_ref, o_ref, *, final):
    p = p_ref[...]
    acc = p[:, 0:1] * y_ref[0]
    for j in range(1, TOP_K):
        acc = acc + p[:, j:j + 1] * y_ref[j]
    x = x_ref[...] + mod_ref[5] * acc
    if final:
        x = x * lax.rsqrt(jnp.mean(x * x, axis=-1, keepdims=True) + NORM_EPS) * g_ref[...]
    o_ref[...] = x


def _combine(y4, probs, x, mod_l, final_g, final):
    batch, seq, d = x.shape
    tm = min(512, seq)
    return pl.pallas_call(
        functools.partial(_combine_kernel, final=final),
        grid=(batch, seq // tm),
        in_specs=[
            pl.BlockSpec((TOP_K, None, tm, d), lambda b, i: (0, b, i, 0)),
            pl.BlockSpec((None, tm, TOP_K), lambda b, i: (b, i, 0)),
            pl.BlockSpec((None, tm, d), lambda b, i: (b, i, 0)),
            pl.BlockSpec((None, 6, 1, d), lambda b, i: (b, 0, 0, 0)),
            pl.BlockSpec((1, d), lambda b, i: (0, 0)),
        ],
        out_specs=pl.BlockSpec((None, tm, d), lambda b, i: (b, i, 0)),
        out_shape=jax.ShapeDtypeStruct((batch, seq, d), F32),
        compiler_params=_params("parallel", "parallel"),
    )(y4, probs, x, mod_l, final_g.reshape(1, d))


def _dispatch_plan(idx, n_rows_padded):
    tm = MOE_ROW_TILE
    flat = idx.reshape(-1)
    n_slots = flat.shape[0]
    order = jnp.argsort(flat, stable=True).astype(jnp.int32)
    sorted_expert = flat[order]
    counts = jnp.zeros((N_EXPERTS,), jnp.int32).at[flat].add(1)
    padded = ((counts + tm - 1) // tm) * tm
    padded_end = jnp.cumsum(padded)
    padded_start = padded_end - padded
    start = jnp.cumsum(counts) - counts
    dest = padded_start[sorted_expert] + jnp.arange(n_slots, dtype=jnp.int32) - start[sorted_expert]
    src_token = jnp.zeros((n_rows_padded,), jnp.int32).at[dest].set(order // TOP_K)
    slot_row = jnp.zeros((n_slots,), jnp.int32).at[order].set(dest)
    tile_start = jnp.arange(n_rows_padded // tm, dtype=jnp.int32) * tm
    tile_expert = jnp.minimum(jnp.searchsorted(padded_end, tile_start, side="right"), N_EXPERTS - 1)
    tile_valid = (tile_start < padded_end[-1]).astype(jnp.int32)
    return src_token, slot_row, tile_expert.astype(jnp.int32), tile_valid


def _moe(h2, idx, probs, x, mod_l, w1, b1, w2, b2, layer, final_g, final):
    batch, seq, d = x.shape
    n_tok = batch * seq
    n_rows_padded = n_tok * TOP_K + N_EXPERTS * MOE_ROW_TILE
    src_token, slot_row, tile_expert, tile_valid = _dispatch_plan(idx, n_rows_padded)
    xg = jnp.take(h2.reshape(n_tok, d), src_token, axis=0)
    y = _experts(xg, tile_expert, tile_valid, w1, b1, w2, b2, layer)
    rows4 = slot_row.reshape(n_tok, TOP_K).T
    y4 = jnp.take(y, rows4, axis=0).reshape(TOP_K, batch, seq, d)
    return _combine(y4, probs, x, mod_l, final_g, final)


def _t5_bucket(rel):
    nb = REL_BUCKETS // 2
    max_exact = nb // 2
    ret = jnp.where(rel > 0, nb, 0)
    n = jnp.abs(rel)
    nf = jnp.maximum(n, 1).astype(F32)
    large = max_exact + (jnp.log(nf / max_exact) / math.log(REL_MAX_DIST / max_exact)
                         * (nb - max_exact)).astype(jnp.int32)
    large = jnp.minimum(large, nb - 1)
    return ret + jnp.where(n < max_exact, n, large)


def _swa_bias_tiles(rel_bias):
    half, qb = SWA_HALF, SWA_QUERY_BLOCK
    a = jnp.arange(qb)[:, None]
    c = jnp.arange(qb + 2 * half)[None, :]
    j = c - half - a
    tiles = []
    for g, (_, dil) in enumerate(SWA_PATTERNS):
        offs = jnp.arange(-half, half + 1) * dil
        table = rel_bias[_t5_bucket(offs)][:, g * SWA_GROUP_HEADS:(g + 1) * SWA_GROUP_HEADS].T
        tile = table[:, jnp.clip(j + half, 0, 2 * half)]
        tiles.append(jnp.where((jnp.abs(j) <= half)[None], tile, NEG_INF).astype(F32))
    return tiles


def _diff_bias_segments(rel_bias, seq):
    tq = min(DIFF_Q_TILE, seq)
    table = rel_bias[:, len(SWA_PATTERNS) * SWA_GROUP_HEADS:].T
    by_rel = table[:, _t5_bucket(jnp.arange(2 * seq) - seq)]
    segs = [by_rel[:, seq - (i + 1) * tq:2 * seq - i * tq] for i in range(seq // tq)]
    return jnp.stack(segs, axis=1)[:, :, None, :].astype(F32)


def _pack_w_in(w_in_l):
    a_w = 3 * SWA_WIDTH
    b_w = 3 * RWKV_WIDTH + RWKV_LORA
    c_w = 3 * DIFF_WIDTH
    a, b, c, gates = (w_in_l[:, :a_w], w_in_l[:, a_w:a_w + b_w], w_in_l[:, a_w + b_w:a_w + b_w + c_w],
                      w_in_l[:, a_w + b_w + c_w:])
    pad = jnp.zeros((w_in_l.shape[0], PROJ_COLS_PADDED - PROJ_COLS), w_in_l.dtype)
    return jnp.concatenate([gates, a, c, b, pad], axis=1).astype(BF16)


def _direction_padded(w_up):
    z = jnp.zeros_like(w_up[0])
    return jnp.stack([jnp.concatenate([w_up[0], z], axis=0), jnp.concatenate([z, w_up[1]], axis=0)])


def kernel(x, c, w_mod, b_mod, norm1_g, norm2_g, w_in, rwkv_mu, rwkv_w0, rwkv_w_up, rwkv_a0, rwkv_a_up, rwkv_g_up, rwkv_k_k, rwkv_k_a, rwkv_r_k, rwkv_ln_g, rwkv_ln_b, diff_lambda, diff_subln_g, rel_bias, w_branch_a, w_branch_b, w_branch_c, w_out, router_w, router_b, moe_w1, moe_b1, moe_w2, moe_b2, final_norm_g):
    batch, seq, d = x.shape
    depth = w_mod.shape[0]
    mod = _modulation(c, w_mod, b_mod)
    swa_tiles = _swa_bias_tiles(rel_bias)
    bias_segs = _diff_bias_segments(rel_bias, seq)
    head_of = jnp.arange(RWKV_WIDTH) // HEAD_DIM
    gsum = (head_of[:, None] == head_of[None, :]).astype(F32)
    w1 = jnp.stack([moe_w1[..., 0::2], moe_w1[..., 1::2]], axis=2).astype(BF16)
    b1 = jnp.stack([moe_b1[..., 0::2], moe_b1[..., 1::2]], axis=2)[:, :, :, None, :]
    w2 = moe_w2.astype(BF16)
    b2 = moe_b2[:, :, None, :]

    for l in range(depth):
        mod_l = mod[l]
        proj = _norm_proj(x, norm1_g[l], mod_l, _pack_w_in(w_in[l]))
        swa_outs = [_swa_group(proj, swa_tiles[g], g, dil) for g, (_, dil) in enumerate(SWA_PATTERNS)]
        r, v, g, bonus, kk, bb, kd, lw = _rwkv_prep(
            proj, rwkv_mu[l], rwkv_w0[l], _direction_padded(rwkv_w_up[l]), rwkv_a0[l],
            _direction_padded(rwkv_a_up[l]), rwkv_g_up[l], rwkv_k_k[l], rwkv_k_a[l], rwkv_r_k[l], gsum)
        y = _rwkv_scan(r, v, kk, bb, kd, lw)
        o_b = _rwkv_post(y, bonus, g, rwkv_ln_g[l], rwkv_ln_b[l], gsum)
        lambda_init = 0.8 - 0.6 * math.exp(-0.3 * l)
        o_c = _diff_attention(proj, bias_segs, diff_lambda[l], diff_subln_g[l], lambda_init)
        x, h2, idx, probs = _merge(
            proj, swa_outs, o_b, o_c, x, w_branch_a[l].astype(BF16), w_branch_b[l].astype(BF16),
            w_branch_c[l].astype(BF16), w_out[l].astype(BF16), mod_l, norm2_g[l], router_w[l], router_b[l])
        x = _moe(h2, idx, probs, x, mod_l, w1, b1, w2, b2, l, final_norm_g, l == depth - 1)
    return x
```

```python
import functools
import math

import jax
import jax.numpy as jnp
from jax import lax
from jax.experimental import pallas as pl
from jax.experimental.pallas import tpu as pltpu

F32 = jnp.float32
BF16 = jnp.bfloat16
HIGHEST = lax.Precision.HIGHEST

D_MODEL = 1024
HEAD_DIM = 64
LANES = 128
NORM_EPS = 1e-6
NEG_INF = -1e30

SWA_PATTERNS = ((128, 1), (512, 4), (2048, 16))
SWA_GROUP_HEADS = 4
SWA_GROUP_WIDTH = SWA_GROUP_HEADS * HEAD_DIM
SWA_WIDTH = len(SWA_PATTERNS) * SWA_GROUP_WIDTH
SWA_HALF = 64
SWA_QUERY_BLOCK = 128

RWKV_HEADS = 12
RWKV_WIDTH = RWKV_HEADS * HEAD_DIM
RWKV_LORA = 384
RWKV_GN_EPS = 64e-5
RWKV_CHUNK = 64
RWKV_PACK = 4

DIFF_HEADS = 6
DIFF_WIDTH = DIFF_HEADS * 2 * HEAD_DIM
DIFF_SUBLN_EPS = 1e-5
DIFF_Q_TILE = 256

REL_BUCKETS = 32
REL_MAX_DIST = 128

N_EXPERTS = 32
TOP_K = 4
D_EXPERT = 1024
SWIGLU_LIMIT = 7.0
SWIGLU_ALPHA = 1.702
MOE_ROW_TILE = 256

COL_GATES = 0
COL_A_DILATED = 3 * D_MODEL
COL_A0 = COL_A_DILATED + 2 * SWA_WIDTH
COL_C = COL_A0 + SWA_WIDTH
COL_R = COL_C + 3 * DIFF_WIDTH
COL_LORA = COL_R + 3 * RWKV_WIDTH
PROJ_COLS = COL_LORA + RWKV_LORA
PROJ_COLS_PADDED = 10752
PROJ_COL_TILE = 1536
STRIDED_COL_TILE = COL_A_DILATED // PROJ_COL_TILE
PROJ_ROW_TILE = 1024

VMEM_LIMIT = 48 * 1024 * 1024

_NT = ((1,), (1,))
_TN = ((0,), (0,))


def _params(*sem):
    return pltpu.CompilerParams(dimension_semantics=sem, vmem_limit_bytes=VMEM_LIMIT)


def _sigmoid(x):
    return 1.0 / (1.0 + jnp.exp(-x))


def _dot(a, b, dims=((1,), (0,)), precision=None):
    return lax.dot_general(a, b, (dims, ((), ())), precision=precision, preferred_element_type=F32)


def _hi_lo(x):
    hi = x.astype(BF16)
    return hi, (x - hi.astype(F32)).astype(BF16)


def _dot3(a, b):
    a_hi, a_lo = _hi_lo(a)
    b_hi, b_lo = _hi_lo(b)
    return _dot(a_hi, b_hi) + _dot(a_hi, b_lo) + _dot(a_lo, b_hi)


def _head_sum(x, ones_bf16):
    hi, lo = _hi_lo(x)
    return _dot(hi, ones_bf16) + _dot(lo, ones_bf16)


def _mod_kernel(c_ref, w_ref, b_ref, o_ref):
    c = c_ref[...]
    cond = c * _sigmoid(c)
    o_ref[...] = _dot(cond, w_ref[...], precision=HIGHEST) + b_ref[...]


def _modulation(c, w_mod, b_mod):
    n_layers, d, n = w_mod.shape
    batch = c.shape[0]
    tn = 1536
    out = pl.pallas_call(
        _mod_kernel,
        name="modulation",
        grid=(n_layers, n // tn),
        in_specs=[
            pl.BlockSpec((batch, d), lambda l, j: (0, 0)),
            pl.BlockSpec((None, d, tn), lambda l, j: (l, 0, j)),
            pl.BlockSpec((None, 1, tn), lambda l, j: (l, 0, j)),
        ],
        out_specs=pl.BlockSpec((None, batch, tn), lambda l, j: (l, 0, j)),
        out_shape=jax.ShapeDtypeStruct((n_layers, batch, n), F32),
        compiler_params=_params("parallel", "parallel"),
    )(c, w_mod, b_mod.reshape(n_layers, 1, n))
    return out.reshape(n_layers, batch, 6, 1, d)


def _modulated_norm(x, g, scale, shift):
    y = x * lax.rsqrt(jnp.mean(x * x, axis=-1, keepdims=True) + NORM_EPS) * g
    return y * (1.0 + scale) + shift


def _norm_proj_kernel(x_ref, g_ref, mod_ref, w_ref, o_ref, res4_ref, res16_ref, h_ref, acc_ref):
    j = pl.program_id(2)

    @pl.when(j == 0)
    def _():
        h_ref[...] = _modulated_norm(x_ref[...], g_ref[...], mod_ref[1], mod_ref[0]).astype(BF16)

    @pl.when(j != STRIDED_COL_TILE)
    def _():
        o_ref[...] = _dot(h_ref[...], w_ref[...]).astype(o_ref.dtype)

    @pl.when(j == STRIDED_COL_TILE)
    def _():
        acc = _dot(h_ref[...], w_ref[...])
        o_ref[...] = acc.astype(o_ref.dtype)
        lanes = acc_ref.shape[2]
        tm = acc_ref.shape[1]
        for cb in range(acc_ref.shape[0]):
            acc_ref[cb] = acc[:, cb * lanes:(cb + 1) * lanes]
        per_group = SWA_WIDTH // lanes
        for res_ref, dil, first in ((res4_ref, SWA_PATTERNS[1][1], 0), (res16_ref, SWA_PATTERNS[2][1], per_group)):
            for r in range(dil):
                for cb in range(per_group):
                    res_ref[r, :, cb * lanes:(cb + 1) * lanes] = (
                        acc_ref[first + cb, pl.ds(r, tm // dil, stride=dil), :].astype(res_ref.dtype))


def _norm_proj(x, g, mod_l, w):
    batch, seq, d = x.shape
    n = w.shape[1]
    tm, tn = min(PROJ_ROW_TILE, seq), PROJ_COL_TILE
    d1, d2 = SWA_PATTERNS[1][1], SWA_PATTERNS[2][1]

    def res_spec(dil):
        return pl.BlockSpec((None, dil, tm // dil, SWA_WIDTH), lambda b, i, j: (b, 0, i, 0))

    return pl.pallas_call(
        _norm_proj_kernel,
        name="norm_proj",
        grid=(batch, seq // tm, n // tn),
        in_specs=[
            pl.BlockSpec((None, tm, d), lambda b, i, j: (b, i, 0)),
            pl.BlockSpec((1, d), lambda b, i, j: (0, 0)),
            pl.BlockSpec((None, 6, 1, d), lambda b, i, j: (b, 0, 0, 0)),
            pl.BlockSpec((d, tn), lambda b, i, j: (0, j)),
        ],
        out_specs=[pl.BlockSpec((None, tm, tn), lambda b, i, j: (b, i, j)), res_spec(d1), res_spec(d2)],
        out_shape=[jax.ShapeDtypeStruct((batch, seq, n), BF16),
                   jax.ShapeDtypeStruct((batch, d1, seq // d1, SWA_WIDTH), BF16),
                   jax.ShapeDtypeStruct((batch, d2, seq // d2, SWA_WIDTH), BF16)],
        scratch_shapes=[pltpu.VMEM((tm, d), BF16), pltpu.VMEM((tn // LANES, tm, LANES), F32)],
        compiler_params=_params("parallel", "parallel", "arbitrary"),
    )(x, g.reshape(1, d), mod_l, w)


def _swa_kernel(q_ref, k_ref, v_ref, bias_ref, o_ref, lse_ref, kpad_ref, vpad_ref, *, length):
    half, qb, hd = SWA_HALF, SWA_QUERY_BLOCK, HEAD_DIM
    win = qb + 2 * half
    zeros = jnp.zeros((half, SWA_GROUP_WIDTH), BF16)
    for pad_ref, src_ref in ((kpad_ref, k_ref), (vpad_ref, v_ref)):
        pad_ref[0:half, :] = zeros
        pad_ref[half + length:half + length + half, :] = zeros
        pad_ref[half:half + length, :] = src_ref[...]

    def block(n, carry):
        q0 = pl.multiple_of(n * qb, qb)
        q = q_ref[pl.ds(q0, qb), :]
        kw = kpad_ref[pl.ds(q0, win), :]
        vw = vpad_ref[pl.ds(q0, win), :]
        key_pos = q0 - half + lax.broadcasted_iota(jnp.int32, (1, win), 1)
        in_range = (key_pos >= 0) & (key_pos < length)
        outs, lses = [], []
        for h in range(SWA_GROUP_HEADS):
            sl = slice(h * hd, (h + 1) * hd)
            s = _dot(q[:, sl], kw[:, sl], _NT) * (hd ** -0.5) + bias_ref[h]
            s = jnp.where(in_range, s, NEG_INF)
            m = jnp.max(s, axis=-1, keepdims=True)
            p = jnp.exp(s - m)
            l = jnp.sum(p, axis=-1, keepdims=True)
            outs.append(_dot(p.astype(BF16), vw[:, sl]) / l)
            lses.append(jnp.broadcast_to(m + jnp.log(l), (qb, hd)))
        o_ref[pl.ds(q0, qb), :] = jnp.concatenate(outs, axis=-1)
        lse_ref[pl.ds(q0, qb), :] = jnp.concatenate(lses, axis=-1)
        return carry

    lax.fori_loop(0, length // qb, block, 0)


def _swa_group(qkv, bias_tile, dilation, col_block):
    gw = SWA_GROUP_WIDTH
    if dilation == 1:
        batch, length, _ = qkv.shape
        grid = (batch, 1)
        block = (None, length, gw)
        out_dims = (batch, length, gw)

        def at(col):
            return lambda b, r: (b, 0, col)
    else:
        batch, _, length, _ = qkv.shape
        grid = (batch, dilation)
        block = (None, None, length, gw)
        out_dims = (batch, dilation, length, gw)

        def at(col):
            return lambda b, r: (b, r, 0, col)

    out_spec = pl.BlockSpec(block, at(0))
    out_shape = jax.ShapeDtypeStruct(out_dims, F32)
    return pl.pallas_call(
        functools.partial(_swa_kernel, length=length),
        name="swa",
        grid=grid,
        in_specs=[pl.BlockSpec(block, at(col_block + which)) for which in range(3)]
        + [pl.BlockSpec(bias_tile.shape, lambda b, r: (0, 0, 0))],
        out_specs=[out_spec, out_spec],
        out_shape=[out_shape, out_shape],
        scratch_shapes=[pltpu.VMEM((length + 2 * SWA_HALF, gw), BF16)] * 2,
        compiler_params=_params("parallel", "parallel"),
    )(qkv, qkv, qkv, bias_tile)


def _diff_kernel(q_ref, k_ref, v_ref, seg_ref, dl_ref, g_ref, o_ref, bias_ref, *, lambda_init):
    hd = HEAD_DIM
    tq, seq = bias_ref.shape

    @pl.when(pl.program_id(2) == 0)
    def _():
        seg = jnp.broadcast_to(seg_ref[...], (tq, seq + tq))
        bias_ref[...] = pltpu.roll(seg, seq, 1, stride=1, stride_axis=0)[:, :seq]

    dl = dl_ref[...]
    lam = (jnp.exp(jnp.sum(dl[0:1] * dl[1:2], axis=-1, keepdims=True))
           - jnp.exp(jnp.sum(dl[2:3] * dl[3:4], axis=-1, keepdims=True)) + lambda_init)
    q = q_ref[...]
    k = k_ref[...]
    bias = bias_ref[...]
    probs = []
    for comp in range(2):
        sl = slice(comp * hd, (comp + 1) * hd)
        s = _dot(q[:, sl], k[:, sl], _NT) * (hd ** -0.5) + bias
        p = jnp.exp(s - jnp.max(s, axis=-1, keepdims=True))
        probs.append(p * (1.0 / jnp.sum(p, axis=-1, keepdims=True)))
    attn = probs[0] - lam * probs[1]
    o = _dot(attn.astype(BF16), v_ref[...])
    o = o * lax.rsqrt(jnp.mean(o * o, axis=-1, keepdims=True) + DIFF_SUBLN_EPS) * g_ref[...]
    o_ref[...] = (o * (1.0 - lambda_init)).astype(o_ref.dtype)


def _diff_attention(proj, bias_segs, diff_lambda_l, subln_g, lambda_init):
    batch, seq, _ = proj.shape
    hw = 2 * HEAD_DIM
    tq = min(DIFF_Q_TILE, seq)
    base = COL_C // hw
    return pl.pallas_call(
        functools.partial(_diff_kernel, lambda_init=lambda_init),
        name="diff_attn",
        grid=(DIFF_HEADS, seq // tq, batch),
        in_specs=[
            pl.BlockSpec((None, tq, hw), lambda h, i, b: (b, i, base + h)),
            pl.BlockSpec((None, seq, hw), lambda h, i, b: (b, 0, base + DIFF_HEADS + h)),
            pl.BlockSpec((None, seq, hw), lambda h, i, b: (b, 0, base + 2 * DIFF_HEADS + h)),
            pl.BlockSpec((None, None, 1, seq + tq), lambda h, i, b: (h, i, 0, 0)),
            pl.BlockSpec((4, HEAD_DIM), lambda h, i, b: (0, 0)),
            pl.BlockSpec((1, hw), lambda h, i, b: (0, 0)),
        ],
        out_specs=pl.BlockSpec((None, tq, hw), lambda h, i, b: (b, i, h)),
        out_shape=jax.ShapeDtypeStruct((batch, seq, DIFF_WIDTH), BF16),
        scratch_shapes=[pltpu.VMEM((tq, seq), F32)],
        compiler_params=_params("parallel", "parallel", "arbitrary"),
    )(proj, proj, proj, bias_segs, diff_lambda_l, subln_g.reshape(1, hw))


def _shifted(cur_ref, prev_ref, next_ref, mu, first, last):
    x = cur_ref[...].astype(F32)
    rows = x.shape[0]
    halo = prev_ref.shape[0]
    before = jnp.where(first, 0.0, prev_ref[halo - 1:halo, :].astype(F32))
    after = jnp.where(last, 0.0, next_ref[0:1, :].astype(F32))
    row = lax.broadcasted_iota(jnp.int32, (rows, 1), 0)
    prev = jnp.where(row == 0, before, pltpu.roll(x, 1, axis=0))
    nxt = jnp.where(row == rows - 1, after, pltpu.roll(x, rows - 1, axis=0))
    return x + mu[0:1] * (prev - x) + mu[1:2] * (nxt - x)


def _rwkv_prep_kernel(r_ref, rp_ref, rn_ref, k_ref, kp_ref, kn_ref, v_ref, vp_ref, vn_ref,
                      lo_ref, lop_ref, lon_ref, mu_ref, w0_ref, wup_ref, a0_ref, aup_ref, gup_ref,
                      kk_scale_ref, ka_ref, rk_ref, gsum_ref,
                      r_out, v_out, g_out, bonus_out, kk_out, bb_out, kd_out, lw_out):
    w = RWKV_WIDTH
    first = pl.program_id(1) == 0
    last = pl.program_id(1) == pl.num_programs(1) - 1
    mu = mu_ref[...]
    r = _shifted(r_ref, rp_ref, rn_ref, mu[:, 0:w], first, last)
    k = _shifted(k_ref, kp_ref, kn_ref, mu[:, w:2 * w], first, last)
    v = _shifted(v_ref, vp_ref, vn_ref, mu[:, 2 * w:3 * w], first, last)
    lora = _shifted(lo_ref, lop_ref, lon_ref, mu[:, 3 * w:3 * w + RWKV_LORA], first, last)
    decay_in = jnp.tanh(lora[:, 0:128])
    iclr_in = lora[:, 128:256]
    gsum = gsum_ref[...]
    r_out[...] = r
    v_out[...] = v
    g_out[...] = _dot3(_sigmoid(lora[:, 256:384]), gup_ref[...])
    bonus = jnp.zeros_like(r)
    for di in range(2):
        z = w0_ref[di:di + 1, :] + _dot3(decay_in, wup_ref[di])
        u = -z
        softplus = jnp.maximum(u, 0.0) + jnp.log(1.0 + jnp.exp(-jnp.abs(u)))
        lw_out[di] = -jnp.exp(-softplus - 0.5)
        a = _sigmoid(a0_ref[di:di + 1, :] + _dot3(iclr_in, aup_ref[di]))
        kk = k * kk_scale_ref[di:di + 1, :]
        kk = kk * lax.rsqrt(jnp.maximum(_head_sum(kk * kk, gsum), 1e-24))
        kd = k * (1.0 + (a - 1.0) * ka_ref[di:di + 1, :])
        kk_out[di] = kk
        bb_out[di] = kk * a
        kd_out[di] = kd
        bonus = bonus + _head_sum(r * kd * rk_ref[...], gsum) * v
    bonus_out[...] = bonus


def _rwkv_prep(proj, mu, w0, wup2, a0, aup2, g_up, k_k, k_a, r_k, gsum):
    batch, seq, _ = proj.shape
    w = RWKV_WIDTH
    tt = min(256, seq)
    halo = 16
    hb = tt // halo
    last_halo = seq // halo - 1

    def cur(width, blk):
        return pl.BlockSpec((None, tt, width), lambda b, i: (b, i, blk))

    def prev(width, blk):
        return pl.BlockSpec((None, halo, width), lambda b, i: (b, jnp.maximum(i * hb - 1, 0), blk))

    def nxt(width, blk):
        return pl.BlockSpec((None, halo, width), lambda b, i: (b, jnp.minimum((i + 1) * hb, last_halo), blk))

    def full(shape):
        return pl.BlockSpec(shape, lambda b, i: (0,) * len(shape))

    in_specs = []
    for blk in (COL_R // w, COL_R // w + 1, COL_R // w + 2):
        in_specs += [cur(w, blk), prev(w, blk), nxt(w, blk)]
    lb = COL_LORA // RWKV_LORA
    in_specs += [cur(RWKV_LORA, lb), prev(RWKV_LORA, lb), nxt(RWKV_LORA, lb)]
    in_specs += [full(mu.shape), full(w0.shape), full(wup2.shape), full(a0.shape), full(aup2.shape),
                 full(g_up.shape), full(k_k.shape), full(k_a.shape), full((1, w)), full(gsum.shape)]
    shared = pl.BlockSpec((None, tt, w), lambda b, i: (b, i, 0))
    per_dir = pl.BlockSpec((2, None, tt, w), lambda b, i: (0, b, i, 0))
    shared_shape = jax.ShapeDtypeStruct((batch, seq, w), F32)
    per_dir_shape = jax.ShapeDtypeStruct((2, batch, seq, w), F32)
    return pl.pallas_call(
        _rwkv_prep_kernel,
        name="rwkv_prep",
        grid=(batch, seq // tt),
        in_specs=in_specs,
        out_specs=[shared] * 4 + [per_dir] * 4,
        out_shape=[shared_shape] * 4 + [per_dir_shape] * 4,
        compiler_params=_params("parallel", "parallel"),
    )(*([proj] * 12), mu, w0, wup2, a0, aup2, g_up, k_k, k_a, r_k.reshape(1, w), gsum)


def _rwkv_scan_kernel(rf_ref, rb_ref, vf_ref, vb_ref, kkf_ref, kkb_ref, bbf_ref, bbb_ref, kdf_ref, kdb_ref,
                      lwf_ref, lwb_ref, yf_ref, yb_ref, state_ref):
    chunk, hd, pack = RWKV_CHUNK, HEAD_DIM, RWKV_PACK
    pw = pack * hd
    groups = RWKV_HEADS // pack

    @pl.when(pl.program_id(1) == 0)
    def _():
        state_ref[...] = jnp.zeros_like(state_ref)

    ti = lax.broadcasted_iota(jnp.int32, (chunk, pw), 0)
    tj = lax.broadcasted_iota(jnp.int32, (chunk, pw), 1) % chunk
    eye = (ti == tj).astype(F32)
    same16 = (ti // 16) == (tj // 16)
    same32 = (ti // 32) == (tj // 32)
    diag = (lax.broadcasted_iota(jnp.int32, (pw, pw), 0) // hd) == (lax.broadcasted_iota(jnp.int32, (pw, pw), 1) // hd)
    diag_bf16 = diag.astype(BF16)

    def block_diag(x):
        return jnp.concatenate([x.astype(BF16)] * pack, axis=0) * diag_bf16

    def stacked(top, bottom):
        return jnp.concatenate([top, bottom], axis=0).astype(BF16)

    chains = []
    per_direction = ((rf_ref, vf_ref, kkf_ref, bbf_ref, kdf_ref, lwf_ref), (rb_ref, vb_ref, kkb_ref, bbb_ref, kdb_ref, lwb_ref))
    for d, (r_ref, v_ref, kk_ref, bb_ref, kd_ref, lw_ref) in enumerate(per_direction):
        lag = ti - tj if d == 0 else tj - ti
        before, upto = lag > 0, lag >= 0
        lw = lw_ref[...]
        tri = upto[:, :chunk].astype(BF16)
        lw_hi = lw.astype(BF16)
        lw_rest = lw - lw_hi.astype(F32)
        lw_mid = lw_rest.astype(BF16)
        lw_lo = (lw_rest - lw_mid.astype(F32)).astype(BF16)
        cum = _dot(tri, lw_hi) + _dot(tri, lw_mid) + _dot(tri, lw_lo)
        total = jnp.sum(lw, axis=0, keepdims=True)
        p_inv = jnp.exp(-cum)
        p_rest = jnp.exp(total - cum)
        p_total = jnp.exp(total)
        kk, bb, kd = kk_ref[...], bb_ref[...], kd_ref[...]
        a_all = -kk * jnp.exp(cum - lw)
        b_all = bb * p_inv
        k_all = kd * p_inv
        r_all = r_ref[...] * jnp.exp(cum)
        b_end = bb * p_rest
        k_end = kd * p_rest
        v_all = v_ref[...]
        for g in range(groups):
            sl = slice(g * pw, (g + 1) * pw)
            chains.append(dict(
                d=d, g=g, before=before, upto=upto, ar=stacked(a_all[:, sl], r_all[:, sl]), b=b_all[:, sl],
                k=k_all[:, sl], v=v_all[:, sl], ends=stacked(b_end[:, sl], k_end[:, sl]), p_total=p_total[:, sl]))

    for c in chains:
        c["by_b"] = _dot(c["ar"], block_diag(c["b"]), _NT)
        c["by_k"] = _dot(c["ar"], block_diag(c["k"]), _NT)
    for c in chains:
        c["m_ab"] = jnp.where(c["before"], c["by_b"][:chunk], 0.0)
        c["m_rb"] = jnp.where(c["upto"], c["by_b"][chunk:], 0.0)
        c["m_kv"] = stacked(jnp.where(c["before"], c["by_k"][:chunk], 0.0), jnp.where(c["upto"], c["by_k"][chunk:], 0.0))
        c["x"] = jnp.where(same16, c["m_ab"], 0.0)
        c["inv"] = eye + c["x"]
    for c in chains:
        c["x"] = _dot(c["x"].astype(BF16), block_diag(c["x"]))
    for _ in range(2):
        for c in chains:
            both = _dot(stacked(c["x"], c["inv"]), block_diag(c["x"]))
            c["x"] = both[:chunk]
            c["inv"] = c["inv"] + both[chunk:]
    for c in chains:
        c["inv"] = c["inv"] + _dot(c["inv"].astype(BF16), block_diag(c["x"]))
    for level in range(2):
        for c in chains:
            off = jnp.where(same32 & ~same16, c["m_ab"], 0.0) if level == 0 else jnp.where(~same32, c["m_ab"], 0.0)
            c["inner"] = _dot(off.astype(BF16), block_diag(c["inv"]))
        for c in chains:
            c["inv"] = c["inv"] + _dot(c["inv"].astype(BF16), block_diag(c["inner"]))
    for c in chains:
        c["state"] = state_ref[c["d"], c["g"]]
        c["by_state"] = _dot(c["ar"], c["state"].astype(BF16), _NT)
        c["by_v"] = _dot(c["m_kv"], block_diag(c["v"]))
    for c in chains:
        c["u"] = _dot(c["inv"].astype(BF16), block_diag(c["by_state"][:chunk] + c["by_v"][:chunk]))
    for c in chains:
        c["y"] = c["by_state"][chunk:] + c["by_v"][chunk:] + _dot(c["m_rb"].astype(BF16), block_diag(c["u"]))
        update = _dot(stacked(c["u"], c["v"]), c["ends"], _TN)
        state_ref[c["d"], c["g"]] = c["state"] * c["p_total"] + jnp.where(diag, update, 0.0)
    for d, y_ref in enumerate((yf_ref, yb_ref)):
        y_ref[...] = jnp.concatenate([c["y"] for c in chains if c["d"] == d], axis=-1)


def _rwkv_scan(r, v, kk, bb, kd, lw):
    batch, seq, w = r.shape
    chunk = RWKV_CHUNK
    assert chunk == HEAD_DIM and RWKV_HEADS % RWKV_PACK == 0
    nc = seq // chunk
    pw = RWKV_PACK * HEAD_DIM
    groups = RWKV_HEADS // RWKV_PACK
    fwd = pl.BlockSpec((None, chunk, w), lambda b, c: (b, c, 0))
    bwd = pl.BlockSpec((None, chunk, w), lambda b, c: (b, nc - 1 - c, 0))
    fwd_dir = pl.BlockSpec((None, None, chunk, w), lambda b, c: (0, b, c, 0))
    bwd_dir = pl.BlockSpec((None, None, chunk, w), lambda b, c: (1, b, nc - 1 - c, 0))
    return pl.pallas_call(
        _rwkv_scan_kernel,
        name="rwkv_scan",
        grid=(batch, nc),
        in_specs=[fwd, bwd, fwd, bwd] + [fwd_dir, bwd_dir] * 4,
        out_specs=[fwd, bwd],
        out_shape=[jax.ShapeDtypeStruct((batch, seq, w), F32)] * 2,
        scratch_shapes=[pltpu.VMEM((2, groups, pw, pw), F32)],
        compiler_params=_params("parallel", "arbitrary"),
    )(r, r, v, v, kk, kk, bb, bb, kd, kd, lw, lw)


def _rwkv_post_kernel(yf_ref, yb_ref, bonus_ref, g_ref, lng_ref, lnb_ref, gsum_ref, o_ref):
    gsum = gsum_ref[...]
    y = yf_ref[...] + yb_ref[...]
    mean = _head_sum(y, gsum) * (1.0 / HEAD_DIM)
    yc = y - mean
    var = _head_sum(yc * yc, gsum) * (1.0 / HEAD_DIM)
    yn = yc * lax.rsqrt(var + RWKV_GN_EPS) * lng_ref[...] + lnb_ref[...]
    o_ref[...] = ((yn + bonus_ref[...]) * g_ref[...]).astype(o_ref.dtype)


def _rwkv_post(y_fwd, y_bwd, bonus, g, ln_g, ln_b, gsum):
    batch, seq, w = y_fwd.shape
    tt = min(512, seq)
    shared = pl.BlockSpec((None, tt, w), lambda b, i: (b, i, 0))
    row = pl.BlockSpec((1, w), lambda b, i: (0, 0))
    return pl.pallas_call(
        _rwkv_post_kernel,
        name="rwkv_post",
        grid=(batch, seq // tt),
        in_specs=[shared, shared, shared, shared, row, row,
                  pl.BlockSpec(gsum.shape, lambda b, i: (0, 0))],
        out_specs=shared,
        out_shape=jax.ShapeDtypeStruct((batch, seq, w), BF16),
        compiler_params=_params("parallel", "parallel"),
    )(y_fwd, y_bwd, bonus, g, ln_g.reshape(1, w), ln_b.reshape(1, w), gsum)


def _merge_kernel(ga_ref, gb_ref, gc_ref, oa0_ref, oa1_ref, oa2_ref, la0_ref, la1_ref, la2_ref,
                  ob_ref, oc_ref, x_ref, wa_ref, wb_ref, wc_ref, wo_ref, mod_ref, g2_ref, rw_ref, rb_ref,
                  x_out, h_out, idx_out, prob_out, *token_order):
    in_token_order = []
    for src_ref, dst_ref in zip((oa1_ref, la1_ref, oa2_ref, la2_ref), token_order):
        dil, per = src_ref.shape[0], src_ref.shape[1]
        for r in range(dil):
            for cb in range(dst_ref.shape[0]):
                dst_ref[cb, pl.ds(r, per, stride=dil), :] = src_ref[r, :, cb * LANES:(cb + 1) * LANES]
        in_token_order.append(jnp.concatenate([dst_ref[cb] for cb in range(dst_ref.shape[0])], axis=-1))
    oa1, la1, oa2, la2 = in_token_order
    lses = [la0_ref[...], la1, la2]
    m = jnp.maximum(jnp.maximum(lses[0], lses[1]), lses[2])
    es = [jnp.exp(l - m) for l in lses]
    inv = 1.0 / (es[0] + es[1] + es[2])
    o_a = (es[0] * oa0_ref[...] + es[1] * oa1 + es[2] * oa2) * inv
    merged = (_sigmoid(ga_ref[...].astype(F32)) * _dot(o_a.astype(BF16), wa_ref[...])
              + _sigmoid(gb_ref[...].astype(F32)) * _dot(ob_ref[...], wb_ref[...])
              + _sigmoid(gc_ref[...].astype(F32)) * _dot(oc_ref[...], wc_ref[...]))
    x = x_ref[...] + mod_ref[2] * _dot(merged.astype(BF16), wo_ref[...])
    x_out[...] = x
    h = _modulated_norm(x, g2_ref[...], mod_ref[4], mod_ref[3])
    h_out[...] = h.astype(h_out.dtype)

    logits = _dot(h, rw_ref[...], precision=HIGHEST) + rb_ref[...]
    lane = lax.broadcasted_iota(jnp.int32, logits.shape, 1)
    work = logits
    vals, idxs = [], []
    for _ in range(TOP_K):
        top = jnp.max(work, axis=-1, keepdims=True)
        first = jnp.min(jnp.where(work == top, lane, N_EXPERTS), axis=-1, keepdims=True)
        vals.append(top)
        idxs.append(first)
        work = jnp.where(lane == first, -jnp.inf, work)
    exps = [jnp.exp(t - vals[0]) for t in vals]
    denom = exps[0] + exps[1] + exps[2] + exps[3]
    idx_out[...] = jnp.concatenate(idxs, axis=-1)
    prob_out[...] = jnp.concatenate([e / denom for e in exps], axis=-1)


def _merge(proj, swa_outs, o_b, o_c, x, wa, wb, wc, wo, mod_l, g2, router_w, router_b):
    batch, seq, d = x.shape
    tm = min(512, seq)
    gw = SWA_GROUP_WIDTH

    def rows(width, blk=0):
        return pl.BlockSpec((None, tm, width), lambda b, i: (b, i, blk))

    def full(shape):
        return pl.BlockSpec(shape, lambda b, i: (0,) * len(shape))

    def residue_rows(dil):
        return pl.BlockSpec((None, dil, tm // dil, gw), lambda b, i: (b, 0, i, 0))

    o_list = [o for o, _ in swa_outs]
    l_list = [l for _, l in swa_outs]
    swa_specs = [rows(gw), residue_rows(SWA_PATTERNS[1][1]), residue_rows(SWA_PATTERNS[2][1])]
    return pl.pallas_call(
        _merge_kernel,
        name="merge",
        grid=(batch, seq // tm),
        in_specs=[rows(d, 0), rows(d, 1), rows(d, 2)] + swa_specs * 2
        + [rows(RWKV_WIDTH), rows(DIFF_WIDTH), rows(d), full(wa.shape), full(wb.shape), full(wc.shape),
           full(wo.shape), pl.BlockSpec((None, 6, 1, d), lambda b, i: (b, 0, 0, 0)), full((1, d)),
           full(router_w.shape), full((1, N_EXPERTS))],
        out_specs=[rows(d), rows(d), rows(TOP_K), rows(TOP_K)],
        out_shape=[jax.ShapeDtypeStruct((batch, seq, d), F32), jax.ShapeDtypeStruct((batch, seq, d), BF16),
                   jax.ShapeDtypeStruct((batch, seq, TOP_K), jnp.int32),
                   jax.ShapeDtypeStruct((batch, seq, TOP_K), F32)],
        scratch_shapes=[pltpu.VMEM((gw // LANES, tm, LANES), F32)] * 4,
        compiler_params=_params("parallel", "parallel"),
    )(proj, proj, proj, *o_list, *l_list, o_b, o_c, x, wa, wb, wc, wo, mod_l, g2.reshape(1, d),
      router_w, router_b.reshape(1, N_EXPERTS))


def _expert_kernel(tile_expert_ref, tile_valid_ref, x_ref, w1_ref, b1_ref, w2_ref, b2_ref, o_ref):
    t = pl.program_id(0)

    @pl.when(tile_valid_ref[t] != 0)
    def _():
        hh = _dot(x_ref[...], w1_ref[...]) + b1_ref[...]
        gated = (lax.broadcasted_iota(jnp.int32, hh.shape, 1) % 2) == 0
        glu = jnp.minimum(hh, SWIGLU_LIMIT)
        part = jnp.where(gated, glu * _sigmoid(SWIGLU_ALPHA * glu), jnp.clip(hh, -SWIGLU_LIMIT, SWIGLU_LIMIT) + 1.0)
        act = part * pltpu.roll(part, hh.shape[1] - 1, axis=1)
        o_ref[...] = (_dot(act.astype(BF16), w2_ref[...]) + b2_ref[...]).astype(o_ref.dtype)

    @pl.when(tile_valid_ref[t] == 0)
    def _():
        o_ref[...] = jnp.zeros_like(o_ref)


def _experts(xg, tile_expert, tile_valid, w1, b1, w2, b2, layer):
    rows, d = xg.shape
    f2 = w1.shape[3]
    tm = MOE_ROW_TILE
    grid_spec = pltpu.PrefetchScalarGridSpec(
        num_scalar_prefetch=2,
        grid=(rows // tm,),
        in_specs=[
            pl.BlockSpec((tm, d), lambda t, te, tv: (t, 0)),
            pl.BlockSpec((None, None, d, f2), lambda t, te, tv: (layer, te[t], 0, 0)),
            pl.BlockSpec((None, None, 1, f2), lambda t, te, tv: (layer, te[t], 0, 0)),
            pl.BlockSpec((None, None, f2, d), lambda t, te, tv: (layer, te[t], 0, 0)),
            pl.BlockSpec((None, None, 1, d), lambda t, te, tv: (layer, te[t], 0, 0)),
        ],
        out_specs=pl.BlockSpec((tm, d), lambda t, te, tv: (t, 0)),
    )
    return pl.pallas_call(
        _expert_kernel,
        name="moe_experts",
        grid_spec=grid_spec,
        out_shape=jax.ShapeDtypeStruct((rows, d), BF16),
        compiler_params=_params("arbitrary"),
    )(tile_expert, tile_valid, xg, w1, b1, w2, b2)


def _combine_kernel(y_ref, p_ref, x_ref, mod_ref, g_ref, o_ref, *, final):
    p = p_ref[...]
    acc = p[:, 0:1] * y_ref[0].astype(F32)
    for j in range(1, TOP_K):
        acc = acc + p[:, j:j + 1] * y_ref[j].astype(F32)
    x = x_ref[...] + mod_ref[5] * acc
    if final:
        x = x * lax.rsqrt(jnp.mean(x * x, axis=-1, keepdims=True) + NORM_EPS) * g_ref[...]
    o_ref[...] = x


def _combine(y4, probs, x, mod_l, final_g, final):
    batch, seq, d = x.shape
    tm = min(512, seq)
    return pl.pallas_call(
        functools.partial(_combine_kernel, final=final),
        name="moe_combine",
        grid=(batch, seq // tm),
        in_specs=[
            pl.BlockSpec((TOP_K, None, tm, d), lambda b, i: (0, b, i, 0)),
            pl.BlockSpec((None, tm, TOP_K), lambda b, i: (b, i, 0)),
            pl.BlockSpec((None, tm, d), lambda b, i: (b, i, 0)),
            pl.BlockSpec((None, 6, 1, d), lambda b, i: (b, 0, 0, 0)),
            pl.BlockSpec((1, d), lambda b, i: (0, 0)),
        ],
        out_specs=pl.BlockSpec((None, tm, d), lambda b, i: (b, i, 0)),
        out_shape=jax.ShapeDtypeStruct((batch, seq, d), F32),
        compiler_params=_params("parallel", "parallel"),
    )(y4, probs, x, mod_l, final_g.reshape(1, d))


def _dispatch_plan(idx, n_rows_padded):
    tm = MOE_ROW_TILE
    flat = idx.reshape(-1)
    experts = jnp.arange(N_EXPERTS, dtype=jnp.int32)
    order = jnp.argsort(flat, stable=True).astype(jnp.int32)
    position = jnp.argsort(order).astype(jnp.int32)
    counts = jnp.sum((flat[:, None] == experts[None, :]).astype(jnp.int32), axis=0)
    padded = ((counts + tm - 1) // tm) * tm
    padded_end = jnp.cumsum(padded)
    padded_start = padded_end - padded
    start = jnp.cumsum(counts) - counts
    slot_row = padded_start[flat] + position - start[flat]
    tile_start = jnp.arange(n_rows_padded // tm, dtype=jnp.int32) * tm
    tile_expert = jnp.minimum(jnp.sum((padded_end[None, :] <= tile_start[:, None]).astype(jnp.int32), axis=1),
                              N_EXPERTS - 1)
    tile_valid = (tile_start < padded_end[-1]).astype(jnp.int32)
    row = jnp.arange(n_rows_padded, dtype=jnp.int32)
    row_expert = jnp.repeat(tile_expert, tm)
    rank = row - padded_start[row_expert]
    routed = rank < counts[row_expert]
    src_slot = order[jnp.clip(start[row_expert] + rank, 0, flat.shape[0] - 1)]
    src_token = jnp.where(routed, src_slot // TOP_K, 0)
    return src_token, slot_row, tile_expert, tile_valid


def _moe(h2, idx, probs, x, mod_l, w1, b1, w2, b2, layer, final_g, final):
    batch, seq, d = x.shape
    n_tok = batch * seq
    n_rows_padded = n_tok * TOP_K + N_EXPERTS * MOE_ROW_TILE
    src_token, slot_row, tile_expert, tile_valid = _dispatch_plan(idx, n_rows_padded)
    xg = jnp.take(h2.reshape(n_tok, d), src_token, axis=0)
    y = _experts(xg, tile_expert, tile_valid, w1, b1, w2, b2, layer)
    rows4 = slot_row.reshape(n_tok, TOP_K).T
    y4 = jnp.take(y, rows4, axis=0).reshape(TOP_K, batch, seq, d)
    return _combine(y4, probs, x, mod_l, final_g, final)


def _t5_bucket(rel):
    nb = REL_BUCKETS // 2
    max_exact = nb // 2
    ret = jnp.where(rel > 0, nb, 0)
    n = jnp.abs(rel)
    nf = jnp.maximum(n, 1).astype(F32)
    large = max_exact + (jnp.log(nf / max_exact) / math.log(REL_MAX_DIST / max_exact)
                         * (nb - max_exact)).astype(jnp.int32)
    large = jnp.minimum(large, nb - 1)
    return ret + jnp.where(n < max_exact, n, large)


def _swa_bias_tiles(rel_bias):
    half, qb = SWA_HALF, SWA_QUERY_BLOCK
    a = jnp.arange(qb)[:, None]
    c = jnp.arange(qb + 2 * half)[None, :]
    j = c - half - a
    tiles = []
    for g, (_, dil) in enumerate(SWA_PATTERNS):
        offs = jnp.arange(-half, half + 1) * dil
        table = rel_bias[_t5_bucket(offs)][:, g * SWA_GROUP_HEADS:(g + 1) * SWA_GROUP_HEADS].T
        tile = table[:, jnp.clip(j + half, 0, 2 * half)]
        tiles.append(jnp.where((jnp.abs(j) <= half)[None], tile, NEG_INF).astype(F32))
    return tiles


def _diff_bias_segments(rel_bias, seq):
    tq = min(DIFF_Q_TILE, seq)
    table = rel_bias[:, len(SWA_PATTERNS) * SWA_GROUP_HEADS:].T
    by_rel = table[:, _t5_bucket(jnp.arange(2 * seq) - seq)]
    segs = [by_rel[:, seq - (i + 1) * tq:2 * seq - i * tq] for i in range(seq // tq)]
    return jnp.stack(segs, axis=1)[:, :, None, :].astype(F32)


def _pack_w_in(w_in_l):
    a_w = 3 * SWA_WIDTH
    b_w = 3 * RWKV_WIDTH + RWKV_LORA
    c_w = 3 * DIFF_WIDTH
    a, b, c, gates = (w_in_l[:, :a_w], w_in_l[:, a_w:a_w + b_w], w_in_l[:, a_w + b_w:a_w + b_w + c_w],
                      w_in_l[:, a_w + b_w + c_w:])
    gw = SWA_GROUP_WIDTH

    def group(g):
        return [a[:, t * SWA_WIDTH + g * gw:t * SWA_WIDTH + (g + 1) * gw] for t in range(3)]

    pad = jnp.zeros((w_in_l.shape[0], PROJ_COLS_PADDED - PROJ_COLS), w_in_l.dtype)
    return jnp.concatenate([gates] + group(1) + group(2) + group(0) + [c, b, pad], axis=1).astype(BF16)


def _direction_padded(w_up):
    z = jnp.zeros_like(w_up[0])
    return jnp.stack([jnp.concatenate([w_up[0], z], axis=0), jnp.concatenate([z, w_up[1]], axis=0)])


def kernel(x, c, w_mod, b_mod, norm1_g, norm2_g, w_in, rwkv_mu, rwkv_w0, rwkv_w_up, rwkv_a0, rwkv_a_up, rwkv_g_up, rwkv_k_k, rwkv_k_a, rwkv_r_k, rwkv_ln_g, rwkv_ln_b, diff_lambda, diff_subln_g, rel_bias, w_branch_a, w_branch_b, w_branch_c, w_out, router_w, router_b, moe_w1, moe_b1, moe_w2, moe_b2, final_norm_g):
    batch, seq, d = x.shape
    depth = w_mod.shape[0]
    mod = _modulation(c, w_mod, b_mod)
    swa_tiles = _swa_bias_tiles(rel_bias)
    bias_segs = _diff_bias_segments(rel_bias, seq)
    head_of = jnp.arange(RWKV_WIDTH) // HEAD_DIM
    gsum = (head_of[:, None] == head_of[None, :]).astype(BF16)
    w1 = moe_w1.astype(BF16)
    b1 = moe_b1[:, :, None, :]
    w2 = jnp.stack([moe_w2, jnp.zeros_like(moe_w2)], axis=3).reshape(
        moe_w2.shape[0], N_EXPERTS, 2 * D_EXPERT, d).astype(BF16)
    b2 = moe_b2[:, :, None, :]

    for l in range(depth):
        mod_l = mod[l]
        proj, res1, res2 = _norm_proj(x, norm1_g[l], mod_l, _pack_w_in(w_in[l]))
        swa_outs = [_swa_group(proj, swa_tiles[0], 1, COL_A0 // SWA_GROUP_WIDTH),
                    _swa_group(res1, swa_tiles[1], SWA_PATTERNS[1][1], 0),
                    _swa_group(res2, swa_tiles[2], SWA_PATTERNS[2][1], 0)]
        r, v, g, bonus, kk, bb, kd, lw = _rwkv_prep(
            proj, rwkv_mu[l], rwkv_w0[l], _direction_padded(rwkv_w_up[l]), rwkv_a0[l],
            _direction_padded(rwkv_a_up[l]), rwkv_g_up[l], rwkv_k_k[l], rwkv_k_a[l], rwkv_r_k[l], gsum)
        y_fwd, y_bwd = _rwkv_scan(r, v, kk, bb, kd, lw)
        o_b = _rwkv_post(y_fwd, y_bwd, bonus, g, rwkv_ln_g[l], rwkv_ln_b[l], gsum)
        lambda_init = 0.8 - 0.6 * math.exp(-0.3 * l)
        o_c = _diff_attention(proj, bias_segs, diff_lambda[l], diff_subln_g[l], lambda_init)
        x, h2, idx, probs = _merge(
            proj, swa_outs, o_b, o_c, x, w_branch_a[l].astype(BF16), w_branch_b[l].astype(BF16),
            w_branch_c[l].astype(BF16), w_out[l].astype(BF16), mod_l, norm2_g[l], router_w[l], router_b[l])
        x = _moe(h2, idx, probs, x, mod_l, w1, b1, w2, b2, l, final_norm_g, l == depth - 1)
    return x
```

```python
import functools
import math

import jax
import jax.numpy as jnp
from jax import lax
from jax.experimental import pallas as pl
from jax.experimental.pallas import tpu as pltpu

F32 = jnp.float32
BF16 = jnp.bfloat16
HIGHEST = lax.Precision.HIGHEST

D_MODEL = 1024
HEAD_DIM = 64
LANES = 128
NORM_EPS = 1e-6
NEG_INF = -1e30
LOG2_E = 1.4426950408889634

SWA_PATTERNS = ((128, 1), (512, 4), (2048, 16))
SWA_GROUP_HEADS = 4
SWA_GROUP_WIDTH = SWA_GROUP_HEADS * HEAD_DIM
SWA_WIDTH = len(SWA_PATTERNS) * SWA_GROUP_WIDTH
SWA_HALF = 64
SWA_QUERY_BLOCK = 128

RWKV_HEADS = 12
RWKV_WIDTH = RWKV_HEADS * HEAD_DIM
RWKV_LORA = 384
RWKV_GN_EPS = 64e-5
RWKV_CHUNK = 64
RWKV_PACK = 4

DIFF_HEADS = 6
DIFF_WIDTH = DIFF_HEADS * 2 * HEAD_DIM
DIFF_SUBLN_EPS = 1e-5
DIFF_Q_TILE = 512

REL_BUCKETS = 32
REL_MAX_DIST = 128

N_EXPERTS = 32
TOP_K = 4
D_EXPERT = 1024
SWIGLU_LIMIT = 7.0
SWIGLU_ALPHA = 1.702
MOE_ROW_TILE = 256

COL_GATES = 0
COL_A_DILATED = 3 * D_MODEL
COL_A0 = COL_A_DILATED + 2 * SWA_WIDTH
COL_C = COL_A0 + SWA_WIDTH
COL_R = COL_C + 3 * DIFF_WIDTH
COL_LORA = COL_R + 3 * RWKV_WIDTH
PROJ_COLS = COL_LORA + RWKV_LORA
PROJ_COLS_PADDED = 10752
PROJ_COL_TILE = 1536
STRIDED_COL_TILE = COL_A_DILATED // PROJ_COL_TILE
PROJ_ROW_TILE = 1024

VMEM_LIMIT = 48 * 1024 * 1024
EXPERT_VMEM_LIMIT = 56 * 1024 * 1024

_NT = ((1,), (1,))
_TN = ((0,), (0,))


def _params(*sem):
    return pltpu.CompilerParams(dimension_semantics=sem, vmem_limit_bytes=VMEM_LIMIT)


def _sigmoid(x):
    return 1.0 / (1.0 + jnp.exp(-x))


def _dot(a, b, dims=((1,), (0,)), precision=None):
    return lax.dot_general(a, b, (dims, ((), ())), precision=precision, preferred_element_type=F32)


def _hi_lo(x):
    hi = x.astype(BF16)
    return hi, (x - hi.astype(F32)).astype(BF16)


def _dot3(a, b):
    a_hi, a_lo = _hi_lo(a)
    b_hi, b_lo = _hi_lo(b)
    return _dot(a_hi, b_hi) + _dot(a_hi, b_lo) + _dot(a_lo, b_hi)


def _head_sum(x, ones_bf16):
    hi, lo = _hi_lo(x)
    return _dot(hi, ones_bf16) + _dot(lo, ones_bf16)


def _mod_kernel(c_ref, w_ref, b_ref, o_ref):
    c = c_ref[...]
    cond = c * _sigmoid(c)
    o_ref[...] = _dot(cond, w_ref[...], precision=HIGHEST) + b_ref[...]


def _modulation(c, w_mod, b_mod):
    n_layers, d, n = w_mod.shape
    batch = c.shape[0]
    tn = 1536
    out = pl.pallas_call(
        _mod_kernel,
        name="modulation",
        grid=(n_layers, n // tn),
        in_specs=[
            pl.BlockSpec((batch, d), lambda l, j: (0, 0)),
            pl.BlockSpec((None, d, tn), lambda l, j: (l, 0, j)),
            pl.BlockSpec((None, 1, tn), lambda l, j: (l, 0, j)),
        ],
        out_specs=pl.BlockSpec((None, batch, tn), lambda l, j: (l, 0, j)),
        out_shape=jax.ShapeDtypeStruct((n_layers, batch, n), F32),
        compiler_params=_params("parallel", "parallel"),
    )(c, w_mod, b_mod.reshape(n_layers, 1, n))
    return out.reshape(n_layers, batch, 6, 1, d)


def _modulated_norm(x, g, scale, shift):
    y = x * lax.rsqrt(jnp.mean(x * x, axis=-1, keepdims=True) + NORM_EPS) * g
    return y * (1.0 + scale) + shift


def _norm_proj_kernel(x_ref, g_ref, mod_ref, w_ref, o_ref, res4_ref, res16_ref, h_ref, acc_ref):
    j = pl.program_id(2)

    @pl.when(j == 0)
    def _():
        h_ref[...] = _modulated_norm(x_ref[...], g_ref[...], mod_ref[1], mod_ref[0]).astype(BF16)

    @pl.when(j != STRIDED_COL_TILE)
    def _():
        o_ref[...] = _dot(h_ref[...], w_ref[...]).astype(o_ref.dtype)

    @pl.when(j == STRIDED_COL_TILE)
    def _():
        acc = _dot(h_ref[...], w_ref[...])
        o_ref[...] = acc.astype(o_ref.dtype)
        lanes = acc_ref.shape[2]
        tm = acc_ref.shape[1]
        for cb in range(acc_ref.shape[0]):
            acc_ref[cb] = acc[:, cb * lanes:(cb + 1) * lanes]
        per_group = SWA_WIDTH // lanes
        for res_ref, dil, first in ((res4_ref, SWA_PATTERNS[1][1], 0), (res16_ref, SWA_PATTERNS[2][1], per_group)):
            for r in range(dil):
                for cb in range(per_group):
                    res_ref[r, :, cb * lanes:(cb + 1) * lanes] = (
                        acc_ref[first + cb, pl.ds(r, tm // dil, stride=dil), :].astype(res_ref.dtype))


def _norm_proj(x, g, mod_l, w):
    batch, seq, d = x.shape
    n = w.shape[1]
    tm, tn = min(PROJ_ROW_TILE, seq), PROJ_COL_TILE
    d1, d2 = SWA_PATTERNS[1][1], SWA_PATTERNS[2][1]

    def res_spec(dil):
        return pl.BlockSpec((None, dil, tm // dil, SWA_WIDTH), lambda b, i, j: (b, 0, i, 0))

    return pl.pallas_call(
        _norm_proj_kernel,
        name="norm_proj",
        grid=(batch, seq // tm, n // tn),
        in_specs=[
            pl.BlockSpec((None, tm, d), lambda b, i, j: (b, i, 0)),
            pl.BlockSpec((1, d), lambda b, i, j: (0, 0)),
            pl.BlockSpec((None, 6, 1, d), lambda b, i, j: (b, 0, 0, 0)),
            pl.BlockSpec((d, tn), lambda b, i, j: (0, j)),
        ],
        out_specs=[pl.BlockSpec((None, tm, tn), lambda b, i, j: (b, i, j)), res_spec(d1), res_spec(d2)],
        out_shape=[jax.ShapeDtypeStruct((batch, seq, n), BF16),
                   jax.ShapeDtypeStruct((batch, d1, seq // d1, SWA_WIDTH), BF16),
                   jax.ShapeDtypeStruct((batch, d2, seq // d2, SWA_WIDTH), BF16)],
        scratch_shapes=[pltpu.VMEM((tm, d), BF16), pltpu.VMEM((tn // LANES, tm, LANES), F32)],
        compiler_params=_params("parallel", "parallel", "arbitrary"),
    )(x, g.reshape(1, d), mod_l, w)


def _swa_kernel(q_ref, k_ref, v_ref, bias_ref, o_ref, lse_ref, kpad_ref, vpad_ref, *, length):
    half, qb, hd = SWA_HALF, SWA_QUERY_BLOCK, HEAD_DIM
    win = qb + 2 * half
    zeros = jnp.zeros((half, SWA_GROUP_WIDTH), BF16)
    for pad_ref, src_ref in ((kpad_ref, k_ref), (vpad_ref, v_ref)):
        pad_ref[0:half, :] = zeros
        pad_ref[half + length:half + length + half, :] = zeros
        pad_ref[half:half + length, :] = src_ref[...]

    def block(n, carry):
        q0 = pl.multiple_of(n * qb, qb)
        q = q_ref[pl.ds(q0, qb), :]
        kw = kpad_ref[pl.ds(q0, win), :]
        vw = vpad_ref[pl.ds(q0, win), :]
        key_pos = q0 - half + lax.broadcasted_iota(jnp.int32, (1, win), 1)
        in_range = (key_pos >= 0) & (key_pos < length)
        outs, lses = [], []
        for h in range(SWA_GROUP_HEADS):
            sl = slice(h * hd, (h + 1) * hd)
            s = _dot(q[:, sl], kw[:, sl], _NT) * (hd ** -0.5) + bias_ref[h]
            s = jnp.where(in_range, s, NEG_INF)
            m = jnp.max(s, axis=-1, keepdims=True)
            p = jnp.exp(s - m)
            l = jnp.sum(p, axis=-1, keepdims=True)
            outs.append(_dot(p.astype(BF16), vw[:, sl]) / l)
            lses.append(jnp.broadcast_to(m + jnp.log(l), (qb, hd)))
        o_ref[pl.ds(q0, qb), :] = jnp.concatenate(outs, axis=-1)
        lse_ref[pl.ds(q0, qb), :] = jnp.concatenate(lses, axis=-1)
        return carry

    lax.fori_loop(0, length // qb, block, 0)


def _swa_group(qkv, bias_tile, dilation, col_block):
    gw = SWA_GROUP_WIDTH
    if dilation == 1:
        batch, length, _ = qkv.shape
        grid = (batch, 1)
        block = (None, length, gw)
        out_dims = (batch, length, gw)

        def at(col):
            return lambda b, r: (b, 0, col)
    else:
        batch, _, length, _ = qkv.shape
        grid = (batch, dilation)
        block = (None, None, length, gw)
        out_dims = (batch, dilation, length, gw)

        def at(col):
            return lambda b, r: (b, r, 0, col)

    out_spec = pl.BlockSpec(block, at(0))
    out_shape = jax.ShapeDtypeStruct(out_dims, F32)
    return pl.pallas_call(
        functools.partial(_swa_kernel, length=length),
        name="swa",
        grid=grid,
        in_specs=[pl.BlockSpec(block, at(col_block + which)) for which in range(3)]
        + [pl.BlockSpec(bias_tile.shape, lambda b, r: (0, 0, 0))],
        out_specs=[out_spec, out_spec],
        out_shape=[out_shape, out_shape],
        scratch_shapes=[pltpu.VMEM((length + 2 * SWA_HALF, gw), BF16)] * 2,
        compiler_params=_params("parallel", "parallel"),
    )(qkv, qkv, qkv, bias_tile)


def _diff_kernel(q_ref, k_ref, v_ref, seg_ref, dl_ref, g_ref, o_ref, bias_ref, *, lambda_init):
    hd = HEAD_DIM
    tq, seq = bias_ref.shape

    @pl.when(pl.program_id(2) == 0)
    def _():
        seg = jnp.broadcast_to(seg_ref[...] * LOG2_E, (tq, seq + tq))
        bias_ref[...] = pltpu.roll(seg, seq, 1, stride=1, stride_axis=0)[:, :seq]

    dl = dl_ref[...]
    lam = (jnp.exp(jnp.sum(dl[0:1] * dl[1:2], axis=-1, keepdims=True))
           - jnp.exp(jnp.sum(dl[2:3] * dl[3:4], axis=-1, keepdims=True)) + lambda_init)
    k = k_ref[...]
    v = v_ref[...]
    q = (q_ref[...].astype(F32) * (hd ** -0.5 * LOG2_E)).astype(BF16)
    chains = [(slice(r0, r0 + 128), slice(comp * hd, (comp + 1) * hd)) for r0 in range(0, tq, 128) for comp in range(2)]
    scores = [_dot(q[rows, sl], k[:, sl], _NT) + bias_ref[rows, :] for rows, sl in chains]
    probs = [jnp.exp2(s - jnp.max(s, axis=-1, keepdims=True)) for s in scores]
    outs = [_dot(p.astype(BF16), v) * (1.0 / jnp.sum(p, axis=-1, keepdims=True)) for p in probs]
    for i in range(0, len(chains), 2):
        o = outs[i] - lam * outs[i + 1]
        o = o * lax.rsqrt(jnp.mean(o * o, axis=-1, keepdims=True) + DIFF_SUBLN_EPS) * g_ref[...]
        o_ref[chains[i][0], :] = (o * (1.0 - lambda_init)).astype(o_ref.dtype)


def _diff_attention(proj, bias_segs, diff_lambda_l, subln_g, lambda_init):
    batch, seq, _ = proj.shape
    hw = 2 * HEAD_DIM
    tq = min(DIFF_Q_TILE, seq)
    base = COL_C // hw
    return pl.pallas_call(
        functools.partial(_diff_kernel, lambda_init=lambda_init),
        name="diff_attn",
        grid=(DIFF_HEADS, seq // tq, batch),
        in_specs=[
            pl.BlockSpec((None, tq, hw), lambda h, i, b: (b, i, base + h)),
            pl.BlockSpec((None, seq, hw), lambda h, i, b: (b, 0, base + DIFF_HEADS + h)),
            pl.BlockSpec((None, seq, hw), lambda h, i, b: (b, 0, base + 2 * DIFF_HEADS + h)),
            pl.BlockSpec((None, None, 1, seq + tq), lambda h, i, b: (h, i, 0, 0)),
            pl.BlockSpec((4, HEAD_DIM), lambda h, i, b: (0, 0)),
            pl.BlockSpec((1, hw), lambda h, i, b: (0, 0)),
        ],
        out_specs=pl.BlockSpec((None, tq, hw), lambda h, i, b: (b, i, h)),
        out_shape=jax.ShapeDtypeStruct((batch, seq, DIFF_WIDTH), BF16),
        scratch_shapes=[pltpu.VMEM((tq, seq), F32)],
        compiler_params=_params("parallel", "parallel", "arbitrary"),
    )(proj, proj, proj, bias_segs, diff_lambda_l, subln_g.reshape(1, hw))


def _shifted(cur_ref, prev_ref, next_ref, mu, first, last):
    x = cur_ref[...].astype(F32)
    rows = x.shape[0]
    halo = prev_ref.shape[0]
    before = jnp.where(first, 0.0, prev_ref[halo - 1:halo, :].astype(F32))
    after = jnp.where(last, 0.0, next_ref[0:1, :].astype(F32))
    row = lax.broadcasted_iota(jnp.int32, (rows, 1), 0)
    prev = jnp.where(row == 0, before, pltpu.roll(x, 1, axis=0))
    nxt = jnp.where(row == rows - 1, after, pltpu.roll(x, rows - 1, axis=0))
    return x + mu[0:1] * (prev - x) + mu[1:2] * (nxt - x)


def _rwkv_prep_kernel(r_ref, rp_ref, rn_ref, k_ref, kp_ref, kn_ref, v_ref, vp_ref, vn_ref,
                      lo_ref, lop_ref, lon_ref, mu_ref, w0_ref, wup_ref, a0_ref, aup_ref, gup_ref,
                      kk_scale_ref, ka_ref, rk_ref, gsum_ref,
                      r_out, v_out, g_out, bonus_out, kk_out, bb_out, kd_out, lw_out):
    w = RWKV_WIDTH
    first = pl.program_id(1) == 0
    last = pl.program_id(1) == pl.num_programs(1) - 1
    mu = mu_ref[...]
    r = _shifted(r_ref, rp_ref, rn_ref, mu[:, 0:w], first, last)
    k = _shifted(k_ref, kp_ref, kn_ref, mu[:, w:2 * w], first, last)
    v = _shifted(v_ref, vp_ref, vn_ref, mu[:, 2 * w:3 * w], first, last)
    lora = _shifted(lo_ref, lop_ref, lon_ref, mu[:, 3 * w:3 * w + RWKV_LORA], first, last)
    decay_in = jnp.tanh(lora[:, 0:128])
    iclr_in = lora[:, 128:256]
    gsum = gsum_ref[...]
    r_out[...] = r
    v_out[...] = v
    g_out[...] = _dot3(_sigmoid(lora[:, 256:384]), gup_ref[...])
    bonus = jnp.zeros_like(r)
    for di in range(2):
        z = w0_ref[di:di + 1, :] + _dot3(decay_in, wup_ref[di])
        u = -z
        softplus = jnp.maximum(u, 0.0) + jnp.log(1.0 + jnp.exp(-jnp.abs(u)))
        lw_out[di] = -jnp.exp(-softplus - 0.5)
        a = _sigmoid(a0_ref[di:di + 1, :] + _dot3(iclr_in, aup_ref[di]))
        kk = k * kk_scale_ref[di:di + 1, :]
        kk = kk * lax.rsqrt(jnp.maximum(_head_sum(kk * kk, gsum), 1e-24))
        kd = k * (1.0 + (a - 1.0) * ka_ref[di:di + 1, :])
        kk_out[di] = kk
        bb_out[di] = kk * a
        kd_out[di] = kd
        bonus = bonus + _head_sum(r * kd * rk_ref[...], gsum) * v
    bonus_out[...] = bonus


def _rwkv_prep(proj, mu, w0, wup2, a0, aup2, g_up, k_k, k_a, r_k, gsum):
    batch, seq, _ = proj.shape
    w = RWKV_WIDTH
    tt = min(256, seq)
    halo = 16
    hb = tt // halo
    last_halo = seq // halo - 1

    def cur(width, blk):
        return pl.BlockSpec((None, tt, width), lambda b, i: (b, i, blk))

    def prev(width, blk):
        return pl.BlockSpec((None, halo, width), lambda b, i: (b, jnp.maximum(i * hb - 1, 0), blk))

    def nxt(width, blk):
        return pl.BlockSpec((None, halo, width), lambda b, i: (b, jnp.minimum((i + 1) * hb, last_halo), blk))

    def full(shape):
        return pl.BlockSpec(shape, lambda b, i: (0,) * len(shape))

    in_specs = []
    for blk in (COL_R // w, COL_R // w + 1, COL_R // w + 2):
        in_specs += [cur(w, blk), prev(w, blk), nxt(w, blk)]
    lb = COL_LORA // RWKV_LORA
    in_specs += [cur(RWKV_LORA, lb), prev(RWKV_LORA, lb), nxt(RWKV_LORA, lb)]
    in_specs += [full(mu.shape), full(w0.shape), full(wup2.shape), full(a0.shape), full(aup2.shape),
                 full(g_up.shape), full(k_k.shape), full(k_a.shape), full((1, w)), full(gsum.shape)]
    shared = pl.BlockSpec((None, tt, w), lambda b, i: (b, i, 0))
    per_dir = pl.BlockSpec((2, None, tt, w), lambda b, i: (0, b, i, 0))
    shared_shape = jax.ShapeDtypeStruct((batch, seq, w), F32)
    per_dir_shape = jax.ShapeDtypeStruct((2, batch, seq, w), F32)
    return pl.pallas_call(
        _rwkv_prep_kernel,
        name="rwkv_prep",
        grid=(batch, seq // tt),
        in_specs=in_specs,
        out_specs=[shared] * 4 + [per_dir] * 4,
        out_shape=[shared_shape] * 4 + [per_dir_shape] * 4,
        compiler_params=_params("parallel", "parallel"),
    )(*([proj] * 12), mu, w0, wup2, a0, aup2, g_up, k_k, k_a, r_k.reshape(1, w), gsum)


def _rwkv_scan_kernel(rf_ref, rb_ref, vf_ref, vb_ref, kkf_ref, kkb_ref, bbf_ref, bbb_ref, kdf_ref, kdb_ref,
                      lwf_ref, lwb_ref, yf_ref, yb_ref, state_ref):
    chunk, hd, pack = RWKV_CHUNK, HEAD_DIM, RWKV_PACK
    pw = pack * hd
    groups = RWKV_HEADS // pack

    @pl.when(pl.program_id(1) == 0)
    def _():
        state_ref[...] = jnp.zeros_like(state_ref)

    ti = lax.broadcasted_iota(jnp.int32, (chunk, pw), 0)
    tj = lax.broadcasted_iota(jnp.int32, (chunk, pw), 1) % chunk
    eye = (ti == tj).astype(F32)
    same16 = (ti // 16) == (tj // 16)
    same32 = (ti // 32) == (tj // 32)
    diag = (lax.broadcasted_iota(jnp.int32, (pw, pw), 0) // hd) == (lax.broadcasted_iota(jnp.int32, (pw, pw), 1) // hd)
    diag_bf16 = diag.astype(BF16)

    def block_diag(x):
        return jnp.concatenate([x.astype(BF16)] * pack, axis=0) * diag_bf16

    def stacked(top, bottom):
        return jnp.concatenate([top, bottom], axis=0).astype(BF16)

    chains = []
    per_direction = ((rf_ref, vf_ref, kkf_ref, bbf_ref, kdf_ref, lwf_ref), (rb_ref, vb_ref, kkb_ref, bbb_ref, kdb_ref, lwb_ref))
    for d, (r_ref, v_ref, kk_ref, bb_ref, kd_ref, lw_ref) in enumerate(per_direction):
        lag = ti - tj if d == 0 else tj - ti
        before, upto = lag > 0, lag >= 0
        lw = lw_ref[...]
        tri = upto[:, :chunk].astype(BF16)
        lw_hi = lw.astype(BF16)
        lw_rest = lw - lw_hi.astype(F32)
        lw_mid = lw_rest.astype(BF16)
        lw_lo = (lw_rest - lw_mid.astype(F32)).astype(BF16)
        cum = _dot(tri, lw_hi) + _dot(tri, lw_mid) + _dot(tri, lw_lo)
        total = jnp.sum(lw, axis=0, keepdims=True)
        p_inv = jnp.exp(-cum)
        p_rest = jnp.exp(total - cum)
        p_total = jnp.exp(total)
        kk, bb, kd = kk_ref[...], bb_ref[...], kd_ref[...]
        a_all = -kk * jnp.exp(cum - lw)
        b_all = bb * p_inv
        k_all = kd * p_inv
        r_all = r_ref[...] * jnp.exp(cum)
        b_end = bb * p_rest
        k_end = kd * p_rest
        v_all = v_ref[...]
        for g in range(groups):
            sl = slice(g * pw, (g + 1) * pw)
            chains.append(dict(
                d=d, g=g, before=before, upto=upto, ar=stacked(a_all[:, sl], r_all[:, sl]), b=b_all[:, sl],
                k=k_all[:, sl], v=v_all[:, sl], ends=stacked(b_end[:, sl], k_end[:, sl]), p_total=p_total[:, sl]))

    for c in chains:
        c["by_b"] = _dot(c["ar"], block_diag(c["b"]), _NT)
        c["by_k"] = _dot(c["ar"], block_diag(c["k"]), _NT)
    for c in chains:
        c["m_ab"] = jnp.where(c["before"], c["by_b"][:chunk], 0.0)
        c["m_rb"] = jnp.where(c["upto"], c["by_b"][chunk:], 0.0)
        c["m_kv"] = stacked(jnp.where(c["before"], c["by_k"][:chunk], 0.0), jnp.where(c["upto"], c["by_k"][chunk:], 0.0))
        c["x"] = jnp.where(same16, c["m_ab"], 0.0)
        c["inv"] = eye + c["x"]
    for c in chains:
        c["x"] = _dot(c["x"].astype(BF16), block_diag(c["x"]))
    for _ in range(2):
        for c in chains:
            both = _dot(stacked(c["x"], c["inv"]), block_diag(c["x"]))
            c["x"] = both[:chunk]
            c["inv"] = c["inv"] + both[chunk:]
    for c in chains:
        c["inv"] = c["inv"] + _dot(c["inv"].astype(BF16), block_diag(c["x"]))
    for level in range(2):
        for c in chains:
            off = jnp.where(same32 & ~same16, c["m_ab"], 0.0) if level == 0 else jnp.where(~same32, c["m_ab"], 0.0)
            c["inner"] = _dot(off.astype(BF16), block_diag(c["inv"]))
        for c in chains:
            c["inv"] = c["inv"] + _dot(c["inv"].astype(BF16), block_diag(c["inner"]))
    for c in chains:
        c["state"] = state_ref[c["d"], c["g"]]
        c["by_state"] = _dot(c["ar"], c["state"].astype(BF16), _NT)
        c["by_v"] = _dot(c["m_kv"], block_diag(c["v"]))
    for c in chains:
        c["u"] = _dot(c["inv"].astype(BF16), block_diag(c["by_state"][:chunk] + c["by_v"][:chunk]))
    for c in chains:
        c["y"] = c["by_state"][chunk:] + c["by_v"][chunk:] + _dot(c["m_rb"].astype(BF16), block_diag(c["u"]))
        update = _dot(stacked(c["u"], c["v"]), c["ends"], _TN)
        state_ref[c["d"], c["g"]] = c["state"] * c["p_total"] + jnp.where(diag, update, 0.0)
    for d, y_ref in enumerate((yf_ref, yb_ref)):
        y_ref[...] = jnp.concatenate([c["y"] for c in chains if c["d"] == d], axis=-1)


def _rwkv_scan(r, v, kk, bb, kd, lw):
    batch, seq, w = r.shape
    chunk = RWKV_CHUNK
    assert chunk == HEAD_DIM and RWKV_HEADS % RWKV_PACK == 0
    nc = seq // chunk
    pw = RWKV_PACK * HEAD_DIM
    groups = RWKV_HEADS // RWKV_PACK
    fwd = pl.BlockSpec((None, chunk, w), lambda b, c: (b, c, 0))
    bwd = pl.BlockSpec((None, chunk, w), lambda b, c: (b, nc - 1 - c, 0))
    fwd_dir = pl.BlockSpec((None, None, chunk, w), lambda b, c: (0, b, c, 0))
    bwd_dir = pl.BlockSpec((None, None, chunk, w), lambda b, c: (1, b, nc - 1 - c, 0))
    return pl.pallas_call(
        _rwkv_scan_kernel,
        name="rwkv_scan",
        grid=(batch, nc),
        in_specs=[fwd, bwd, fwd, bwd] + [fwd_dir, bwd_dir] * 4,
        out_specs=[fwd, bwd],
        out_shape=[jax.ShapeDtypeStruct((batch, seq, w), F32)] * 2,
        scratch_shapes=[pltpu.VMEM((2, groups, pw, pw), F32)],
        compiler_params=_params("parallel", "arbitrary"),
    )(r, r, v, v, kk, kk, bb, bb, kd, kd, lw, lw)


def _rwkv_post_kernel(yf_ref, yb_ref, bonus_ref, g_ref, lng_ref, lnb_ref, gsum_ref, o_ref):
    gsum = gsum_ref[...]
    y = yf_ref[...] + yb_ref[...]
    mean = _head_sum(y, gsum) * (1.0 / HEAD_DIM)
    yc = y - mean
    var = _head_sum(yc * yc, gsum) * (1.0 / HEAD_DIM)
    yn = yc * lax.rsqrt(var + RWKV_GN_EPS) * lng_ref[...] + lnb_ref[...]
    o_ref[...] = ((yn + bonus_ref[...]) * g_ref[...]).astype(o_ref.dtype)


def _rwkv_post(y_fwd, y_bwd, bonus, g, ln_g, ln_b, gsum):
    batch, seq, w = y_fwd.shape
    tt = min(512, seq)
    shared = pl.BlockSpec((None, tt, w), lambda b, i: (b, i, 0))
    row = pl.BlockSpec((1, w), lambda b, i: (0, 0))
    return pl.pallas_call(
        _rwkv_post_kernel,
        name="rwkv_post",
        grid=(batch, seq // tt),
        in_specs=[shared, shared, shared, shared, row, row,
                  pl.BlockSpec(gsum.shape, lambda b, i: (0, 0))],
        out_specs=shared,
        out_shape=jax.ShapeDtypeStruct((batch, seq, w), BF16),
        compiler_params=_params("parallel", "parallel"),
    )(y_fwd, y_bwd, bonus, g, ln_g.reshape(1, w), ln_b.reshape(1, w), gsum)


def _merge_kernel(ga_ref, gb_ref, gc_ref, oa0_ref, oa1_ref, oa2_ref, la0_ref, la1_ref, la2_ref,
                  ob_ref, oc_ref, x_ref, wa_ref, wb_ref, wc_ref, wo_ref, mod_ref, g2_ref, rw_ref, rb_ref,
                  x_out, h_out, idx_out, prob_out, *token_order):
    in_token_order = []
    for src_ref, dst_ref in zip((oa1_ref, la1_ref, oa2_ref, la2_ref), token_order):
        dil, per = src_ref.shape[0], src_ref.shape[1]
        for r in range(dil):
            for cb in range(dst_ref.shape[0]):
                dst_ref[cb, pl.ds(r, per, stride=dil), :] = src_ref[r, :, cb * LANES:(cb + 1) * LANES]
        in_token_order.append(jnp.concatenate([dst_ref[cb] for cb in range(dst_ref.shape[0])], axis=-1))
    oa1, la1, oa2, la2 = in_token_order
    lses = [la0_ref[...], la1, la2]
    m = jnp.maximum(jnp.maximum(lses[0], lses[1]), lses[2])
    es = [jnp.exp(l - m) for l in lses]
    inv = 1.0 / (es[0] + es[1] + es[2])
    o_a = (es[0] * oa0_ref[...] + es[1] * oa1 + es[2] * oa2) * inv
    merged = (_sigmoid(ga_ref[...].astype(F32)) * _dot(o_a.astype(BF16), wa_ref[...])
              + _sigmoid(gb_ref[...].astype(F32)) * _dot(ob_ref[...], wb_ref[...])
              + _sigmoid(gc_ref[...].astype(F32)) * _dot(oc_ref[...], wc_ref[...]))
    x = x_ref[...] + mod_ref[2] * _dot(merged.astype(BF16), wo_ref[...])
    x_out[...] = x
    h = _modulated_norm(x, g2_ref[...], mod_ref[4], mod_ref[3])
    h_out[...] = h.astype(h_out.dtype)

    logits = _dot(h, rw_ref[...], precision=HIGHEST) + rb_ref[...]
    lane = lax.broadcasted_iota(jnp.int32, logits.shape, 1)
    work = logits
    vals, idxs = [], []
    for _ in range(TOP_K):
        top = jnp.max(work, axis=-1, keepdims=True)
        first = jnp.min(jnp.where(work == top, lane, N_EXPERTS), axis=-1, keepdims=True)
        vals.append(top)
        idxs.append(first)
        work = jnp.where(lane == first, -jnp.inf, work)
    exps = [jnp.exp(t - vals[0]) for t in vals]
    denom = exps[0] + exps[1] + exps[2] + exps[3]
    idx_out[...] = jnp.concatenate(idxs, axis=-1)
    prob_out[...] = jnp.concatenate([e / denom for e in exps], axis=-1)


def _merge(proj, swa_outs, o_b, o_c, x, wa, wb, wc, wo, mod_l, g2, router_w, router_b):
    batch, seq, d = x.shape
    tm = min(512, seq)
    gw = SWA_GROUP_WIDTH

    def rows(width, blk=0):
        return pl.BlockSpec((None, tm, width), lambda b, i: (b, i, blk))

    def full(shape):
        return pl.BlockSpec(shape, lambda b, i: (0,) * len(shape))

    def residue_rows(dil):
        return pl.BlockSpec((None, dil, tm // dil, gw), lambda b, i: (b, 0, i, 0))

    o_list = [o for o, _ in swa_outs]
    l_list = [l for _, l in swa_outs]
    swa_specs = [rows(gw), residue_rows(SWA_PATTERNS[1][1]), residue_rows(SWA_PATTERNS[2][1])]
    return pl.pallas_call(
        _merge_kernel,
        name="merge",
        grid=(batch, seq // tm),
        in_specs=[rows(d, 0), rows(d, 1), rows(d, 2)] + swa_specs * 2
        + [rows(RWKV_WIDTH), rows(DIFF_WIDTH), rows(d), full(wa.shape), full(wb.shape), full(wc.shape),
           full(wo.shape), pl.BlockSpec((None, 6, 1, d), lambda b, i: (b, 0, 0, 0)), full((1, d)),
           full(router_w.shape), full((1, N_EXPERTS))],
        out_specs=[rows(d), rows(d), rows(TOP_K), rows(TOP_K)],
        out_shape=[jax.ShapeDtypeStruct((batch, seq, d), F32), jax.ShapeDtypeStruct((batch, seq, d), BF16),
                   jax.ShapeDtypeStruct((batch, seq, TOP_K), jnp.int32),
                   jax.ShapeDtypeStruct((batch, seq, TOP_K), F32)],
        scratch_shapes=[pltpu.VMEM((gw // LANES, tm, LANES), F32)] * 4,
        compiler_params=_params("parallel", "parallel"),
    )(proj, proj, proj, *o_list, *l_list, o_b, o_c, x, wa, wb, wc, wo, mod_l, g2.reshape(1, d),
      router_w, router_b.reshape(1, N_EXPERTS))


def _expert_kernel(tile_expert_ref, tile_valid_ref, tile_first_ref, x_ref, w1_ref, b1_ref, w2_ref, b2_ref, o_ref,
                   w1_bf16_ref, w2_pairs_ref):
    t = pl.program_id(0)

    @pl.when(tile_first_ref[t] != 0)
    def _():
        w1_bf16_ref[...] = w1_ref[...].astype(BF16)
        rounded = w2_ref[...].astype(BF16).astype(F32)
        high = pltpu.bitcast(rounded, jnp.uint32) & jnp.uint32(0xFFFF0000)
        w2_pairs_ref[...] = high | (high >> 16)

    @pl.when(tile_valid_ref[t] != 0)
    def _():
        hh = _dot(x_ref[...], w1_bf16_ref[...]) + b1_ref[...]
        gated = (lax.broadcasted_iota(jnp.int32, hh.shape, 1) % 2) == 0
        glu = jnp.minimum(hh, SWIGLU_LIMIT)
        part = jnp.where(gated, glu * _sigmoid(SWIGLU_ALPHA * glu), jnp.clip(hh, -SWIGLU_LIMIT, SWIGLU_LIMIT) + 1.0)
        act = jnp.where(gated, part * pltpu.roll(part, hh.shape[1] - 1, axis=1), 0.0)
        w2_rows = pltpu.bitcast(w2_pairs_ref[...], BF16)
        o_ref[...] = (_dot(act.astype(BF16), w2_rows) + b2_ref[...]).astype(o_ref.dtype)

    @pl.when(tile_valid_ref[t] == 0)
    def _():
        o_ref[...] = jnp.zeros_like(o_ref)


def _experts(xg, tile_expert, tile_valid, tile_first, w1, b1, w2, b2, layer):
    rows, d = xg.shape
    f2 = w1.shape[3]
    tm = MOE_ROW_TILE
    grid_spec = pltpu.PrefetchScalarGridSpec(
        num_scalar_prefetch=3,
        grid=(rows // tm,),
        in_specs=[
            pl.BlockSpec((tm, d), lambda t, te, tv, tf: (t, 0)),
            pl.BlockSpec((None, None, d, f2), lambda t, te, tv, tf: (layer, te[t], 0, 0)),
            pl.BlockSpec((None, None, 1, f2), lambda t, te, tv, tf: (layer, te[t], 0, 0)),
            pl.BlockSpec((None, None, f2 // 2, d), lambda t, te, tv, tf: (layer, te[t], 0, 0)),
            pl.BlockSpec((None, None, 1, d), lambda t, te, tv, tf: (layer, te[t], 0, 0)),
        ],
        out_specs=pl.BlockSpec((tm, d), lambda t, te, tv, tf: (t, 0)),
        scratch_shapes=[pltpu.VMEM((d, f2), BF16), pltpu.VMEM((f2 // 2, d), jnp.uint32)],
    )
    return pl.pallas_call(
        _expert_kernel,
        name="moe_experts",
        grid_spec=grid_spec,
        out_shape=jax.ShapeDtypeStruct((rows, d), BF16),
        compiler_params=pltpu.CompilerParams(dimension_semantics=("arbitrary",), vmem_limit_bytes=EXPERT_VMEM_LIMIT),
    )(tile_expert, tile_valid, tile_first, xg, w1, b1, w2, b2)


def _combine_kernel(y_ref, p_ref, x_ref, mod_ref, g_ref, o_ref, *, final):
    p = p_ref[...]
    d = x_ref.shape[1]
    acc = p[:, 0:1] * y_ref[:, 0:d].astype(F32)
    for j in range(1, TOP_K):
        acc = acc + p[:, j:j + 1] * y_ref[:, j * d:(j + 1) * d].astype(F32)
    x = x_ref[...] + mod_ref[5] * acc
    if final:
        x = x * lax.rsqrt(jnp.mean(x * x, axis=-1, keepdims=True) + NORM_EPS) * g_ref[...]
    o_ref[...] = x


def _combine(y4, probs, x, mod_l, final_g, final):
    batch, seq, d = x.shape
    tm = min(512, seq)
    return pl.pallas_call(
        functools.partial(_combine_kernel, final=final),
        name="moe_combine",
        grid=(batch, seq // tm),
        in_specs=[
            pl.BlockSpec((None, tm, TOP_K * d), lambda b, i: (b, i, 0)),
            pl.BlockSpec((None, tm, TOP_K), lambda b, i: (b, i, 0)),
            pl.BlockSpec((None, tm, d), lambda b, i: (b, i, 0)),
            pl.BlockSpec((None, 6, 1, d), lambda b, i: (b, 0, 0, 0)),
            pl.BlockSpec((1, d), lambda b, i: (0, 0)),
        ],
        out_specs=pl.BlockSpec((None, tm, d), lambda b, i: (b, i, 0)),
        out_shape=jax.ShapeDtypeStruct((batch, seq, d), F32),
        compiler_params=_params("parallel", "parallel"),
    )(y4, probs, x, mod_l, final_g.reshape(1, d))


def _dispatch_plan(idx, n_rows_padded):
    tm = MOE_ROW_TILE
    flat = idx.reshape(-1)
    experts = jnp.arange(N_EXPERTS, dtype=jnp.int32)
    order = jnp.argsort(flat, stable=True).astype(jnp.int32)
    position = jnp.argsort(order).astype(jnp.int32)
    counts = jnp.sum((flat[:, None] == experts[None, :]).astype(jnp.int32), axis=0)
    padded = ((counts + tm - 1) // tm) * tm
    padded_end = jnp.cumsum(padded)
    padded_start = padded_end - padded
    start = jnp.cumsum(counts) - counts
    slot_row = padded_start[flat] + position - start[flat]
    tile_start = jnp.arange(n_rows_padded // tm, dtype=jnp.int32) * tm
    tile_expert = jnp.minimum(jnp.sum((padded_end[None, :] <= tile_start[:, None]).astype(jnp.int32), axis=1),
                              N_EXPERTS - 1)
    tile_valid = (tile_start < padded_end[-1]).astype(jnp.int32)
    previous = jnp.concatenate([jnp.full((1,), -1, jnp.int32), tile_expert[:-1]])
    tile_first = tile_valid * (tile_expert != previous).astype(jnp.int32)
    row = jnp.arange(n_rows_padded, dtype=jnp.int32)
    row_expert = jnp.repeat(tile_expert, tm)
    rank = row - padded_start[row_expert]
    routed = rank < counts[row_expert]
    src_slot = order[jnp.clip(start[row_expert] + rank, 0, flat.shape[0] - 1)]
    src_token = jnp.where(routed, src_slot // TOP_K, 0)
    return src_token, slot_row, tile_expert, tile_valid, tile_first


def _moe(h2, idx, probs, x, mod_l, w1, b1, w2, b2, layer, final_g, final):
    batch, seq, d = x.shape
    n_tok = batch * seq
    n_rows_padded = n_tok * TOP_K + N_EXPERTS * MOE_ROW_TILE
    src_token, slot_row, tile_expert, tile_valid, tile_first = _dispatch_plan(idx, n_rows_padded)
    xg = jnp.take(h2.reshape(n_tok, d), src_token, axis=0)
    y = _experts(xg, tile_expert, tile_valid, tile_first, w1, b1, w2, b2, layer)
    y4 = jnp.take(y, slot_row, axis=0).reshape(batch, seq, TOP_K * d)
    return _combine(y4, probs, x, mod_l, final_g, final)


def _t5_bucket(rel):
    nb = REL_BUCKETS // 2
    max_exact = nb // 2
    ret = jnp.where(rel > 0, nb, 0)
    n = jnp.abs(rel)
    nf = jnp.maximum(n, 1).astype(F32)
    large = max_exact + (jnp.log(nf / max_exact) / math.log(REL_MAX_DIST / max_exact)
                         * (nb - max_exact)).astype(jnp.int32)
    large = jnp.minimum(large, nb - 1)
    return ret + jnp.where(n < max_exact, n, large)


def _swa_bias_tiles(rel_bias):
    half, qb = SWA_HALF, SWA_QUERY_BLOCK
    a = jnp.arange(qb)[:, None]
    c = jnp.arange(qb + 2 * half)[None, :]
    j = c - half - a
    tiles = []
    for g, (_, dil) in enumerate(SWA_PATTERNS):
        offs = jnp.arange(-half, half + 1) * dil
        table = rel_bias[_t5_bucket(offs)][:, g * SWA_GROUP_HEADS:(g + 1) * SWA_GROUP_HEADS].T
        tile = table[:, jnp.clip(j + half, 0, 2 * half)]
        tiles.append(jnp.where((jnp.abs(j) <= half)[None], tile, NEG_INF).astype(F32))
    return tiles


def _diff_bias_segments(rel_bias, seq):
    tq = min(DIFF_Q_TILE, seq)
    table = rel_bias[:, len(SWA_PATTERNS) * SWA_GROUP_HEADS:].T
    by_rel = table[:, _t5_bucket(jnp.arange(2 * seq) - seq)]
    segs = [by_rel[:, seq - (i + 1) * tq:2 * seq - i * tq] for i in range(seq // tq)]
    return jnp.stack(segs, axis=1)[:, :, None, :].astype(F32)


def _pack_w_in(w_in_l):
    a_w = 3 * SWA_WIDTH
    b_w = 3 * RWKV_WIDTH + RWKV_LORA
    c_w = 3 * DIFF_WIDTH
    a, b, c, gates = (w_in_l[:, :a_w], w_in_l[:, a_w:a_w + b_w], w_in_l[:, a_w + b_w:a_w + b_w + c_w],
                      w_in_l[:, a_w + b_w + c_w:])
    gw = SWA_GROUP_WIDTH

    def group(g):
        return [a[:, t * SWA_WIDTH + g * gw:t * SWA_WIDTH + (g + 1) * gw] for t in range(3)]

    pad = jnp.zeros((w_in_l.shape[0], PROJ_COLS_PADDED - PROJ_COLS), w_in_l.dtype)
    return jnp.concatenate([gates] + group(1) + group(2) + group(0) + [c, b, pad], axis=1).astype(BF16)


def _direction_padded(w_up):
    z = jnp.zeros_like(w_up[0])
    return jnp.stack([jnp.concatenate([w_up[0], z], axis=0), jnp.concatenate([z, w_up[1]], axis=0)])


def kernel(x, c, w_mod, b_mod, norm1_g, norm2_g, w_in, rwkv_mu, rwkv_w0, rwkv_w_up, rwkv_a0, rwkv_a_up, rwkv_g_up, rwkv_k_k, rwkv_k_a, rwkv_r_k, rwkv_ln_g, rwkv_ln_b, diff_lambda, diff_subln_g, rel_bias, w_branch_a, w_branch_b, w_branch_c, w_out, router_w, router_b, moe_w1, moe_b1, moe_w2, moe_b2, final_norm_g):
    batch, seq, d = x.shape
    depth = w_mod.shape[0]
    mod = _modulation(c, w_mod, b_mod)
    swa_tiles = _swa_bias_tiles(rel_bias)
    bias_segs = _diff_bias_segments(rel_bias, seq)
    head_of = jnp.arange(RWKV_WIDTH) // HEAD_DIM
    gsum = (head_of[:, None] == head_of[None, :]).astype(BF16)
    b1 = moe_b1[:, :, None, :]
    b2 = moe_b2[:, :, None, :]

    for l in range(depth):
        mod_l = mod[l]
        proj, res1, res2 = _norm_proj(x, norm1_g[l], mod_l, _pack_w_in(w_in[l]))
        swa_outs = [_swa_group(proj, swa_tiles[0], 1, COL_A0 // SWA_GROUP_WIDTH),
                    _swa_group(res1, swa_tiles[1], SWA_PATTERNS[1][1], 0),
                    _swa_group(res2, swa_tiles[2], SWA_PATTERNS[2][1], 0)]
        r, v, g, bonus, kk, bb, kd, lw = _rwkv_prep(
            proj, rwkv_mu[l], rwkv_w0[l], _direction_padded(rwkv_w_up[l]), rwkv_a0[l],
            _direction_padded(rwkv_a_up[l]), rwkv_g_up[l], rwkv_k_k[l], rwkv_k_a[l], rwkv_r_k[l], gsum)
        y_fwd, y_bwd = _rwkv_scan(r, v, kk, bb, kd, lw)
        o_b = _rwkv_post(y_fwd, y_bwd, bonus, g, rwkv_ln_g[l], rwkv_ln_b[l], gsum)
        lambda_init = 0.8 - 0.6 * math.exp(-0.3 * l)
        o_c = _diff_attention(proj, bias_segs, diff_lambda[l], diff_subln_g[l], lambda_init)
        x, h2, idx, probs = _merge(
            proj, swa_outs, o_b, o_c, x, w_branch_a[l].astype(BF16), w_branch_b[l].astype(BF16),
            w_branch_c[l].astype(BF16), w_out[l].astype(BF16), mod_l, norm2_g[l], router_w[l], router_b[l])
        x = _moe(h2, idx, probs, x, mod_l, moe_w1, b1, moe_w2, b2, l, final_norm_g, l == depth - 1)
    return x
```

```python
import functools
import math

import jax
import jax.numpy as jnp
from jax import lax
from jax.experimental import pallas as pl
from jax.experimental.pallas import tpu as pltpu

F32 = jnp.float32
BF16 = jnp.bfloat16
HIGHEST = lax.Precision.HIGHEST

D_MODEL = 1024
HEAD_DIM = 64
LANES = 128
NORM_EPS = 1e-6
NEG_INF = -1e30
LOG2_E = 1.4426950408889634

SWA_PATTERNS = ((128, 1), (512, 4), (2048, 16))
SWA_GROUP_HEADS = 4
SWA_GROUP_WIDTH = SWA_GROUP_HEADS * HEAD_DIM
SWA_WIDTH = len(SWA_PATTERNS) * SWA_GROUP_WIDTH
SWA_HALF = 64
SWA_QUERY_BLOCK = 128

RWKV_HEADS = 12
RWKV_WIDTH = RWKV_HEADS * HEAD_DIM
RWKV_LORA = 384
RWKV_GN_EPS = 64e-5
RWKV_CHUNK = 64
RWKV_BATCH_BLOCK = 2
RWKV_PACK = 4

DIFF_HEADS = 6
DIFF_WIDTH = DIFF_HEADS * 2 * HEAD_DIM
DIFF_SUBLN_EPS = 1e-5
DIFF_Q_TILE = 512

REL_BUCKETS = 32
REL_MAX_DIST = 128

N_EXPERTS = 32
TOP_K = 4
D_EXPERT = 1024
SWIGLU_LIMIT = 7.0
SWIGLU_ALPHA = 1.702
MOE_ROW_TILE = 256

COL_GATES = 0
COL_A_DILATED = 3 * D_MODEL
COL_A0 = COL_A_DILATED + 2 * SWA_WIDTH
COL_C = COL_A0 + SWA_WIDTH
COL_R = COL_C + 3 * DIFF_WIDTH
COL_LORA = COL_R + 3 * RWKV_WIDTH
PROJ_COLS = COL_LORA + RWKV_LORA
PROJ_COLS_PADDED = 10752
PROJ_COL_TILE = 1536
STRIDED_COL_TILE = COL_A_DILATED // PROJ_COL_TILE
PROJ_ROW_TILE = 1024

VMEM_LIMIT = 48 * 1024 * 1024
EXPERT_VMEM_LIMIT = 56 * 1024 * 1024

_NT = ((1,), (1,))
_TN = ((0,), (0,))


def _params(*sem):
    return pltpu.CompilerParams(dimension_semantics=sem, vmem_limit_bytes=VMEM_LIMIT)


def _sigmoid(x):
    return 1.0 / (1.0 + jnp.exp(-x))


def _dot(a, b, dims=((1,), (0,)), precision=None):
    return lax.dot_general(a, b, (dims, ((), ())), precision=precision, preferred_element_type=F32)


def _hi_lo(x):
    hi = x.astype(BF16)
    return hi, (x - hi.astype(F32)).astype(BF16)


def _dot3(a, b):
    a_hi, a_lo = _hi_lo(a)
    b_hi, b_lo = _hi_lo(b)
    return _dot(a_hi, b_hi) + _dot(a_hi, b_lo) + _dot(a_lo, b_hi)


def _head_sum(x, ones_bf16):
    hi, lo = _hi_lo(x)
    return _dot(hi, ones_bf16) + _dot(lo, ones_bf16)


def _mod_kernel(c_ref, w_ref, b_ref, o_ref):
    c = c_ref[...]
    cond = c * _sigmoid(c)
    o_ref[...] = _dot(cond, w_ref[...], precision=HIGHEST) + b_ref[...]


def _modulation(c, w_mod, b_mod):
    n_layers, d, n = w_mod.shape
    batch = c.shape[0]
    tn = 1536
    out = pl.pallas_call(
        _mod_kernel,
        name="modulation",
        grid=(n_layers, n // tn),
        in_specs=[
            pl.BlockSpec((batch, d), lambda l, j: (0, 0)),
            pl.BlockSpec((None, d, tn), lambda l, j: (l, 0, j)),
            pl.BlockSpec((None, 1, tn), lambda l, j: (l, 0, j)),
        ],
        out_specs=pl.BlockSpec((None, batch, tn), lambda l, j: (l, 0, j)),
        out_shape=jax.ShapeDtypeStruct((n_layers, batch, n), F32),
        compiler_params=_params("parallel", "parallel"),
    )(c, w_mod, b_mod.reshape(n_layers, 1, n))
    return out.reshape(n_layers, batch, 6, 1, d)


def _modulated_norm(x, g, scale, shift):
    y = x * lax.rsqrt(jnp.mean(x * x, axis=-1, keepdims=True) + NORM_EPS) * g
    return y * (1.0 + scale) + shift


def _norm_proj_kernel(x_ref, g_ref, mod_ref, w_ref, o_ref, res4_ref, res16_ref, h_ref, acc_ref):
    j = pl.program_id(2)

    @pl.when(j == 0)
    def _():
        h_ref[...] = _modulated_norm(x_ref[...], g_ref[...], mod_ref[1], mod_ref[0]).astype(BF16)

    @pl.when(j != STRIDED_COL_TILE)
    def _():
        o_ref[...] = _dot(h_ref[...], w_ref[...]).astype(o_ref.dtype)

    @pl.when(j == STRIDED_COL_TILE)
    def _():
        acc = _dot(h_ref[...], w_ref[...])
        o_ref[...] = acc.astype(o_ref.dtype)
        lanes = acc_ref.shape[2]
        tm = acc_ref.shape[1]
        for cb in range(acc_ref.shape[0]):
            acc_ref[cb] = acc[:, cb * lanes:(cb + 1) * lanes]
        per_group = SWA_WIDTH // lanes
        for res_ref, dil, first in ((res4_ref, SWA_PATTERNS[1][1], 0), (res16_ref, SWA_PATTERNS[2][1], per_group)):
            for r in range(dil):
                for cb in range(per_group):
                    res_ref[r, :, cb * lanes:(cb + 1) * lanes] = (
                        acc_ref[first + cb, pl.ds(r, tm // dil, stride=dil), :].astype(res_ref.dtype))


def _norm_proj(x, g, mod_l, w):
    batch, seq, d = x.shape
    n = w.shape[1]
    tm, tn = min(PROJ_ROW_TILE, seq), PROJ_COL_TILE
    d1, d2 = SWA_PATTERNS[1][1], SWA_PATTERNS[2][1]

    def res_spec(dil):
        return pl.BlockSpec((None, dil, tm // dil, SWA_WIDTH), lambda b, i, j: (b, 0, i, 0))

    return pl.pallas_call(
        _norm_proj_kernel,
        name="norm_proj",
        grid=(batch, seq // tm, n // tn),
        in_specs=[
            pl.BlockSpec((None, tm, d), lambda b, i, j: (b, i, 0)),
            pl.BlockSpec((1, d), lambda b, i, j: (0, 0)),
            pl.BlockSpec((None, 6, 1, d), lambda b, i, j: (b, 0, 0, 0)),
            pl.BlockSpec((d, tn), lambda b, i, j: (0, j)),
        ],
        out_specs=[pl.BlockSpec((None, tm, tn), lambda b, i, j: (b, i, j)), res_spec(d1), res_spec(d2)],
        out_shape=[jax.ShapeDtypeStruct((batch, seq, n), BF16),
                   jax.ShapeDtypeStruct((batch, d1, seq // d1, SWA_WIDTH), BF16),
                   jax.ShapeDtypeStruct((batch, d2, seq // d2, SWA_WIDTH), BF16)],
        scratch_shapes=[pltpu.VMEM((tm, d), BF16), pltpu.VMEM((tn // LANES, tm, LANES), F32)],
        compiler_params=_params("parallel", "parallel", "arbitrary"),
    )(x, g.reshape(1, d), mod_l, w)


def _swa_kernel(q_ref, k_ref, v_ref, bias_ref, o_ref, lse_ref, kpad_ref, vpad_ref, *, length):
    half, qb, hd = SWA_HALF, SWA_QUERY_BLOCK, HEAD_DIM
    win = qb + 2 * half
    zeros = jnp.zeros((half, SWA_GROUP_WIDTH), BF16)
    for pad_ref, src_ref in ((kpad_ref, k_ref), (vpad_ref, v_ref)):
        pad_ref[0:half, :] = zeros
        pad_ref[half + length:half + length + half, :] = zeros
        pad_ref[half:half + length, :] = src_ref[...]

    def block(n, carry):
        q0 = pl.multiple_of(n * qb, qb)
        q = q_ref[pl.ds(q0, qb), :]
        kw = kpad_ref[pl.ds(q0, win), :]
        vw = vpad_ref[pl.ds(q0, win), :]
        key_pos = q0 - half + lax.broadcasted_iota(jnp.int32, (1, win), 1)
        in_range = (key_pos >= 0) & (key_pos < length)
        heads = [slice(h * hd, (h + 1) * hd) for h in range(SWA_GROUP_HEADS)]
        scores = [jnp.where(in_range, _dot(q[:, sl], kw[:, sl], _NT) * (hd ** -0.5) + bias_ref[h], NEG_INF)
                  for h, sl in enumerate(heads)]
        maxes = [jnp.max(s, axis=-1, keepdims=True) for s in scores]
        probs = [jnp.exp(s - m) for s, m in zip(scores, maxes)]
        sums = [jnp.sum(p, axis=-1, keepdims=True) for p in probs]
        outs = [_dot(p.astype(BF16), vw[:, sl]) / l for p, sl, l in zip(probs, heads, sums)]
        lses = [jnp.broadcast_to(m + jnp.log(l), (qb, hd)) for m, l in zip(maxes, sums)]
        o_ref[pl.ds(q0, qb), :] = jnp.concatenate(outs, axis=-1)
        lse_ref[pl.ds(q0, qb), :] = jnp.concatenate(lses, axis=-1)
        return carry

    lax.fori_loop(0, length // qb, block, 0)


def _swa_group(qkv, bias_tile, dilation, col_block):
    gw = SWA_GROUP_WIDTH
    if dilation == 1:
        batch, length, _ = qkv.shape
        grid = (batch, 1)
        block = (None, length, gw)
        out_dims = (batch, length, gw)

        def at(col):
            return lambda b, r: (b, 0, col)
    else:
        batch, _, length, _ = qkv.shape
        grid = (batch, dilation)
        block = (None, None, length, gw)
        out_dims = (batch, dilation, length, gw)

        def at(col):
            return lambda b, r: (b, r, 0, col)

    out_spec = pl.BlockSpec(block, at(0))
    out_shape = jax.ShapeDtypeStruct(out_dims, F32)
    return pl.pallas_call(
        functools.partial(_swa_kernel, length=length),
        name="swa",
        grid=grid,
        in_specs=[pl.BlockSpec(block, at(col_block + which)) for which in range(3)]
        + [pl.BlockSpec(bias_tile.shape, lambda b, r: (0, 0, 0))],
        out_specs=[out_spec, out_spec],
        out_shape=[out_shape, out_shape],
        scratch_shapes=[pltpu.VMEM((length + 2 * SWA_HALF, gw), BF16)] * 2,
        compiler_params=_params("parallel", "parallel"),
    )(qkv, qkv, qkv, bias_tile)


def _diff_kernel(q_ref, k_ref, v_ref, seg_ref, dl_ref, g_ref, o_ref, bias_ref, *, lambda_init):
    hd = HEAD_DIM
    tq, seq = bias_ref.shape

    @pl.when(pl.program_id(2) == 0)
    def _():
        seg = jnp.broadcast_to(seg_ref[...] * LOG2_E, (tq, seq + tq))
        bias_ref[...] = pltpu.roll(seg, seq, 1, stride=1, stride_axis=0)[:, :seq]

    dl = dl_ref[...]
    lam = (jnp.exp(jnp.sum(dl[0:1] * dl[1:2], axis=-1, keepdims=True))
           - jnp.exp(jnp.sum(dl[2:3] * dl[3:4], axis=-1, keepdims=True)) + lambda_init)
    k = k_ref[...]
    v = v_ref[...]
    q = (q_ref[...].astype(F32) * (hd ** -0.5 * LOG2_E)).astype(BF16)
    chains = [(slice(r0, r0 + 128), slice(comp * hd, (comp + 1) * hd)) for r0 in range(0, tq, 128) for comp in range(2)]
    scores = [_dot(q[rows, sl], k[:, sl], _NT) + bias_ref[rows, :] for rows, sl in chains]
    probs = [jnp.exp2(s - jnp.max(s, axis=-1, keepdims=True)) for s in scores]
    outs = [_dot(p.astype(BF16), v) * (1.0 / jnp.sum(p, axis=-1, keepdims=True)) for p in probs]
    for i in range(0, len(chains), 2):
        o = outs[i] - lam * outs[i + 1]
        o = o * lax.rsqrt(jnp.mean(o * o, axis=-1, keepdims=True) + DIFF_SUBLN_EPS) * g_ref[...]
        o_ref[chains[i][0], :] = (o * (1.0 - lambda_init)).astype(o_ref.dtype)


def _diff_attention(proj, bias_segs, diff_lambda_l, subln_g, lambda_init):
    batch, seq, _ = proj.shape
    hw = 2 * HEAD_DIM
    tq = min(DIFF_Q_TILE, seq)
    base = COL_C // hw
    return pl.pallas_call(
        functools.partial(_diff_kernel, lambda_init=lambda_init),
        name="diff_attn",
        grid=(DIFF_HEADS, seq // tq, batch),
        in_specs=[
            pl.BlockSpec((None, tq, hw), lambda h, i, b: (b, i, base + h)),
            pl.BlockSpec((None, seq, hw), lambda h, i, b: (b, 0, base + DIFF_HEADS + h)),
            pl.BlockSpec((None, seq, hw), lambda h, i, b: (b, 0, base + 2 * DIFF_HEADS + h)),
            pl.BlockSpec((None, None, 1, seq + tq), lambda h, i, b: (h, i, 0, 0)),
            pl.BlockSpec((4, HEAD_DIM), lambda h, i, b: (0, 0)),
            pl.BlockSpec((1, hw), lambda h, i, b: (0, 0)),
        ],
        out_specs=pl.BlockSpec((None, tq, hw), lambda h, i, b: (b, i, h)),
        out_shape=jax.ShapeDtypeStruct((batch, seq, DIFF_WIDTH), BF16),
        scratch_shapes=[pltpu.VMEM((tq, seq), F32)],
        compiler_params=_params("parallel", "parallel", "arbitrary"),
    )(proj, proj, proj, bias_segs, diff_lambda_l, subln_g.reshape(1, hw))


def _shifted(cur_ref, prev_ref, next_ref, mu, first, last):
    x = cur_ref[...].astype(F32)
    rows = x.shape[0]
    halo = prev_ref.shape[0]
    before = jnp.where(first, 0.0, prev_ref[halo - 1:halo, :].astype(F32))
    after = jnp.where(last, 0.0, next_ref[0:1, :].astype(F32))
    row = lax.broadcasted_iota(jnp.int32, (rows, 1), 0)
    prev = jnp.where(row == 0, before, pltpu.roll(x, 1, axis=0))
    nxt = jnp.where(row == rows - 1, after, pltpu.roll(x, rows - 1, axis=0))
    return x + mu[0:1] * (prev - x) + mu[1:2] * (nxt - x)


def _rwkv_prep_kernel(r_ref, rp_ref, rn_ref, k_ref, kp_ref, kn_ref, v_ref, vp_ref, vn_ref,
                      lo_ref, lop_ref, lon_ref, mu_ref, w0_ref, wup_ref, a0_ref, aup_ref, gup_ref,
                      kk_scale_ref, ka_ref, rk_ref, gsum_ref,
                      r_out, v_out, g_out, bonus_out, kk_out, bb_out, kd_out, lw_out):
    w = RWKV_WIDTH
    first = pl.program_id(1) == 0
    last = pl.program_id(1) == pl.num_programs(1) - 1
    mu = mu_ref[...]
    r = _shifted(r_ref, rp_ref, rn_ref, mu[:, 0:w], first, last)
    k = _shifted(k_ref, kp_ref, kn_ref, mu[:, w:2 * w], first, last)
    v = _shifted(v_ref, vp_ref, vn_ref, mu[:, 2 * w:3 * w], first, last)
    lora = _shifted(lo_ref, lop_ref, lon_ref, mu[:, 3 * w:3 * w + RWKV_LORA], first, last)
    decay_in = jnp.tanh(lora[:, 0:128])
    iclr_in = lora[:, 128:256]
    gsum = gsum_ref[...]
    r_out[...] = r
    v_out[...] = v
    g_out[...] = _dot3(_sigmoid(lora[:, 256:384]), gup_ref[...])
    bonus = jnp.zeros_like(r)
    for di in range(2):
        z = w0_ref[di:di + 1, :] + _dot3(decay_in, wup_ref[di])
        u = -z
        softplus = jnp.maximum(u, 0.0) + jnp.log(1.0 + jnp.exp(-jnp.abs(u)))
        lw_out[di] = -jnp.exp(-softplus - 0.5)
        a = _sigmoid(a0_ref[di:di + 1, :] + _dot3(iclr_in, aup_ref[di]))
        kk = k * kk_scale_ref[di:di + 1, :]
        kk = kk * lax.rsqrt(jnp.maximum(_head_sum(kk * kk, gsum), 1e-24))
        kd = k * (1.0 + (a - 1.0) * ka_ref[di:di + 1, :])
        kk_out[di] = kk
        bb_out[di] = kk * a
        kd_out[di] = kd
        bonus = bonus + _head_sum(r * kd * rk_ref[...], gsum) * v
    bonus_out[...] = bonus


def _rwkv_prep(proj, mu, w0, wup2, a0, aup2, g_up, k_k, k_a, r_k, gsum):
    batch, seq, _ = proj.shape
    w = RWKV_WIDTH
    tt = min(256, seq)
    halo = 16
    hb = tt // halo
    last_halo = seq // halo - 1

    def cur(width, blk):
        return pl.BlockSpec((None, tt, width), lambda b, i: (b, i, blk))

    def prev(width, blk):
        return pl.BlockSpec((None, halo, width), lambda b, i: (b, jnp.maximum(i * hb - 1, 0), blk))

    def nxt(width, blk):
        return pl.BlockSpec((None, halo, width), lambda b, i: (b, jnp.minimum((i + 1) * hb, last_halo), blk))

    def full(shape):
        return pl.BlockSpec(shape, lambda b, i: (0,) * len(shape))

    in_specs = []
    for blk in (COL_R // w, COL_R // w + 1, COL_R // w + 2):
        in_specs += [cur(w, blk), prev(w, blk), nxt(w, blk)]
    lb = COL_LORA // RWKV_LORA
    in_specs += [cur(RWKV_LORA, lb), prev(RWKV_LORA, lb), nxt(RWKV_LORA, lb)]
    in_specs += [full(mu.shape), full(w0.shape), full(wup2.shape), full(a0.shape), full(aup2.shape),
                 full(g_up.shape), full(k_k.shape), full(k_a.shape), full((1, w)), full(gsum.shape)]
    shared = pl.BlockSpec((None, tt, w), lambda b, i: (b, i, 0))
    per_dir = pl.BlockSpec((2, None, tt, w), lambda b, i: (0, b, i, 0))
    shared_shape = jax.ShapeDtypeStruct((batch, seq, w), F32)
    per_dir_shape = jax.ShapeDtypeStruct((2, batch, seq, w), F32)
    return pl.pallas_call(
        _rwkv_prep_kernel,
        name="rwkv_prep",
        grid=(batch, seq // tt),
        in_specs=in_specs,
        out_specs=[shared] * 4 + [per_dir] * 4,
        out_shape=[shared_shape] * 4 + [per_dir_shape] * 4,
        compiler_params=_params("parallel", "parallel"),
    )(*([proj] * 12), mu, w0, wup2, a0, aup2, g_up, k_k, k_a, r_k.reshape(1, w), gsum)


def _rwkv_scan_kernel(rf_ref, rb_ref, vf_ref, vb_ref, kkf_ref, kkb_ref, bbf_ref, bbb_ref, kdf_ref, kdb_ref,
                      lwf_ref, lwb_ref, yf_ref, yb_ref, state_ref):
    chunk, hd, pack = RWKV_CHUNK, HEAD_DIM, RWKV_PACK
    pw = pack * hd
    groups = RWKV_HEADS // pack

    @pl.when(pl.program_id(1) == 0)
    def _():
        state_ref[...] = jnp.zeros_like(state_ref)

    ti = lax.broadcasted_iota(jnp.int32, (chunk, pw), 0)
    tj = lax.broadcasted_iota(jnp.int32, (chunk, pw), 1) % chunk
    eye = (ti == tj).astype(F32)
    same16 = (ti // 16) == (tj // 16)
    same32 = (ti // 32) == (tj // 32)
    diag = (lax.broadcasted_iota(jnp.int32, (pw, pw), 0) // hd) == (lax.broadcasted_iota(jnp.int32, (pw, pw), 1) // hd)
    diag_bf16 = diag.astype(BF16)

    def block_diag(x):
        return jnp.concatenate([x.astype(BF16)] * pack, axis=0) * diag_bf16

    def stacked(top, bottom):
        return jnp.concatenate([top, bottom], axis=0).astype(BF16)

    chains = []
    per_direction = ((rf_ref, vf_ref, kkf_ref, bbf_ref, kdf_ref, lwf_ref), (rb_ref, vb_ref, kkb_ref, bbb_ref, kdb_ref, lwb_ref))
    for n, (d, refs) in ((n, dr) for n in range(rf_ref.shape[0]) for dr in enumerate(per_direction)):
        r_ref, v_ref, kk_ref, bb_ref, kd_ref, lw_ref = (ref.at[n] for ref in refs)
        lag = ti - tj if d == 0 else tj - ti
        before, upto = lag > 0, lag >= 0
        lw = lw_ref[...]
        tri = upto[:, :chunk].astype(BF16)
        lw_hi = lw.astype(BF16)
        lw_rest = lw - lw_hi.astype(F32)
        lw_mid = lw_rest.astype(BF16)
        lw_lo = (lw_rest - lw_mid.astype(F32)).astype(BF16)
        cum = _dot(tri, lw_hi) + _dot(tri, lw_mid) + _dot(tri, lw_lo)
        total = jnp.sum(lw, axis=0, keepdims=True)
        p_inv = jnp.exp(-cum)
        p_rest = jnp.exp(total - cum)
        p_total = jnp.exp(total)
        kk, bb, kd = kk_ref[...], bb_ref[...], kd_ref[...]
        a_all = -kk * jnp.exp(cum - lw)
        b_all = bb * p_inv
        k_all = kd * p_inv
        r_all = r_ref[...] * jnp.exp(cum)
        b_end = bb * p_rest
        k_end = kd * p_rest
        v_all = v_ref[...]
        for g in range(groups):
            sl = slice(g * pw, (g + 1) * pw)
            chains.append(dict(
                n=n, d=d, g=g, before=before, upto=upto, ar=stacked(a_all[:, sl], r_all[:, sl]), b=b_all[:, sl],
                k=k_all[:, sl], v=v_all[:, sl], ends=stacked(b_end[:, sl], k_end[:, sl]), p_total=p_total[:, sl]))

    for c in chains:
        c["by_b"] = _dot(c["ar"], block_diag(c["b"]), _NT)
        c["by_k"] = _dot(c["ar"], block_diag(c["k"]), _NT)
    for c in chains:
        c["m_ab"] = jnp.where(c["before"], c["by_b"][:chunk], 0.0)
        c["m_rb"] = jnp.where(c["upto"], c["by_b"][chunk:], 0.0)
        c["m_kv"] = stacked(jnp.where(c["before"], c["by_k"][:chunk], 0.0), jnp.where(c["upto"], c["by_k"][chunk:], 0.0))
        c["x"] = jnp.where(same16, c["m_ab"], 0.0)
        c["inv"] = eye + c["x"]
    for c in chains:
        c["x"] = _dot(c["x"].astype(BF16), block_diag(c["x"]))
    for _ in range(2):
        for c in chains:
            both = _dot(stacked(c["x"], c["inv"]), block_diag(c["x"]))
            c["x"] = both[:chunk]
            c["inv"] = c["inv"] + both[chunk:]
    for c in chains:
        c["inv"] = c["inv"] + _dot(c["inv"].astype(BF16), block_diag(c["x"]))
    for level in range(2):
        for c in chains:
            off = jnp.where(same32 & ~same16, c["m_ab"], 0.0) if level == 0 else jnp.where(~same32, c["m_ab"], 0.0)
            c["inner"] = _dot(off.astype(BF16), block_diag(c["inv"]))
        for c in chains:
            c["inv"] = c["inv"] + _dot(c["inv"].astype(BF16), block_diag(c["inner"]))
    for c in chains:
        c["state"] = state_ref[c["n"], c["d"], c["g"]]
        c["by_state"] = _dot(c["ar"], c["state"].astype(BF16), _NT)
        c["by_v"] = _dot(c["m_kv"], block_diag(c["v"]))
    for c in chains:
        c["u"] = _dot(c["inv"].astype(BF16), block_diag(c["by_state"][:chunk] + c["by_v"][:chunk]))
    for c in chains:
        c["y"] = c["by_state"][chunk:] + c["by_v"][chunk:] + _dot(c["m_rb"].astype(BF16), block_diag(c["u"]))
        update = _dot(stacked(c["u"], c["v"]), c["ends"], _TN)
        state_ref[c["n"], c["d"], c["g"]] = c["state"] * c["p_total"] + jnp.where(diag, update, 0.0)
    for n in range(rf_ref.shape[0]):
        for d, y_ref in enumerate((yf_ref, yb_ref)):
            y_ref[n] = jnp.concatenate([c["y"] for c in chains if (c["n"], c["d"]) == (n, d)], axis=-1)


def _rwkv_scan(r, v, kk, bb, kd, lw):
    batch, seq, w = r.shape
    chunk = RWKV_CHUNK
    assert chunk == HEAD_DIM and RWKV_HEADS % RWKV_PACK == 0
    nc = seq // chunk
    pw = RWKV_PACK * HEAD_DIM
    groups = RWKV_HEADS // RWKV_PACK
    nb = RWKV_BATCH_BLOCK
    fwd = pl.BlockSpec((nb, chunk, w), lambda b, c: (b, c, 0))
    bwd = pl.BlockSpec((nb, chunk, w), lambda b, c: (b, nc - 1 - c, 0))
    fwd_dir = pl.BlockSpec((None, nb, chunk, w), lambda b, c: (0, b, c, 0))
    bwd_dir = pl.BlockSpec((None, nb, chunk, w), lambda b, c: (1, b, nc - 1 - c, 0))
    return pl.pallas_call(
        _rwkv_scan_kernel,
        name="rwkv_scan",
        grid=(batch // nb, nc),
        in_specs=[fwd, bwd, fwd, bwd] + [fwd_dir, bwd_dir] * 4,
        out_specs=[fwd, bwd],
        out_shape=[jax.ShapeDtypeStruct((batch, seq, w), F32)] * 2,
        scratch_shapes=[pltpu.VMEM((nb, 2, groups, pw, pw), F32)],
        compiler_params=_params("parallel", "arbitrary"),
    )(r, r, v, v, kk, kk, bb, bb, kd, kd, lw, lw)


def _rwkv_post_kernel(yf_ref, yb_ref, bonus_ref, g_ref, lng_ref, lnb_ref, gsum_ref, o_ref):
    gsum = gsum_ref[...]
    y = yf_ref[...] + yb_ref[...]
    mean = _head_sum(y, gsum) * (1.0 / HEAD_DIM)
    yc = y - mean
    var = _head_sum(yc * yc, gsum) * (1.0 / HEAD_DIM)
    yn = yc * lax.rsqrt(var + RWKV_GN_EPS) * lng_ref[...] + lnb_ref[...]
    o_ref[...] = ((yn + bonus_ref[...]) * g_ref[...]).astype(o_ref.dtype)


def _rwkv_post(y_fwd, y_bwd, bonus, g, ln_g, ln_b, gsum):
    batch, seq, w = y_fwd.shape
    tt = min(512, seq)
    shared = pl.BlockSpec((None, tt, w), lambda b, i: (b, i, 0))
    row = pl.BlockSpec((1, w), lambda b, i: (0, 0))
    return pl.pallas_call(
        _rwkv_post_kernel,
        name="rwkv_post",
        grid=(batch, seq // tt),
        in_specs=[shared, shared, shared, shared, row, row,
                  pl.BlockSpec(gsum.shape, lambda b, i: (0, 0))],
        out_specs=shared,
        out_shape=jax.ShapeDtypeStruct((batch, seq, w), BF16),
        compiler_params=_params("parallel", "parallel"),
    )(y_fwd, y_bwd, bonus, g, ln_g.reshape(1, w), ln_b.reshape(1, w), gsum)


def _merge_kernel(ga_ref, gb_ref, gc_ref, oa0_ref, oa1_ref, oa2_ref, la0_ref, la1_ref, la2_ref,
                  ob_ref, oc_ref, x_ref, wa_ref, wb_ref, wc_ref, wo_ref, mod_ref, g2_ref, rw_ref, rb_ref,
                  x_out, h_out, idx_out, prob_out, *token_order):
    in_token_order = []
    for src_ref, dst_ref in zip((oa1_ref, la1_ref, oa2_ref, la2_ref), token_order):
        dil, per = src_ref.shape[0], src_ref.shape[1]
        for r in range(dil):
            for cb in range(dst_ref.shape[0]):
                dst_ref[cb, pl.ds(r, per, stride=dil), :] = src_ref[r, :, cb * LANES:(cb + 1) * LANES]
        in_token_order.append(jnp.concatenate([dst_ref[cb] for cb in range(dst_ref.shape[0])], axis=-1))
    oa1, la1, oa2, la2 = in_token_order
    lses = [la0_ref[...], la1, la2]
    m = jnp.maximum(jnp.maximum(lses[0], lses[1]), lses[2])
    es = [jnp.exp(l - m) for l in lses]
    inv = 1.0 / (es[0] + es[1] + es[2])
    o_a = (es[0] * oa0_ref[...] + es[1] * oa1 + es[2] * oa2) * inv
    merged = (_sigmoid(ga_ref[...].astype(F32)) * _dot(o_a.astype(BF16), wa_ref[...])
              + _sigmoid(gb_ref[...].astype(F32)) * _dot(ob_ref[...], wb_ref[...])
              + _sigmoid(gc_ref[...].astype(F32)) * _dot(oc_ref[...], wc_ref[...]))
    x = x_ref[...] + mod_ref[2] * _dot(merged.astype(BF16), wo_ref[...])
    x_out[...] = x
    h = _modulated_norm(x, g2_ref[...], mod_ref[4], mod_ref[3])
    h_out[...] = h.astype(h_out.dtype)

    logits = _dot3(h, rw_ref[...]) + rb_ref[...]
    lane = lax.broadcasted_iota(jnp.int32, logits.shape, 1)
    work = logits
    vals, idxs = [], []
    for _ in range(TOP_K):
        top = jnp.max(work, axis=-1, keepdims=True)
        first = jnp.min(jnp.where(work == top, lane, N_EXPERTS), axis=-1, keepdims=True)
        vals.append(top)
        idxs.append(first)
        work = jnp.where(lane == first, -jnp.inf, work)
    exps = [jnp.exp(t - vals[0]) for t in vals]
    denom = exps[0] + exps[1] + exps[2] + exps[3]
    idx_out[...] = jnp.concatenate(idxs, axis=-1)
    prob_out[...] = jnp.concatenate([e / denom for e in exps], axis=-1)


def _merge(proj, swa_outs, o_b, o_c, x, wa, wb, wc, wo, mod_l, g2, router_w, router_b):
    batch, seq, d = x.shape
    tm = min(512, seq)
    gw = SWA_GROUP_WIDTH

    def rows(width, blk=0):
        return pl.BlockSpec((None, tm, width), lambda b, i: (b, i, blk))

    def full(shape):
        return pl.BlockSpec(shape, lambda b, i: (0,) * len(shape))

    def residue_rows(dil):
        return pl.BlockSpec((None, dil, tm // dil, gw), lambda b, i: (b, 0, i, 0))

    o_list = [o for o, _ in swa_outs]
    l_list = [l for _, l in swa_outs]
    swa_specs = [rows(gw), residue_rows(SWA_PATTERNS[1][1]), residue_rows(SWA_PATTERNS[2][1])]
    return pl.pallas_call(
        _merge_kernel,
        name="merge",
        grid=(batch, seq // tm),
        in_specs=[rows(d, 0), rows(d, 1), rows(d, 2)] + swa_specs * 2
        + [rows(RWKV_WIDTH), rows(DIFF_WIDTH), rows(d), full(wa.shape), full(wb.shape), full(wc.shape),
           full(wo.shape), pl.BlockSpec((None, 6, 1, d), lambda b, i: (b, 0, 0, 0)), full((1, d)),
           full(router_w.shape), full((1, N_EXPERTS))],
        out_specs=[rows(d), rows(d), rows(TOP_K), rows(TOP_K)],
        out_shape=[jax.ShapeDtypeStruct((batch, seq, d), F32), jax.ShapeDtypeStruct((batch, seq, d), BF16),
                   jax.ShapeDtypeStruct((batch, seq, TOP_K), jnp.int32),
                   jax.ShapeDtypeStruct((batch, seq, TOP_K), F32)],
        scratch_shapes=[pltpu.VMEM((gw // LANES, tm, LANES), F32)] * 4,
        compiler_params=_params("parallel", "parallel"),
    )(proj, proj, proj, *o_list, *l_list, o_b, o_c, x, wa, wb, wc, wo, mod_l, g2.reshape(1, d),
      router_w, router_b.reshape(1, N_EXPERTS))


def _expert_kernel(tile_expert_ref, tile_valid_ref, tile_first_ref, x_ref, w1_ref, b1_ref, w2_ref, b2_ref, o_ref,
                   w1_bf16_ref, w2_pairs_ref):
    t = pl.program_id(0)

    @pl.when(tile_first_ref[t] != 0)
    def _():
        w1_bf16_ref[...] = w1_ref[...].astype(BF16)
        rounded = w2_ref[...].astype(BF16).astype(F32)
        high = pltpu.bitcast(rounded, jnp.uint32) & jnp.uint32(0xFFFF0000)
        w2_pairs_ref[...] = high | (high >> 16)

    @pl.when(tile_valid_ref[t] != 0)
    def _():
        hh = _dot(x_ref[...], w1_bf16_ref[...]) + b1_ref[...]
        gated = (lax.broadcasted_iota(jnp.int32, hh.shape, 1) % 2) == 0
        glu = jnp.minimum(hh, SWIGLU_LIMIT)
        part = jnp.where(gated, glu * _sigmoid(SWIGLU_ALPHA * glu), jnp.clip(hh, -SWIGLU_LIMIT, SWIGLU_LIMIT) + 1.0)
        act = jnp.where(gated, part * pltpu.roll(part, hh.shape[1] - 1, axis=1), 0.0)
        w2_rows = pltpu.bitcast(w2_pairs_ref[...], BF16)
        o_ref[...] = (_dot(act.astype(BF16), w2_rows) + b2_ref[...]).astype(o_ref.dtype)

    @pl.when(tile_valid_ref[t] == 0)
    def _():
        o_ref[...] = jnp.zeros_like(o_ref)


def _experts(xg, tile_expert, tile_valid, tile_first, w1, b1, w2, b2, layer):
    rows, d = xg.shape
    f2 = w1.shape[3]
    tm = MOE_ROW_TILE
    grid_spec = pltpu.PrefetchScalarGridSpec(
        num_scalar_prefetch=3,
        grid=(rows // tm,),
        in_specs=[
            pl.BlockSpec((tm, d), lambda t, te, tv, tf: (t, 0)),
            pl.BlockSpec((None, None, d, f2), lambda t, te, tv, tf: (layer, te[t], 0, 0)),
            pl.BlockSpec((None, None, 1, f2), lambda t, te, tv, tf: (layer, te[t], 0, 0)),
            pl.BlockSpec((None, None, f2 // 2, d), lambda t, te, tv, tf: (layer, te[t], 0, 0)),
            pl.BlockSpec((None, None, 1, d), lambda t, te, tv, tf: (layer, te[t], 0, 0)),
        ],
        out_specs=pl.BlockSpec((tm, d), lambda t, te, tv, tf: (t, 0)),
        scratch_shapes=[pltpu.VMEM((d, f2), BF16), pltpu.VMEM((f2 // 2, d), jnp.uint32)],
    )
    return pl.pallas_call(
        _expert_kernel,
        name="moe_experts",
        grid_spec=grid_spec,
        out_shape=jax.ShapeDtypeStruct((rows, d), BF16),
        compiler_params=pltpu.CompilerParams(dimension_semantics=("arbitrary",), vmem_limit_bytes=EXPERT_VMEM_LIMIT),
    )(tile_expert, tile_valid, tile_first, xg, w1, b1, w2, b2)


def _combine_kernel(y_ref, p_ref, x_ref, mod_ref, g_ref, o_ref, *, final):
    p = p_ref[...]
    acc = p[:, 0:1] * y_ref[0].astype(F32)
    for j in range(1, TOP_K):
        acc = acc + p[:, j:j + 1] * y_ref[j].astype(F32)
    x = x_ref[...] + mod_ref[5] * acc
    if final:
        x = x * lax.rsqrt(jnp.mean(x * x, axis=-1, keepdims=True) + NORM_EPS) * g_ref[...]
    o_ref[...] = x


def _combine(y4, probs, x, mod_l, final_g, final):
    batch, seq, d = x.shape
    tm = min(512, seq)
    return pl.pallas_call(
        functools.partial(_combine_kernel, final=final),
        name="moe_combine",
        grid=(batch, seq // tm),
        in_specs=[
            pl.BlockSpec((TOP_K, None, tm, d), lambda b, i: (0, b, i, 0)),
            pl.BlockSpec((None, tm, TOP_K), lambda b, i: (b, i, 0)),
            pl.BlockSpec((None, tm, d), lambda b, i: (b, i, 0)),
            pl.BlockSpec((None, 6, 1, d), lambda b, i: (b, 0, 0, 0)),
            pl.BlockSpec((1, d), lambda b, i: (0, 0)),
        ],
        out_specs=pl.BlockSpec((None, tm, d), lambda b, i: (b, i, 0)),
        out_shape=jax.ShapeDtypeStruct((batch, seq, d), F32),
        compiler_params=_params("parallel", "parallel"),
    )(y4, probs, x, mod_l, final_g.reshape(1, d))


def _dispatch_plan(idx):
    tm = MOE_ROW_TILE
    flat = idx.reshape(-1)
    n_slots = flat.shape[0]
    experts = jnp.arange(N_EXPERTS, dtype=jnp.int32)
    counts = jnp.sum((flat[:, None] == experts[None, :]).astype(jnp.int32), axis=0)
    padded = ((counts + tm - 1) // tm) * tm
    padded_end = jnp.cumsum(padded)
    spare_used = jnp.arange(tm, dtype=jnp.int32)[None, :] < (padded - counts)[:, None]
    spare_key = jnp.where(spare_used, 2 * experts[:, None] + 1, 2 * N_EXPERTS).reshape(-1)
    order = jnp.argsort(jnp.concatenate([2 * flat, spare_key]), stable=True).astype(jnp.int32)
    src_token = jnp.where(order < n_slots, order // TOP_K, 0)
    slot_row = jnp.argsort(order).astype(jnp.int32)[:n_slots]
    tile_start = jnp.arange(order.shape[0] // tm, dtype=jnp.int32) * tm
    tile_expert = jnp.minimum(jnp.sum((padded_end[None, :] <= tile_start[:, None]).astype(jnp.int32), axis=1),
                              N_EXPERTS - 1)
    tile_valid = (tile_start < padded_end[-1]).astype(jnp.int32)
    previous = jnp.concatenate([jnp.full((1,), -1, jnp.int32), tile_expert[:-1]])
    tile_first = tile_valid * (tile_expert != previous).astype(jnp.int32)
    return src_token, slot_row, tile_expert, tile_valid, tile_first


def _moe(h2, idx, probs, x, mod_l, w1, b1, w2, b2, layer, final_g, final):
    batch, seq, d = x.shape
    n_tok = batch * seq
    src_token, slot_row, tile_expert, tile_valid, tile_first = _dispatch_plan(idx)
    xg = jnp.take(h2.reshape(n_tok, d), src_token, axis=0)
    y = _experts(xg, tile_expert, tile_valid, tile_first, w1, b1, w2, b2, layer)
    rows_by_slot = slot_row.reshape(n_tok, TOP_K).T.reshape(-1)
    y4 = jnp.take(y, rows_by_slot, axis=0).reshape(TOP_K, batch, seq, d)
    return _combine(y4, probs, x, mod_l, final_g, final)


def _t5_bucket(rel):
    nb = REL_BUCKETS // 2
    max_exact = nb // 2
    ret = jnp.where(rel > 0, nb, 0)
    n = jnp.abs(rel)
    nf = jnp.maximum(n, 1).astype(F32)
    large = max_exact + (jnp.log(nf / max_exact) / math.log(REL_MAX_DIST / max_exact)
                         * (nb - max_exact)).astype(jnp.int32)
    large = jnp.minimum(large, nb - 1)
    return ret + jnp.where(n < max_exact, n, large)


def _swa_bias_tiles(rel_bias):
    half, qb = SWA_HALF, SWA_QUERY_BLOCK
    a = jnp.arange(qb)[:, None]
    c = jnp.arange(qb + 2 * half)[None, :]
    j = c - half - a
    tiles = []
    for g, (_, dil) in enumerate(SWA_PATTERNS):
        offs = jnp.arange(-half, half + 1) * dil
        table = rel_bias[_t5_bucket(offs)][:, g * SWA_GROUP_HEADS:(g + 1) * SWA_GROUP_HEADS].T
        tile = table[:, jnp.clip(j + half, 0, 2 * half)]
        tiles.append(jnp.where((jnp.abs(j) <= half)[None], tile, NEG_INF).astype(F32))
    return tiles


def _diff_bias_segments(rel_bias, seq):
    tq = min(DIFF_Q_TILE, seq)
    table = rel_bias[:, len(SWA_PATTERNS) * SWA_GROUP_HEADS:].T
    by_rel = table[:, _t5_bucket(jnp.arange(2 * seq) - seq)]
    segs = [by_rel[:, seq - (i + 1) * tq:2 * seq - i * tq] for i in range(seq // tq)]
    return jnp.stack(segs, axis=1)[:, :, None, :].astype(F32)


def _pack_w_in(w_in_l):
    a_w = 3 * SWA_WIDTH
    b_w = 3 * RWKV_WIDTH + RWKV_LORA
    c_w = 3 * DIFF_WIDTH
    a, b, c, gates = (w_in_l[:, :a_w], w_in_l[:, a_w:a_w + b_w], w_in_l[:, a_w + b_w:a_w + b_w + c_w],
                      w_in_l[:, a_w + b_w + c_w:])
    gw = SWA_GROUP_WIDTH

    def group(g):
        return [a[:, t * SWA_WIDTH + g * gw:t * SWA_WIDTH + (g + 1) * gw] for t in range(3)]

    pad = jnp.zeros((w_in_l.shape[0], PROJ_COLS_PADDED - PROJ_COLS), w_in_l.dtype)
    return jnp.concatenate([gates] + group(1) + group(2) + group(0) + [c, b, pad], axis=1).astype(BF16)


def _direction_padded(w_up):
    z = jnp.zeros_like(w_up[0])
    return jnp.stack([jnp.concatenate([w_up[0], z], axis=0), jnp.concatenate([z, w_up[1]], axis=0)])


def kernel(x, c, w_mod, b_mod, norm1_g, norm2_g, w_in, rwkv_mu, rwkv_w0, rwkv_w_up, rwkv_a0, rwkv_a_up, rwkv_g_up, rwkv_k_k, rwkv_k_a, rwkv_r_k, rwkv_ln_g, rwkv_ln_b, diff_lambda, diff_subln_g, rel_bias, w_branch_a, w_branch_b, w_branch_c, w_out, router_w, router_b, moe_w1, moe_b1, moe_w2, moe_b2, final_norm_g):
    batch, seq, d = x.shape
    depth = w_mod.shape[0]
    mod = _modulation(c, w_mod, b_mod)
    swa_tiles = _swa_bias_tiles(rel_bias)
    bias_segs = _diff_bias_segments(rel_bias, seq)
    head_of = jnp.arange(RWKV_WIDTH) // HEAD_DIM
    gsum = (head_of[:, None] == head_of[None, :]).astype(BF16)
    b1 = moe_b1[:, :, None, :]
    b2 = moe_b2[:, :, None, :]

    for l in range(depth):
        mod_l = mod[l]
        proj, res1, res2 = _norm_proj(x, norm1_g[l], mod_l, _pack_w_in(w_in[l]))
        swa_outs = [_swa_group(proj, swa_tiles[0], 1, COL_A0 // SWA_GROUP_WIDTH),
                    _swa_group(res1, swa_tiles[1], SWA_PATTERNS[1][1], 0),
                    _swa_group(res2, swa_tiles[2], SWA_PATTERNS[2][1], 0)]
        r, v, g, bonus, kk, bb, kd, lw = _rwkv_prep(
            proj, rwkv_mu[l], rwkv_w0[l], _direction_padded(rwkv_w_up[l]), rwkv_a0[l],
            _direction_padded(rwkv_a_up[l]), rwkv_g_up[l], rwkv_k_k[l], rwkv_k_a[l], rwkv_r_k[l], gsum)
        y_fwd, y_bwd = _rwkv_scan(r, v, kk, bb, kd, lw)
        o_b = _rwkv_post(y_fwd, y_bwd, bonus, g, rwkv_ln_g[l], rwkv_ln_b[l], gsum)
        lambda_init = 0.8 - 0.6 * math.exp(-0.3 * l)
        o_c = _diff_attention(proj, bias_segs, diff_lambda[l], diff_subln_g[l], lambda_init)
        x, h2, idx, probs = _merge(
            proj, swa_outs, o_b, o_c, x, w_branch_a[l].astype(BF16), w_branch_b[l].astype(BF16),
            w_branch_c[l].astype(BF16), w_out[l].astype(BF16), mod_l, norm2_g[l], router_w[l], router_b[l])
        x = _moe(h2, idx, probs, x, mod_l, moe_w1, b1, moe_w2, b2, l, final_norm_g, l == depth - 1)
    return x
```

```python
import functools
import math

import jax
import jax.numpy as jnp
from jax import lax
from jax.experimental import pallas as pl
from jax.experimental.pallas import tpu as pltpu

F32 = jnp.float32
BF16 = jnp.bfloat16
HIGHEST = lax.Precision.HIGHEST

D_MODEL = 1024
HEAD_DIM = 64
LANES = 128
NORM_EPS = 1e-6
NEG_INF = -1e30
LOG2_E = 1.4426950408889634

SWA_PATTERNS = ((128, 1), (512, 4), (2048, 16))
SWA_GROUP_HEADS = 4
SWA_GROUP_WIDTH = SWA_GROUP_HEADS * HEAD_DIM
SWA_WIDTH = len(SWA_PATTERNS) * SWA_GROUP_WIDTH
SWA_HALF = 64
SWA_QUERY_BLOCK = 128

RWKV_HEADS = 12
RWKV_WIDTH = RWKV_HEADS * HEAD_DIM
RWKV_LORA = 384
RWKV_GN_EPS = 64e-5
RWKV_CHUNK = 64
RWKV_BATCH_BLOCK = 2
RWKV_PACK = 4

DIFF_HEADS = 6
DIFF_WIDTH = DIFF_HEADS * 2 * HEAD_DIM
DIFF_SUBLN_EPS = 1e-5
DIFF_Q_TILE = 512

REL_BUCKETS = 32
REL_MAX_DIST = 128

N_EXPERTS = 32
TOP_K = 4
D_EXPERT = 1024
SWIGLU_LIMIT = 7.0
SWIGLU_ALPHA = 1.702
MOE_ROW_TILE = 256

COL_GATES = 0
COL_A_DILATED = 3 * D_MODEL
COL_A0 = COL_A_DILATED + 2 * SWA_WIDTH
COL_C = COL_A0 + SWA_WIDTH
COL_R = COL_C + 3 * DIFF_WIDTH
COL_LORA = COL_R + 3 * RWKV_WIDTH
PROJ_COLS = COL_LORA + RWKV_LORA
PROJ_COLS_PADDED = 10752
PROJ_COL_TILE = 1536
STRIDED_COL_TILE = COL_A_DILATED // PROJ_COL_TILE
PROJ_ROW_TILE = 1024

VMEM_LIMIT = 48 * 1024 * 1024
EXPERT_VMEM_LIMIT = 56 * 1024 * 1024

_NT = ((1,), (1,))
_TN = ((0,), (0,))


def _params(*sem):
    return pltpu.CompilerParams(dimension_semantics=sem, vmem_limit_bytes=VMEM_LIMIT)


def _sigmoid(x):
    return 1.0 / (1.0 + jnp.exp(-x))


def _dot(a, b, dims=((1,), (0,)), precision=None):
    return lax.dot_general(a, b, (dims, ((), ())), precision=precision, preferred_element_type=F32)


def _hi_lo(x):
    hi = x.astype(BF16)
    return hi, (x - hi.astype(F32)).astype(BF16)


def _dot3(a, b):
    a_hi, a_lo = _hi_lo(a)
    b_hi, b_lo = _hi_lo(b)
    return _dot(a_hi, b_hi) + _dot(a_hi, b_lo) + _dot(a_lo, b_hi)


def _head_sum(x, ones_bf16):
    hi, lo = _hi_lo(x)
    return _dot(hi, ones_bf16) + _dot(lo, ones_bf16)


def _mod_kernel(c_ref, w_ref, b_ref, o_ref):
    c = c_ref[...]
    cond = c * _sigmoid(c)
    o_ref[...] = _dot(cond, w_ref[...], precision=HIGHEST) + b_ref[...]


def _modulation(c, w_mod, b_mod):
    n_layers, d, n = w_mod.shape
    batch = c.shape[0]
    tn = 1536
    out = pl.pallas_call(
        _mod_kernel,
        name="modulation",
        grid=(n_layers, n // tn),
        in_specs=[
            pl.BlockSpec((batch, d), lambda l, j: (0, 0)),
            pl.BlockSpec((None, d, tn), lambda l, j: (l, 0, j)),
            pl.BlockSpec((None, 1, tn), lambda l, j: (l, 0, j)),
        ],
        out_specs=pl.BlockSpec((None, batch, tn), lambda l, j: (l, 0, j)),
        out_shape=jax.ShapeDtypeStruct((n_layers, batch, n), F32),
        compiler_params=_params("parallel", "parallel"),
    )(c, w_mod, b_mod.reshape(n_layers, 1, n))
    return out.reshape(n_layers, batch, 6, 1, d)


def _modulated_norm(x, g, scale, shift):
    y = x * lax.rsqrt(jnp.mean(x * x, axis=-1, keepdims=True) + NORM_EPS) * g
    return y * (1.0 + scale) + shift


def _norm_proj_kernel(x_ref, g_ref, mod_ref, w_ref, o_ref, res4_ref, res16_ref, h_ref, acc_ref):
    j = pl.program_id(2)

    @pl.when(j == 0)
    def _():
        h_ref[...] = _modulated_norm(x_ref[...], g_ref[...], mod_ref[1], mod_ref[0]).astype(BF16)

    @pl.when(j != STRIDED_COL_TILE)
    def _():
        o_ref[...] = _dot(h_ref[...], w_ref[...]).astype(o_ref.dtype)

    @pl.when(j == STRIDED_COL_TILE)
    def _():
        acc = _dot(h_ref[...], w_ref[...])
        o_ref[...] = acc.astype(o_ref.dtype)
        lanes = acc_ref.shape[2]
        tm = acc_ref.shape[1]
        for cb in range(acc_ref.shape[0]):
            acc_ref[cb] = acc[:, cb * lanes:(cb + 1) * lanes]
        per_group = SWA_WIDTH // lanes
        for res_ref, dil, first in ((res4_ref, SWA_PATTERNS[1][1], 0), (res16_ref, SWA_PATTERNS[2][1], per_group)):
            for r in range(dil):
                for cb in range(per_group):
                    res_ref[r, :, cb * lanes:(cb + 1) * lanes] = (
                        acc_ref[first + cb, pl.ds(r, tm // dil, stride=dil), :].astype(res_ref.dtype))


def _norm_proj(x, g, mod_l, w):
    batch, seq, d = x.shape
    n = w.shape[1]
    tm, tn = min(PROJ_ROW_TILE, seq), PROJ_COL_TILE
    d1, d2 = SWA_PATTERNS[1][1], SWA_PATTERNS[2][1]

    def res_spec(dil):
        return pl.BlockSpec((None, dil, tm // dil, SWA_WIDTH), lambda b, i, j: (b, 0, i, 0))

    return pl.pallas_call(
        _norm_proj_kernel,
        name="norm_proj",
        grid=(batch, seq // tm, n // tn),
        in_specs=[
            pl.BlockSpec((None, tm, d), lambda b, i, j: (b, i, 0)),
            pl.BlockSpec((1, d), lambda b, i, j: (0, 0)),
            pl.BlockSpec((None, 6, 1, d), lambda b, i, j: (b, 0, 0, 0)),
            pl.BlockSpec((d, tn), lambda b, i, j: (0, j)),
        ],
        out_specs=[pl.BlockSpec((None, tm, tn), lambda b, i, j: (b, i, j)), res_spec(d1), res_spec(d2)],
        out_shape=[jax.ShapeDtypeStruct((batch, seq, n), BF16),
                   jax.ShapeDtypeStruct((batch, d1, seq // d1, SWA_WIDTH), BF16),
                   jax.ShapeDtypeStruct((batch, d2, seq // d2, SWA_WIDTH), BF16)],
        scratch_shapes=[pltpu.VMEM((tm, d), BF16), pltpu.VMEM((tn // LANES, tm, LANES), F32)],
        compiler_params=_params("parallel", "parallel", "arbitrary"),
    )(x, g.reshape(1, d), mod_l, w)


def _swa_kernel(q_ref, k_ref, v_ref, bias_ref, o_ref, lse_ref, kpad_ref, vpad_ref, *, length):
    half, qb, hd = SWA_HALF, SWA_QUERY_BLOCK, HEAD_DIM
    win = qb + 2 * half
    zeros = jnp.zeros((half, SWA_GROUP_WIDTH), BF16)
    for pad_ref, src_ref in ((kpad_ref, k_ref), (vpad_ref, v_ref)):
        pad_ref[0:half, :] = zeros
        pad_ref[half + length:half + length + half, :] = zeros
        pad_ref[half:half + length, :] = src_ref[...]

    def block(n, carry):
        q0 = pl.multiple_of(n * qb, qb)
        q = q_ref[pl.ds(q0, qb), :]
        kw = kpad_ref[pl.ds(q0, win), :]
        vw = vpad_ref[pl.ds(q0, win), :]
        key_pos = q0 - half + lax.broadcasted_iota(jnp.int32, (1, win), 1)
        in_range = (key_pos >= 0) & (key_pos < length)
        heads = [slice(h * hd, (h + 1) * hd) for h in range(SWA_GROUP_HEADS)]
        scores = [jnp.where(in_range, _dot(q[:, sl], kw[:, sl], _NT) * (hd ** -0.5) + bias_ref[h], NEG_INF)
                  for h, sl in enumerate(heads)]
        maxes = [jnp.max(s, axis=-1, keepdims=True) for s in scores]
        probs = [jnp.exp(s - m) for s, m in zip(scores, maxes)]
        sums = [jnp.sum(p, axis=-1, keepdims=True) for p in probs]
        outs = [_dot(p.astype(BF16), vw[:, sl]) / l for p, sl, l in zip(probs, heads, sums)]
        lses = [jnp.broadcast_to(m + jnp.log(l), (qb, hd)) for m, l in zip(maxes, sums)]
        o_ref[pl.ds(q0, qb), :] = jnp.concatenate(outs, axis=-1)
        lse_ref[pl.ds(q0, qb), :] = jnp.concatenate(lses, axis=-1)
        return carry

    lax.fori_loop(0, length // qb, block, 0)


def _swa_group(qkv, bias_tile, dilation, col_block):
    gw = SWA_GROUP_WIDTH
    if dilation == 1:
        batch, length, _ = qkv.shape
        grid = (batch, 1)
        block = (None, length, gw)
        out_dims = (batch, length, gw)

        def at(col):
            return lambda b, r: (b, 0, col)
    else:
        batch, _, length, _ = qkv.shape
        grid = (batch, dilation)
        block = (None, None, length, gw)
        out_dims = (batch, dilation, length, gw)

        def at(col):
            return lambda b, r: (b, r, 0, col)

    out_spec = pl.BlockSpec(block, at(0))
    out_shape = jax.ShapeDtypeStruct(out_dims, F32)
    return pl.pallas_call(
        functools.partial(_swa_kernel, length=length),
        name="swa",
        grid=grid,
        in_specs=[pl.BlockSpec(block, at(col_block + which)) for which in range(3)]
        + [pl.BlockSpec(bias_tile.shape, lambda b, r: (0, 0, 0))],
        out_specs=[out_spec, out_spec],
        out_shape=[out_shape, out_shape],
        scratch_shapes=[pltpu.VMEM((length + 2 * SWA_HALF, gw), BF16)] * 2,
        compiler_params=_params("parallel", "parallel"),
    )(qkv, qkv, qkv, bias_tile)


def _diff_kernel(q_ref, k_ref, v_ref, seg_ref, dl_ref, g_ref, o_ref, bias_ref, *, lambda_init):
    hd = HEAD_DIM
    tq, seq = bias_ref.shape

    @pl.when(pl.program_id(2) == 0)
    def _():
        seg = jnp.broadcast_to(seg_ref[...] * LOG2_E, (tq, seq + tq))
        bias_ref[...] = pltpu.roll(seg, seq, 1, stride=1, stride_axis=0)[:, :seq]

    dl = dl_ref[...]
    lam = (jnp.exp(jnp.sum(dl[0:1] * dl[1:2], axis=-1, keepdims=True))
           - jnp.exp(jnp.sum(dl[2:3] * dl[3:4], axis=-1, keepdims=True)) + lambda_init)
    k = k_ref[...]
    v = v_ref[...]
    q = (q_ref[...].astype(F32) * (hd ** -0.5 * LOG2_E)).astype(BF16)
    chains = [(slice(r0, r0 + 128), slice(comp * hd, (comp + 1) * hd)) for r0 in range(0, tq, 128) for comp in range(2)]
    scores = [_dot(q[rows, sl], k[:, sl], _NT) + bias_ref[rows, :] for rows, sl in chains]
    probs = [jnp.exp2(s - jnp.max(s, axis=-1, keepdims=True)) for s in scores]
    outs = [_dot(p.astype(BF16), v) * (1.0 / jnp.sum(p, axis=-1, keepdims=True)) for p in probs]
    for i in range(0, len(chains), 2):
        o = outs[i] - lam * outs[i + 1]
        o = o * lax.rsqrt(jnp.mean(o * o, axis=-1, keepdims=True) + DIFF_SUBLN_EPS) * g_ref[...]
        o_ref[chains[i][0], :] = (o * (1.0 - lambda_init)).astype(o_ref.dtype)


def _diff_attention(proj, bias_segs, diff_lambda_l, subln_g, lambda_init):
    batch, seq, _ = proj.shape
    hw = 2 * HEAD_DIM
    tq = min(DIFF_Q_TILE, seq)
    base = COL_C // hw
    return pl.pallas_call(
        functools.partial(_diff_kernel, lambda_init=lambda_init),
        name="diff_attn",
        grid=(DIFF_HEADS, seq // tq, batch),
        in_specs=[
            pl.BlockSpec((None, tq, hw), lambda h, i, b: (b, i, base + h)),
            pl.BlockSpec((None, seq, hw), lambda h, i, b: (b, 0, base + DIFF_HEADS + h)),
            pl.BlockSpec((None, seq, hw), lambda h, i, b: (b, 0, base + 2 * DIFF_HEADS + h)),
            pl.BlockSpec((None, None, 1, seq + tq), lambda h, i, b: (h, i, 0, 0)),
            pl.BlockSpec((4, HEAD_DIM), lambda h, i, b: (0, 0)),
            pl.BlockSpec((1, hw), lambda h, i, b: (0, 0)),
        ],
        out_specs=pl.BlockSpec((None, tq, hw), lambda h, i, b: (b, i, h)),
        out_shape=jax.ShapeDtypeStruct((batch, seq, DIFF_WIDTH), BF16),
        scratch_shapes=[pltpu.VMEM((tq, seq), F32)],
        compiler_params=_params("parallel", "parallel", "arbitrary"),
    )(proj, proj, proj, bias_segs, diff_lambda_l, subln_g.reshape(1, hw))


def _shifted(cur_ref, prev_ref, next_ref, mu, first, last):
    x = cur_ref[...].astype(F32)
    rows = x.shape[0]
    halo = prev_ref.shape[0]
    before = jnp.where(first, 0.0, prev_ref[halo - 1:halo, :].astype(F32))
    after = jnp.where(last, 0.0, next_ref[0:1, :].astype(F32))
    row = lax.broadcasted_iota(jnp.int32, (rows, 1), 0)
    prev = jnp.where(row == 0, before, pltpu.roll(x, 1, axis=0))
    nxt = jnp.where(row == rows - 1, after, pltpu.roll(x, rows - 1, axis=0))
    return x + mu[0:1] * (prev - x) + mu[1:2] * (nxt - x)


def _rwkv_prep_kernel(r_ref, rp_ref, rn_ref, k_ref, kp_ref, kn_ref, v_ref, vp_ref, vn_ref,
                      lo_ref, lop_ref, lon_ref, mu_ref, w0_ref, wup_ref, a0_ref, aup_ref, gup_ref,
                      kk_scale_ref, ka_ref, rk_ref, gsum_ref,
                      r_out, v_out, g_out, bonus_out, kk_out, bb_out, kd_out, lw_out):
    w = RWKV_WIDTH
    first = pl.program_id(1) == 0
    last = pl.program_id(1) == pl.num_programs(1) - 1
    mu = mu_ref[...]
    r = _shifted(r_ref, rp_ref, rn_ref, mu[:, 0:w], first, last)
    k = _shifted(k_ref, kp_ref, kn_ref, mu[:, w:2 * w], first, last)
    v = _shifted(v_ref, vp_ref, vn_ref, mu[:, 2 * w:3 * w], first, last)
    lora = _shifted(lo_ref, lop_ref, lon_ref, mu[:, 3 * w:3 * w + RWKV_LORA], first, last)
    decay_in = jnp.tanh(lora[:, 0:128])
    iclr_in = lora[:, 128:256]
    gsum = gsum_ref[...]
    r_out[...] = r
    v_out[...] = v
    g_out[...] = _dot3(_sigmoid(lora[:, 256:384]), gup_ref[...])
    bonus = jnp.zeros_like(r)
    for di in range(2):
        z = w0_ref[di:di + 1, :] + _dot3(decay_in, wup_ref[di])
        u = -z
        softplus = jnp.maximum(u, 0.0) + jnp.log(1.0 + jnp.exp(-jnp.abs(u)))
        lw_out[di] = -jnp.exp(-softplus - 0.5)
        a = _sigmoid(a0_ref[di:di + 1, :] + _dot3(iclr_in, aup_ref[di]))
        kk = k * kk_scale_ref[di:di + 1, :]
        kk = kk * lax.rsqrt(jnp.maximum(_head_sum(kk * kk, gsum), 1e-24))
        kd = k * (1.0 + (a - 1.0) * ka_ref[di:di + 1, :])
        kk_out[di] = kk
        bb_out[di] = kk * a
        kd_out[di] = kd
        bonus = bonus + _head_sum(r * kd * rk_ref[...], gsum) * v
    bonus_out[...] = bonus


def _rwkv_prep(proj, mu, w0, wup2, a0, aup2, g_up, k_k, k_a, r_k, gsum):
    batch, seq, _ = proj.shape
    w = RWKV_WIDTH
    tt = min(256, seq)
    halo = 16
    hb = tt // halo
    last_halo = seq // halo - 1

    def cur(width, blk):
        return pl.BlockSpec((None, tt, width), lambda b, i: (b, i, blk))

    def prev(width, blk):
        return pl.BlockSpec((None, halo, width), lambda b, i: (b, jnp.maximum(i * hb - 1, 0), blk))

    def nxt(width, blk):
        return pl.BlockSpec((None, halo, width), lambda b, i: (b, jnp.minimum((i + 1) * hb, last_halo), blk))

    def full(shape):
        return pl.BlockSpec(shape, lambda b, i: (0,) * len(shape))

    in_specs = []
    for blk in (COL_R // w, COL_R // w + 1, COL_R // w + 2):
        in_specs += [cur(w, blk), prev(w, blk), nxt(w, blk)]
    lb = COL_LORA // RWKV_LORA
    in_specs += [cur(RWKV_LORA, lb), prev(RWKV_LORA, lb), nxt(RWKV_LORA, lb)]
    in_specs += [full(mu.shape), full(w0.shape), full(wup2.shape), full(a0.shape), full(aup2.shape),
                 full(g_up.shape), full(k_k.shape), full(k_a.shape), full((1, w)), full(gsum.shape)]
    shared = pl.BlockSpec((None, tt, w), lambda b, i: (b, i, 0))
    per_dir = pl.BlockSpec((2, None, tt, w), lambda b, i: (0, b, i, 0))
    shared_shape = jax.ShapeDtypeStruct((batch, seq, w), F32)
    per_dir_shape = jax.ShapeDtypeStruct((2, batch, seq, w), F32)
    return pl.pallas_call(
        _rwkv_prep_kernel,
        name="rwkv_prep",
        grid=(batch, seq // tt),
        in_specs=in_specs,
        out_specs=[shared] * 4 + [per_dir] * 4,
        out_shape=[shared_shape] * 4 + [per_dir_shape] * 4,
        compiler_params=_params("parallel", "parallel"),
    )(*([proj] * 12), mu, w0, wup2, a0, aup2, g_up, k_k, k_a, r_k.reshape(1, w), gsum)


def _rwkv_scan_kernel(rf_ref, rb_ref, vf_ref, vb_ref, kkf_ref, kkb_ref, bbf_ref, bbb_ref, kdf_ref, kdb_ref,
                      lwf_ref, lwb_ref, yf_ref, yb_ref, state_ref):
    chunk, hd, pack = RWKV_CHUNK, HEAD_DIM, RWKV_PACK
    pw = pack * hd
    groups = RWKV_HEADS // pack

    @pl.when(pl.program_id(1) == 0)
    def _():
        state_ref[...] = jnp.zeros_like(state_ref)

    ti = lax.broadcasted_iota(jnp.int32, (chunk, pw), 0)
    tj = lax.broadcasted_iota(jnp.int32, (chunk, pw), 1) % chunk
    eye = (ti == tj).astype(F32)
    same16 = (ti // 16) == (tj // 16)
    same32 = (ti // 32) == (tj // 32)
    diag = (lax.broadcasted_iota(jnp.int32, (pw, pw), 0) // hd) == (lax.broadcasted_iota(jnp.int32, (pw, pw), 1) // hd)
    diag_bf16 = diag.astype(BF16)

    def block_diag(x):
        return jnp.concatenate([x.astype(BF16)] * pack, axis=0) * diag_bf16

    def stacked(top, bottom):
        return jnp.concatenate([top, bottom], axis=0).astype(BF16)

    chains = []
    per_direction = ((rf_ref, vf_ref, kkf_ref, bbf_ref, kdf_ref, lwf_ref), (rb_ref, vb_ref, kkb_ref, bbb_ref, kdb_ref, lwb_ref))
    for n, (d, refs) in ((n, dr) for n in range(rf_ref.shape[0]) for dr in enumerate(per_direction)):
        r_ref, v_ref, kk_ref, bb_ref, kd_ref, lw_ref = (ref.at[n] for ref in refs)
        lag = ti - tj if d == 0 else tj - ti
        before, upto = lag > 0, lag >= 0
        lw = lw_ref[...]
        tri = upto[:, :chunk].astype(BF16)
        lw_hi = lw.astype(BF16)
        lw_rest = lw - lw_hi.astype(F32)
        lw_mid = lw_rest.astype(BF16)
        lw_lo = (lw_rest - lw_mid.astype(F32)).astype(BF16)
        cum = _dot(tri, lw_hi) + _dot(tri, lw_mid) + _dot(tri, lw_lo)
        total = jnp.sum(lw, axis=0, keepdims=True)
        p_inv = jnp.exp(-cum)
        p_rest = jnp.exp(total - cum)
        p_total = jnp.exp(total)
        kk, bb, kd = kk_ref[...], bb_ref[...], kd_ref[...]
        a_all = -kk * jnp.exp(cum - lw)
        b_all = bb * p_inv
        k_all = kd * p_inv
        r_all = r_ref[...] * jnp.exp(cum)
        b_end = bb * p_rest
        k_end = kd * p_rest
        v_all = v_ref[...]
        for g in range(groups):
            sl = slice(g * pw, (g + 1) * pw)
            chains.append(dict(
                n=n, d=d, g=g, before=before, upto=upto, ar=stacked(a_all[:, sl], r_all[:, sl]), b=b_all[:, sl],
                k=k_all[:, sl], v=v_all[:, sl], ends=stacked(b_end[:, sl], k_end[:, sl]), p_total=p_total[:, sl]))

    for c in chains:
        c["by_b"] = _dot(c["ar"], block_diag(c["b"]), _NT)
        c["by_k"] = _dot(c["ar"], block_diag(c["k"]), _NT)
    for c in chains:
        c["m_ab"] = jnp.where(c["before"], c["by_b"][:chunk], 0.0)
        c["m_rb"] = jnp.where(c["upto"], c["by_b"][chunk:], 0.0)
        c["m_kv"] = stacked(jnp.where(c["before"], c["by_k"][:chunk], 0.0), jnp.where(c["upto"], c["by_k"][chunk:], 0.0))
        c["x"] = jnp.where(same16, c["m_ab"], 0.0)
        c["inv"] = eye + c["x"]
    for c in chains:
        c["x"] = _dot(c["x"].astype(BF16), block_diag(c["x"]))
    for _ in range(2):
        for c in chains:
            both = _dot(stacked(c["x"], c["inv"]), block_diag(c["x"]))
            c["x"] = both[:chunk]
            c["inv"] = c["inv"] + both[chunk:]
    for c in chains:
        c["inv"] = c["inv"] + _dot(c["inv"].astype(BF16), block_diag(c["x"]))
    for level in range(2):
        for c in chains:
            off = jnp.where(same32 & ~same16, c["m_ab"], 0.0) if level == 0 else jnp.where(~same32, c["m_ab"], 0.0)
            c["inner"] = _dot(off.astype(BF16), block_diag(c["inv"]))
        for c in chains:
            c["inv"] = c["inv"] + _dot(c["inv"].astype(BF16), block_diag(c["inner"]))
    for c in chains:
        c["state"] = state_ref[c["n"], c["d"], c["g"]]
        c["by_state"] = _dot(c["ar"], c["state"].astype(BF16), _NT)
        c["by_v"] = _dot(c["m_kv"], block_diag(c["v"]))
    for c in chains:
        c["u"] = _dot(c["inv"].astype(BF16), block_diag(c["by_state"][:chunk] + c["by_v"][:chunk]))
    for c in chains:
        c["y"] = c["by_state"][chunk:] + c["by_v"][chunk:] + _dot(c["m_rb"].astype(BF16), block_diag(c["u"]))
        update = _dot(stacked(c["u"], c["v"]), c["ends"], _TN)
        state_ref[c["n"], c["d"], c["g"]] = c["state"] * c["p_total"] + jnp.where(diag, update, 0.0)
    for n in range(rf_ref.shape[0]):
        for d, y_ref in enumerate((yf_ref, yb_ref)):
            y_ref[n] = jnp.concatenate([c["y"] for c in chains if (c["n"], c["d"]) == (n, d)], axis=-1)


def _rwkv_scan(r, v, kk, bb, kd, lw):
    batch, seq, w = r.shape
    chunk = RWKV_CHUNK
    assert chunk == HEAD_DIM and RWKV_HEADS % RWKV_PACK == 0
    nc = seq // chunk
    pw = RWKV_PACK * HEAD_DIM
    groups = RWKV_HEADS // RWKV_PACK
    nb = RWKV_BATCH_BLOCK
    fwd = pl.BlockSpec((nb, chunk, w), lambda b, c: (b, c, 0))
    bwd = pl.BlockSpec((nb, chunk, w), lambda b, c: (b, nc - 1 - c, 0))
    fwd_dir = pl.BlockSpec((None, nb, chunk, w), lambda b, c: (0, b, c, 0))
    bwd_dir = pl.BlockSpec((None, nb, chunk, w), lambda b, c: (1, b, nc - 1 - c, 0))
    return pl.pallas_call(
        _rwkv_scan_kernel,
        name="rwkv_scan",
        grid=(batch // nb, nc),
        in_specs=[fwd, bwd, fwd, bwd] + [fwd_dir, bwd_dir] * 4,
        out_specs=[fwd, bwd],
        out_shape=[jax.ShapeDtypeStruct((batch, seq, w), F32)] * 2,
        scratch_shapes=[pltpu.VMEM((nb, 2, groups, pw, pw), F32)],
        compiler_params=_params("parallel", "arbitrary"),
    )(r, r, v, v, kk, kk, bb, bb, kd, kd, lw, lw)


def _rwkv_post_kernel(yf_ref, yb_ref, bonus_ref, g_ref, lng_ref, lnb_ref, gsum_ref, o_ref):
    gsum = gsum_ref[...]
    y = yf_ref[...] + yb_ref[...]
    mean = _head_sum(y, gsum) * (1.0 / HEAD_DIM)
    yc = y - mean
    var = _head_sum(yc * yc, gsum) * (1.0 / HEAD_DIM)
    yn = yc * lax.rsqrt(var + RWKV_GN_EPS) * lng_ref[...] + lnb_ref[...]
    o_ref[...] = ((yn + bonus_ref[...]) * g_ref[...]).astype(o_ref.dtype)


def _rwkv_post(y_fwd, y_bwd, bonus, g, ln_g, ln_b, gsum):
    batch, seq, w = y_fwd.shape
    tt = min(512, seq)
    shared = pl.BlockSpec((None, tt, w), lambda b, i: (b, i, 0))
    row = pl.BlockSpec((1, w), lambda b, i: (0, 0))
    return pl.pallas_call(
        _rwkv_post_kernel,
        name="rwkv_post",
        grid=(batch, seq // tt),
        in_specs=[shared, shared, shared, shared, row, row,
                  pl.BlockSpec(gsum.shape, lambda b, i: (0, 0))],
        out_specs=shared,
        out_shape=jax.ShapeDtypeStruct((batch, seq, w), BF16),
        compiler_params=_params("parallel", "parallel"),
    )(y_fwd, y_bwd, bonus, g, ln_g.reshape(1, w), ln_b.reshape(1, w), gsum)


def _merge_kernel(ga_ref, gb_ref, gc_ref, oa0_ref, oa1_ref, oa2_ref, la0_ref, la1_ref, la2_ref,
                  ob_ref, oc_ref, x_ref, wa_ref, wb_ref, wc_ref, wo_ref, mod_ref, g2_ref, rw_ref, rb_ref,
                  x_out, h_out, idx_out, prob_out, *token_order):
    in_token_order = []
    for src_ref, dst_ref in zip((oa1_ref, la1_ref, oa2_ref, la2_ref), token_order):
        dil, per = src_ref.shape[0], src_ref.shape[1]
        for r in range(dil):
            for cb in range(dst_ref.shape[0]):
                dst_ref[cb, pl.ds(r, per, stride=dil), :] = src_ref[r, :, cb * LANES:(cb + 1) * LANES]
        in_token_order.append(jnp.concatenate([dst_ref[cb] for cb in range(dst_ref.shape[0])], axis=-1))
    oa1, la1, oa2, la2 = in_token_order
    lses = [la0_ref[...], la1, la2]
    m = jnp.maximum(jnp.maximum(lses[0], lses[1]), lses[2])
    es = [jnp.exp(l - m) for l in lses]
    inv = 1.0 / (es[0] + es[1] + es[2])
    o_a = (es[0] * oa0_ref[...] + es[1] * oa1 + es[2] * oa2) * inv
    merged = (_sigmoid(ga_ref[...].astype(F32)) * _dot(o_a.astype(BF16), wa_ref[...])
              + _sigmoid(gb_ref[...].astype(F32)) * _dot(ob_ref[...], wb_ref[...])
              + _sigmoid(gc_ref[...].astype(F32)) * _dot(oc_ref[...], wc_ref[...]))
    x = x_ref[...] + mod_ref[2] * _dot(merged.astype(BF16), wo_ref[...])
    x_out[...] = x
    h = _modulated_norm(x, g2_ref[...], mod_ref[4], mod_ref[3])
    h_out[...] = h.astype(h_out.dtype)

    logits = _dot3(h, rw_ref[...]) + rb_ref[...]
    lane = lax.broadcasted_iota(jnp.int32, logits.shape, 1)
    work = logits
    vals, idxs = [], []
    for _ in range(TOP_K):
        top = jnp.max(work, axis=-1, keepdims=True)
        first = jnp.min(jnp.where(work == top, lane, N_EXPERTS), axis=-1, keepdims=True)
        vals.append(top)
        idxs.append(first)
        work = jnp.where(lane == first, -jnp.inf, work)
    exps = [jnp.exp(t - vals[0]) for t in vals]
    denom = exps[0] + exps[1] + exps[2] + exps[3]
    idx_out[...] = jnp.concatenate(idxs, axis=-1)
    prob_out[...] = jnp.concatenate([e / denom for e in exps], axis=-1)


def _merge(proj, swa_outs, o_b, o_c, x, wa, wb, wc, wo, mod_l, g2, router_w, router_b):
    batch, seq, d = x.shape
    tm = min(512, seq)
    gw = SWA_GROUP_WIDTH

    def rows(width, blk=0):
        return pl.BlockSpec((None, tm, width), lambda b, i: (b, i, blk))

    def full(shape):
        return pl.BlockSpec(shape, lambda b, i: (0,) * len(shape))

    def residue_rows(dil):
        return pl.BlockSpec((None, dil, tm // dil, gw), lambda b, i: (b, 0, i, 0))

    o_list = [o for o, _ in swa_outs]
    l_list = [l for _, l in swa_outs]
    swa_specs = [rows(gw), residue_rows(SWA_PATTERNS[1][1]), residue_rows(SWA_PATTERNS[2][1])]
    return pl.pallas_call(
        _merge_kernel,
        name="merge",
        grid=(batch, seq // tm),
        in_specs=[rows(d, 0), rows(d, 1), rows(d, 2)] + swa_specs * 2
        + [rows(RWKV_WIDTH), rows(DIFF_WIDTH), rows(d), full(wa.shape), full(wb.shape), full(wc.shape),
           full(wo.shape), pl.BlockSpec((None, 6, 1, d), lambda b, i: (b, 0, 0, 0)), full((1, d)),
           full(router_w.shape), full((1, N_EXPERTS))],
        out_specs=[rows(d), rows(d), rows(TOP_K), rows(TOP_K)],
        out_shape=[jax.ShapeDtypeStruct((batch, seq, d), F32), jax.ShapeDtypeStruct((batch, seq, d), F32),
                   jax.ShapeDtypeStruct((batch, seq, TOP_K), jnp.int32),
                   jax.ShapeDtypeStruct((batch, seq, TOP_K), F32)],
        scratch_shapes=[pltpu.VMEM((gw // LANES, tm, LANES), F32)] * 4,
        compiler_params=_params("parallel", "parallel"),
    )(proj, proj, proj, *o_list, *l_list, o_b, o_c, x, wa, wb, wc, wo, mod_l, g2.reshape(1, d),
      router_w, router_b.reshape(1, N_EXPERTS))


def _expert_kernel(tile_expert_ref, tile_valid_ref, tile_first_ref, x_ref, w1_ref, b1_ref, w2_ref, b2_ref, o_ref,
                   w1_bf16_ref, w2_pairs_ref):
    t = pl.program_id(0)

    @pl.when(tile_first_ref[t] != 0)
    def _():
        w1_bf16_ref[...] = w1_ref[...].astype(BF16)
        rounded = w2_ref[...].astype(BF16).astype(F32)
        high = pltpu.bitcast(rounded, jnp.uint32) & jnp.uint32(0xFFFF0000)
        w2_pairs_ref[...] = high | (high >> 16)

    @pl.when(tile_valid_ref[t] != 0)
    def _():
        hh = _dot(x_ref[...].astype(BF16), w1_bf16_ref[...]) + b1_ref[...]
        gated = (lax.broadcasted_iota(jnp.int32, hh.shape, 1) % 2) == 0
        glu = jnp.minimum(hh, SWIGLU_LIMIT)
        part = jnp.where(gated, glu * _sigmoid(SWIGLU_ALPHA * glu), jnp.clip(hh, -SWIGLU_LIMIT, SWIGLU_LIMIT) + 1.0)
        act = jnp.where(gated, part * pltpu.roll(part, hh.shape[1] - 1, axis=1), 0.0)
        w2_rows = pltpu.bitcast(w2_pairs_ref[...], BF16)
        o_ref[...] = (_dot(act.astype(BF16), w2_rows) + b2_ref[...]).astype(o_ref.dtype)

    @pl.when(tile_valid_ref[t] == 0)
    def _():
        o_ref[...] = jnp.zeros_like(o_ref)


def _experts(xg, tile_expert, tile_valid, tile_first, w1, b1, w2, b2, layer):
    rows, d = xg.shape
    f2 = w1.shape[3]
    tm = MOE_ROW_TILE
    grid_spec = pltpu.PrefetchScalarGridSpec(
        num_scalar_prefetch=3,
        grid=(rows // tm,),
        in_specs=[
            pl.BlockSpec((tm, d), lambda t, te, tv, tf: (t, 0)),
            pl.BlockSpec((None, None, d, f2), lambda t, te, tv, tf: (layer, te[t], 0, 0)),
            pl.BlockSpec((None, None, 1, f2), lambda t, te, tv, tf: (layer, te[t], 0, 0)),
            pl.BlockSpec((None, None, f2 // 2, d), lambda t, te, tv, tf: (layer, te[t], 0, 0)),
            pl.BlockSpec((None, None, 1, d), lambda t, te, tv, tf: (layer, te[t], 0, 0)),
        ],
        out_specs=pl.BlockSpec((tm, d), lambda t, te, tv, tf: (t, 0)),
        scratch_shapes=[pltpu.VMEM((d, f2), BF16), pltpu.VMEM((f2 // 2, d), jnp.uint32)],
    )
    return pl.pallas_call(
        _expert_kernel,
        name="moe_experts",
        grid_spec=grid_spec,
        out_shape=jax.ShapeDtypeStruct((rows, d), BF16),
        compiler_params=pltpu.CompilerParams(dimension_semantics=("arbitrary",), vmem_limit_bytes=EXPERT_VMEM_LIMIT),
    )(tile_expert, tile_valid, tile_first, xg, w1, b1, w2, b2)


def _combine_kernel(y_ref, p_ref, x_ref, mod_ref, g_ref, o_ref, *, final):
    p = p_ref[...]
    acc = p[:, 0:1] * y_ref[0].astype(F32)
    for j in range(1, TOP_K):
        acc = acc + p[:, j:j + 1] * y_ref[j].astype(F32)
    x = x_ref[...] + mod_ref[5] * acc
    if final:
        x = x * lax.rsqrt(jnp.mean(x * x, axis=-1, keepdims=True) + NORM_EPS) * g_ref[...]
    o_ref[...] = x


def _combine(y4, probs, x, mod_l, final_g, final):
    batch, seq, d = x.shape
    tm = min(512, seq)
    return pl.pallas_call(
        functools.partial(_combine_kernel, final=final),
        name="moe_combine",
        grid=(batch, seq // tm),
        in_specs=[
            pl.BlockSpec((TOP_K, None, tm, d), lambda b, i: (0, b, i, 0)),
            pl.BlockSpec((None, tm, TOP_K), lambda b, i: (b, i, 0)),
            pl.BlockSpec((None, tm, d), lambda b, i: (b, i, 0)),
            pl.BlockSpec((None, 6, 1, d), lambda b, i: (b, 0, 0, 0)),
            pl.BlockSpec((1, d), lambda b, i: (0, 0)),
        ],
        out_specs=pl.BlockSpec((None, tm, d), lambda b, i: (b, i, 0)),
        out_shape=jax.ShapeDtypeStruct((batch, seq, d), F32),
        compiler_params=_params("parallel", "parallel"),
    )(y4, probs, x, mod_l, final_g.reshape(1, d))


def _dispatch_plan(idx):
    tm = MOE_ROW_TILE
    flat = idx.reshape(-1)
    n_slots = flat.shape[0]
    experts = jnp.arange(N_EXPERTS, dtype=jnp.int32)
    counts = jnp.sum((flat[:, None] == experts[None, :]).astype(jnp.int32), axis=0)
    padded = ((counts + tm - 1) // tm) * tm
    padded_end = jnp.cumsum(padded)
    spare_used = jnp.arange(tm, dtype=jnp.int32)[None, :] < (padded - counts)[:, None]
    spare_key = jnp.where(spare_used, 2 * experts[:, None] + 1, 2 * N_EXPERTS).reshape(-1)
    order = jnp.argsort(jnp.concatenate([2 * flat, spare_key]), stable=True).astype(jnp.int32)
    src_token = jnp.where(order < n_slots, order // TOP_K, 0)
    slot_row = jnp.argsort(order).astype(jnp.int32)[:n_slots]
    tile_start = jnp.arange(order.shape[0] // tm, dtype=jnp.int32) * tm
    tile_expert = jnp.minimum(jnp.sum((padded_end[None, :] <= tile_start[:, None]).astype(jnp.int32), axis=1),
                              N_EXPERTS - 1)
    tile_valid = (tile_start < padded_end[-1]).astype(jnp.int32)
    previous = jnp.concatenate([jnp.full((1,), -1, jnp.int32), tile_expert[:-1]])
    tile_first = tile_valid * (tile_expert != previous).astype(jnp.int32)
    return src_token, slot_row, tile_expert, tile_valid, tile_first


def _moe(h2, idx, probs, x, mod_l, w1, b1, w2, b2, layer, final_g, final):
    batch, seq, d = x.shape
    n_tok = batch * seq
    src_token, slot_row, tile_expert, tile_valid, tile_first = _dispatch_plan(idx)
    xg = h2.reshape(n_tok, d).at[src_token].get(mode="promise_in_bounds")
    y = _experts(xg, tile_expert, tile_valid, tile_first, w1, b1, w2, b2, layer)
    rows_by_slot = slot_row.reshape(n_tok, TOP_K).T.reshape(-1)
    y4 = y.at[rows_by_slot].get(mode="promise_in_bounds").reshape(TOP_K, batch, seq, d)
    return _combine(y4, probs, x, mod_l, final_g, final)


def _t5_bucket(rel):
    nb = REL_BUCKETS // 2
    max_exact = nb // 2
    ret = jnp.where(rel > 0, nb, 0)
    n = jnp.abs(rel)
    nf = jnp.maximum(n, 1).astype(F32)
    large = max_exact + (jnp.log(nf / max_exact) / math.log(REL_MAX_DIST / max_exact)
                         * (nb - max_exact)).astype(jnp.int32)
    large = jnp.minimum(large, nb - 1)
    return ret + jnp.where(n < max_exact, n, large)


def _swa_bias_tiles(rel_bias):
    half, qb = SWA_HALF, SWA_QUERY_BLOCK
    win = qb + 2 * half
    n = win + qb
    tiles = []
    for g, (_, dil) in enumerate(SWA_PATTERNS):
        offs = jnp.arange(-half, half + 1) * dil
        table = rel_bias[_t5_bucket(offs)][:, g * SWA_GROUP_HEADS:(g + 1) * SWA_GROUP_HEADS].T
        by_lag = jnp.concatenate(
            [table, jnp.full((SWA_GROUP_HEADS, n - table.shape[1]), NEG_INF, table.dtype)], axis=1)
        rows = jnp.broadcast_to(by_lag[:, None, :], (SWA_GROUP_HEADS, qb, n)).reshape(SWA_GROUP_HEADS, qb * n)
        tiles.append(rows[:, :qb * (n - 1)].reshape(SWA_GROUP_HEADS, qb, n - 1)[:, :, :win].astype(F32))
    return tiles


def _diff_bias_segments(rel_bias, seq):
    tq = min(DIFF_Q_TILE, seq)
    table = rel_bias[:, len(SWA_PATTERNS) * SWA_GROUP_HEADS:].T
    by_rel = table[:, _t5_bucket(jnp.arange(2 * seq) - seq)]
    segs = [by_rel[:, seq - (i + 1) * tq:2 * seq - i * tq] for i in range(seq // tq)]
    return jnp.stack(segs, axis=1)[:, :, None, :].astype(F32)


def _pack_w_in(w_in_l):
    a_w = 3 * SWA_WIDTH
    b_w = 3 * RWKV_WIDTH + RWKV_LORA
    c_w = 3 * DIFF_WIDTH
    a, b, c, gates = (w_in_l[:, :a_w], w_in_l[:, a_w:a_w + b_w], w_in_l[:, a_w + b_w:a_w + b_w + c_w],
                      w_in_l[:, a_w + b_w + c_w:])
    gw = SWA_GROUP_WIDTH

    def group(g):
        return [a[:, t * SWA_WIDTH + g * gw:t * SWA_WIDTH + (g + 1) * gw] for t in range(3)]

    pad = jnp.zeros((w_in_l.shape[0], PROJ_COLS_PADDED - PROJ_COLS), w_in_l.dtype)
    return jnp.concatenate([gates] + group(1) + group(2) + group(0) + [c, b, pad], axis=1).astype(BF16)


def _direction_padded(w_up):
    z = jnp.zeros_like(w_up[0])
    return jnp.stack([jnp.concatenate([w_up[0], z], axis=0), jnp.concatenate([z, w_up[1]], axis=0)])


def kernel(x, c, w_mod, b_mod, norm1_g, norm2_g, w_in, rwkv_mu, rwkv_w0, rwkv_w_up, rwkv_a0, rwkv_a_up, rwkv_g_up, rwkv_k_k, rwkv_k_a, rwkv_r_k, rwkv_ln_g, rwkv_ln_b, diff_lambda, diff_subln_g, rel_bias, w_branch_a, w_branch_b, w_branch_c, w_out, router_w, router_b, moe_w1, moe_b1, moe_w2, moe_b2, final_norm_g):
    batch, seq, d = x.shape
    depth = w_mod.shape[0]
    mod = _modulation(c, w_mod, b_mod)
    swa_tiles = _swa_bias_tiles(rel_bias)
    bias_segs = _diff_bias_segments(rel_bias, seq)
    head_of = jnp.arange(RWKV_WIDTH) // HEAD_DIM
    gsum = (head_of[:, None] == head_of[None, :]).astype(BF16)
    b1 = moe_b1[:, :, None, :]
    b2 = moe_b2[:, :, None, :]

    for l in range(depth):
        mod_l = mod[l]
        proj, res1, res2 = _norm_proj(x, norm1_g[l], mod_l, _pack_w_in(w_in[l]))
        swa_outs = [_swa_group(proj, swa_tiles[0], 1, COL_A0 // SWA_GROUP_WIDTH),
                    _swa_group(res1, swa_tiles[1], SWA_PATTERNS[1][1], 0),
                    _swa_group(res2, swa_tiles[2], SWA_PATTERNS[2][1], 0)]
        r, v, g, bonus, kk, bb, kd, lw = _rwkv_prep(
            proj, rwkv_mu[l], rwkv_w0[l], _direction_padded(rwkv_w_up[l]), rwkv_a0[l],
            _direction_padded(rwkv_a_up[l]), rwkv_g_up[l], rwkv_k_k[l], rwkv_k_a[l], rwkv_r_k[l], gsum)
        y_fwd, y_bwd = _rwkv_scan(r, v, kk, bb, kd, lw)
        o_b = _rwkv_post(y_fwd, y_bwd, bonus, g, rwkv_ln_g[l], rwkv_ln_b[l], gsum)
        lambda_init = 0.8 - 0.6 * math.exp(-0.3 * l)
        o_c = _diff_attention(proj, bias_segs, diff_lambda[l], diff_subln_g[l], lambda_init)
        x, h2, idx, probs = _merge(
            proj, swa_outs, o_b, o_c, x, w_branch_a[l].astype(BF16), w_branch_b[l].astype(BF16),
            w_branch_c[l].astype(BF16), w_out[l].astype(BF16), mod_l, norm2_g[l], router_w[l], router_b[l])
        x = _moe(h2, idx, probs, x, mod_l, moe_w1, b1, moe_w2, b2, l, final_norm_g, l == depth - 1)
    return x
```

```python
import functools
import math

import jax
import jax.numpy as jnp
from jax import lax
from jax.experimental import pallas as pl
from jax.experimental.pallas import tpu as pltpu

F32 = jnp.float32
BF16 = jnp.bfloat16
HIGHEST = lax.Precision.HIGHEST

D_MODEL = 1024
HEAD_DIM = 64
LANES = 128
NORM_EPS = 1e-6
NEG_INF = -1e30
LOG2_E = 1.4426950408889634

SWA_PATTERNS = ((128, 1), (512, 4), (2048, 16))
SWA_GROUP_HEADS = 4
SWA_GROUP_WIDTH = SWA_GROUP_HEADS * HEAD_DIM
SWA_WIDTH = len(SWA_PATTERNS) * SWA_GROUP_WIDTH
SWA_HALF = 64
SWA_QUERY_BLOCK = 128

RWKV_HEADS = 12
RWKV_WIDTH = RWKV_HEADS * HEAD_DIM
RWKV_LORA = 384
RWKV_GN_EPS = 64e-5
RWKV_CHUNK = 64
RWKV_BATCH_BLOCK = 2
RWKV_PACK = 4

DIFF_HEADS = 6
DIFF_WIDTH = DIFF_HEADS * 2 * HEAD_DIM
DIFF_SUBLN_EPS = 1e-5
DIFF_Q_TILE = 512

REL_BUCKETS = 32
REL_MAX_DIST = 128

N_EXPERTS = 32
TOP_K = 4
D_EXPERT = 1024
SWIGLU_LIMIT = 7.0
SWIGLU_ALPHA = 1.702
MOE_ROW_TILE = 256
SWIGLU_BLOCK = 256
MOE_ROW_CHUNKS = 4

COL_GATES = 0
COL_A_DILATED = 3 * D_MODEL
COL_A0 = COL_A_DILATED + 2 * SWA_WIDTH
COL_C = COL_A0 + SWA_WIDTH
COL_R = COL_C + 3 * DIFF_WIDTH
COL_LORA = COL_R + 3 * RWKV_WIDTH
PROJ_COLS = COL_LORA + RWKV_LORA
PROJ_COLS_PADDED = 10752
PROJ_COL_TILE = 1536
STRIDED_COL_TILE = COL_A_DILATED // PROJ_COL_TILE
PROJ_ROW_TILE = 1024

VMEM_LIMIT = 48 * 1024 * 1024
EXPERT_VMEM_LIMIT = 56 * 1024 * 1024

_NT = ((1,), (1,))
_TN = ((0,), (0,))


def _params(*sem):
    return pltpu.CompilerParams(dimension_semantics=sem, vmem_limit_bytes=VMEM_LIMIT)


def _sigmoid(x):
    return 1.0 / (1.0 + jnp.exp(-x))


def _dot(a, b, dims=((1,), (0,)), precision=None):
    return lax.dot_general(a, b, (dims, ((), ())), precision=precision, preferred_element_type=F32)


def _hi_lo(x):
    hi = x.astype(BF16)
    return hi, (x - hi.astype(F32)).astype(BF16)


def _dot3(a, b):
    a_hi, a_lo = _hi_lo(a)
    b_hi, b_lo = _hi_lo(b)
    return _dot(a_hi, b_hi) + _dot(a_hi, b_lo) + _dot(a_lo, b_hi)


def _head_sum(x, ones_bf16):
    hi, lo = _hi_lo(x)
    return _dot(hi, ones_bf16) + _dot(lo, ones_bf16)


def _mod_kernel(c_ref, w_ref, b_ref, o_ref):
    c = c_ref[...]
    cond = c * _sigmoid(c)
    o_ref[...] = _dot(cond, w_ref[...], precision=HIGHEST) + b_ref[...]


def _modulation(c, w_mod, b_mod):
    n_layers, d, n = w_mod.shape
    batch = c.shape[0]
    tn = 1536
    out = pl.pallas_call(
        _mod_kernel,
        name="modulation",
        grid=(n_layers, n // tn),
        in_specs=[
            pl.BlockSpec((batch, d), lambda l, j: (0, 0)),
            pl.BlockSpec((None, d, tn), lambda l, j: (l, 0, j)),
            pl.BlockSpec((None, 1, tn), lambda l, j: (l, 0, j)),
        ],
        out_specs=pl.BlockSpec((None, batch, tn), lambda l, j: (l, 0, j)),
        out_shape=jax.ShapeDtypeStruct((n_layers, batch, n), F32),
        compiler_params=_params("parallel", "parallel"),
    )(c, w_mod, b_mod.reshape(n_layers, 1, n))
    return out.reshape(n_layers, batch, 6, 1, d)


def _modulated_norm(x, g, scale, shift):
    y = x * lax.rsqrt(jnp.mean(x * x, axis=-1, keepdims=True) + NORM_EPS) * g
    return y * (1.0 + scale) + shift


def _norm_proj_kernel(x_ref, g_ref, mod_ref, w_ref, o_ref, res4_ref, res16_ref, h_ref, acc_ref):
    j = pl.program_id(2)

    @pl.when(j == 0)
    def _():
        h_ref[...] = _modulated_norm(x_ref[...], g_ref[...], mod_ref[1], mod_ref[0]).astype(BF16)

    @pl.when(j != STRIDED_COL_TILE)
    def _():
        o_ref[...] = _dot(h_ref[...], w_ref[...]).astype(o_ref.dtype)

    @pl.when(j == STRIDED_COL_TILE)
    def _():
        acc = _dot(h_ref[...], w_ref[...])
        o_ref[...] = acc.astype(o_ref.dtype)
        lanes = acc_ref.shape[2]
        tm = acc_ref.shape[1]
        for cb in range(acc_ref.shape[0]):
            acc_ref[cb] = acc[:, cb * lanes:(cb + 1) * lanes]
        per_group = SWA_WIDTH // lanes
        for res_ref, dil, first in ((res4_ref, SWA_PATTERNS[1][1], 0), (res16_ref, SWA_PATTERNS[2][1], per_group)):
            for r in range(dil):
                for cb in range(per_group):
                    res_ref[r, :, cb * lanes:(cb + 1) * lanes] = (
                        acc_ref[first + cb, pl.ds(r, tm // dil, stride=dil), :].astype(res_ref.dtype))


def _norm_proj(x, g, mod_l, w):
    batch, seq, d = x.shape
    n = w.shape[1]
    tm, tn = min(PROJ_ROW_TILE, seq), PROJ_COL_TILE
    d1, d2 = SWA_PATTERNS[1][1], SWA_PATTERNS[2][1]

    def res_spec(dil):
        return pl.BlockSpec((None, dil, tm // dil, SWA_WIDTH), lambda b, i, j: (b, 0, i, 0))

    return pl.pallas_call(
        _norm_proj_kernel,
        name="norm_proj",
        grid=(batch, seq // tm, n // tn),
        in_specs=[
            pl.BlockSpec((None, tm, d), lambda b, i, j: (b, i, 0)),
            pl.BlockSpec((1, d), lambda b, i, j: (0, 0)),
            pl.BlockSpec((None, 6, 1, d), lambda b, i, j: (b, 0, 0, 0)),
            pl.BlockSpec((d, tn), lambda b, i, j: (0, j)),
        ],
        out_specs=[pl.BlockSpec((None, tm, tn), lambda b, i, j: (b, i, j)), res_spec(d1), res_spec(d2)],
        out_shape=[jax.ShapeDtypeStruct((batch, seq, n), BF16),
                   jax.ShapeDtypeStruct((batch, d1, seq // d1, SWA_WIDTH), BF16),
                   jax.ShapeDtypeStruct((batch, d2, seq // d2, SWA_WIDTH), BF16)],
        scratch_shapes=[pltpu.VMEM((tm, d), BF16), pltpu.VMEM((tn // LANES, tm, LANES), F32)],
        compiler_params=_params("parallel", "parallel", "arbitrary"),
    )(x, g.reshape(1, d), mod_l, w)


def _swa_kernel(q_ref, k_ref, v_ref, bias_ref, o_ref, lse_ref, kpad_ref, vpad_ref, *, length):
    half, qb, hd = SWA_HALF, SWA_QUERY_BLOCK, HEAD_DIM
    win = qb + 2 * half
    zeros = jnp.zeros((half, SWA_GROUP_WIDTH), BF16)
    for pad_ref, src_ref in ((kpad_ref, k_ref), (vpad_ref, v_ref)):
        pad_ref[0:half, :] = zeros
        pad_ref[half + length:half + length + half, :] = zeros
        pad_ref[half:half + length, :] = src_ref[...]

    def block(n, carry):
        q0 = pl.multiple_of(n * qb, qb)
        q = q_ref[pl.ds(q0, qb), :]
        kw = kpad_ref[pl.ds(q0, win), :]
        vw = vpad_ref[pl.ds(q0, win), :]
        key_pos = q0 - half + lax.broadcasted_iota(jnp.int32, (1, win), 1)
        in_range = (key_pos >= 0) & (key_pos < length)
        heads = [slice(h * hd, (h + 1) * hd) for h in range(SWA_GROUP_HEADS)]
        scores = [jnp.where(in_range, _dot(q[:, sl], kw[:, sl], _NT) * (hd ** -0.5) + bias_ref[h], NEG_INF)
                  for h, sl in enumerate(heads)]
        maxes = [jnp.max(s, axis=-1, keepdims=True) for s in scores]
        probs = [jnp.exp(s - m) for s, m in zip(scores, maxes)]
        sums = [jnp.sum(p, axis=-1, keepdims=True) for p in probs]
        outs = [_dot(p.astype(BF16), vw[:, sl]) / l for p, sl, l in zip(probs, heads, sums)]
        lses = [jnp.broadcast_to(m + jnp.log(l), (qb, hd)) for m, l in zip(maxes, sums)]
        o_ref[pl.ds(q0, qb), :] = jnp.concatenate(outs, axis=-1)
        lse_ref[pl.ds(q0, qb), :] = jnp.concatenate(lses, axis=-1)
        return carry

    lax.fori_loop(0, length // qb, block, 0)


def _swa_group(qkv, bias_tile, dilation, col_block):
    gw = SWA_GROUP_WIDTH
    if dilation == 1:
        batch, length, _ = qkv.shape
        grid = (batch, 1)
        block = (None, length, gw)
        out_dims = (batch, length, gw)

        def at(col):
            return lambda b, r: (b, 0, col)
    else:
        batch, _, length, _ = qkv.shape
        grid = (batch, dilation)
        block = (None, None, length, gw)
        out_dims = (batch, dilation, length, gw)

        def at(col):
            return lambda b, r: (b, r, 0, col)

    out_spec = pl.BlockSpec(block, at(0))
    out_shape = jax.ShapeDtypeStruct(out_dims, F32)
    return pl.pallas_call(
        functools.partial(_swa_kernel, length=length),
        name="swa",
        grid=grid,
        in_specs=[pl.BlockSpec(block, at(col_block + which)) for which in range(3)]
        + [pl.BlockSpec(bias_tile.shape, lambda b, r: (0, 0, 0))],
        out_specs=[out_spec, out_spec],
        out_shape=[out_shape, out_shape],
        scratch_shapes=[pltpu.VMEM((length + 2 * SWA_HALF, gw), BF16)] * 2,
        compiler_params=_params("parallel", "parallel"),
    )(qkv, qkv, qkv, bias_tile)


def _diff_kernel(q_ref, k_ref, v_ref, seg_ref, dl_ref, g_ref, o_ref, bias_ref, *, lambda_init):
    hd = HEAD_DIM
    tq, seq = bias_ref.shape

    @pl.when(pl.program_id(2) == 0)
    def _():
        seg = jnp.broadcast_to(seg_ref[...] * LOG2_E, (tq, seq + tq))
        bias_ref[...] = pltpu.roll(seg, seq, 1, stride=1, stride_axis=0)[:, :seq]

    dl = dl_ref[...]
    lam = (jnp.exp(jnp.sum(dl[0:1] * dl[1:2], axis=-1, keepdims=True))
           - jnp.exp(jnp.sum(dl[2:3] * dl[3:4], axis=-1, keepdims=True)) + lambda_init)
    k = k_ref[...]
    v = v_ref[...]
    q = (q_ref[...].astype(F32) * (hd ** -0.5 * LOG2_E)).astype(BF16)
    chains = [(slice(r0, r0 + 128), slice(comp * hd, (comp + 1) * hd)) for r0 in range(0, tq, 128) for comp in range(2)]
    scores = [_dot(q[rows, sl], k[:, sl], _NT) + bias_ref[rows, :] for rows, sl in chains]
    probs = [jnp.exp2(s - jnp.max(s, axis=-1, keepdims=True)) for s in scores]
    outs = [_dot(p.astype(BF16), v) * (1.0 / jnp.sum(p, axis=-1, keepdims=True)) for p in probs]
    for i in range(0, len(chains), 2):
        o = outs[i] - lam * outs[i + 1]
        o = o * lax.rsqrt(jnp.mean(o * o, axis=-1, keepdims=True) + DIFF_SUBLN_EPS) * g_ref[...]
        o_ref[chains[i][0], :] = (o * (1.0 - lambda_init)).astype(o_ref.dtype)


def _diff_attention(proj, bias_segs, diff_lambda_l, subln_g, lambda_init):
    batch, seq, _ = proj.shape
    hw = 2 * HEAD_DIM
    tq = min(DIFF_Q_TILE, seq)
    base = COL_C // hw
    return pl.pallas_call(
        functools.partial(_diff_kernel, lambda_init=lambda_init),
        name="diff_attn",
        grid=(DIFF_HEADS, seq // tq, batch),
        in_specs=[
            pl.BlockSpec((None, tq, hw), lambda h, i, b: (b, i, base + h)),
            pl.BlockSpec((None, seq, hw), lambda h, i, b: (b, 0, base + DIFF_HEADS + h)),
            pl.BlockSpec((None, seq, hw), lambda h, i, b: (b, 0, base + 2 * DIFF_HEADS + h)),
            pl.BlockSpec((None, None, 1, seq + tq), lambda h, i, b: (h, i, 0, 0)),
            pl.BlockSpec((4, HEAD_DIM), lambda h, i, b: (0, 0)),
            pl.BlockSpec((1, hw), lambda h, i, b: (0, 0)),
        ],
        out_specs=pl.BlockSpec((None, tq, hw), lambda h, i, b: (b, i, h)),
        out_shape=jax.ShapeDtypeStruct((batch, seq, DIFF_WIDTH), BF16),
        scratch_shapes=[pltpu.VMEM((tq, seq), F32)],
        compiler_params=_params("parallel", "parallel", "arbitrary"),
    )(proj, proj, proj, bias_segs, diff_lambda_l, subln_g.reshape(1, hw))


def _shifted(cur_ref, prev_ref, next_ref, mu, first, last):
    x = cur_ref[...].astype(F32)
    rows = x.shape[0]
    halo = prev_ref.shape[0]
    before = jnp.where(first, 0.0, prev_ref[halo - 1:halo, :].astype(F32))
    after = jnp.where(last, 0.0, next_ref[0:1, :].astype(F32))
    row = lax.broadcasted_iota(jnp.int32, (rows, 1), 0)
    prev = jnp.where(row == 0, before, pltpu.roll(x, 1, axis=0))
    nxt = jnp.where(row == rows - 1, after, pltpu.roll(x, rows - 1, axis=0))
    return x + mu[0:1] * (prev - x) + mu[1:2] * (nxt - x)


def _rwkv_prep_kernel(r_ref, rp_ref, rn_ref, k_ref, kp_ref, kn_ref, v_ref, vp_ref, vn_ref,
                      lo_ref, lop_ref, lon_ref, mu_ref, w0_ref, wup_ref, a0_ref, aup_ref, gup_ref,
                      kk_scale_ref, ka_ref, rk_ref, gsum_ref,
                      r_out, v_out, g_out, bonus_out, kk_out, bb_out, kd_out, lw_out):
    w = RWKV_WIDTH
    first = pl.program_id(1) == 0
    last = pl.program_id(1) == pl.num_programs(1) - 1
    mu = mu_ref[...]
    r = _shifted(r_ref, rp_ref, rn_ref, mu[:, 0:w], first, last)
    k = _shifted(k_ref, kp_ref, kn_ref, mu[:, w:2 * w], first, last)
    v = _shifted(v_ref, vp_ref, vn_ref, mu[:, 2 * w:3 * w], first, last)
    lora = _shifted(lo_ref, lop_ref, lon_ref, mu[:, 3 * w:3 * w + RWKV_LORA], first, last)
    decay_in = jnp.tanh(lora[:, 0:128])
    iclr_in = lora[:, 128:256]
    gsum = gsum_ref[...]
    r_out[...] = r
    v_out[...] = v
    g_out[...] = _dot3(_sigmoid(lora[:, 256:384]), gup_ref[...])
    bonus = jnp.zeros_like(r)
    for di in range(2):
        z = w0_ref[di:di + 1, :] + _dot3(decay_in, wup_ref[di])
        u = -z
        softplus = jnp.maximum(u, 0.0) + jnp.log(1.0 + jnp.exp(-jnp.abs(u)))
        lw_out[di] = -jnp.exp(-softplus - 0.5)
        a = _sigmoid(a0_ref[di:di + 1, :] + _dot3(iclr_in, aup_ref[di]))
        kk = k * kk_scale_ref[di:di + 1, :]
        kk = kk * lax.rsqrt(jnp.maximum(_head_sum(kk * kk, gsum), 1e-24))
        kd = k * (1.0 + (a - 1.0) * ka_ref[di:di + 1, :])
        kk_out[di] = kk
        bb_out[di] = kk * a
        kd_out[di] = kd
        bonus = bonus + _head_sum(r * kd * rk_ref[...], gsum) * v
    bonus_out[...] = bonus


def _rwkv_prep(proj, mu, w0, wup2, a0, aup2, g_up, k_k, k_a, r_k, gsum):
    batch, seq, _ = proj.shape
    w = RWKV_WIDTH
    tt = min(256, seq)
    halo = 16
    hb = tt // halo
    last_halo = seq // halo - 1

    def cur(width, blk):
        return pl.BlockSpec((None, tt, width), lambda b, i: (b, i, blk))

    def prev(width, blk):
        return pl.BlockSpec((None, halo, width), lambda b, i: (b, jnp.maximum(i * hb - 1, 0), blk))

    def nxt(width, blk):
        return pl.BlockSpec((None, halo, width), lambda b, i: (b, jnp.minimum((i + 1) * hb, last_halo), blk))

    def full(shape):
        return pl.BlockSpec(shape, lambda b, i: (0,) * len(shape))

    in_specs = []
    for blk in (COL_R // w, COL_R // w + 1, COL_R // w + 2):
        in_specs += [cur(w, blk), prev(w, blk), nxt(w, blk)]
    lb = COL_LORA // RWKV_LORA
    in_specs += [cur(RWKV_LORA, lb), prev(RWKV_LORA, lb), nxt(RWKV_LORA, lb)]
    in_specs += [full(mu.shape), full(w0.shape), full(wup2.shape), full(a0.shape), full(aup2.shape),
                 full(g_up.shape), full(k_k.shape), full(k_a.shape), full((1, w)), full(gsum.shape)]
    shared = pl.BlockSpec((None, tt, w), lambda b, i: (b, i, 0))
    per_dir = pl.BlockSpec((2, None, tt, w), lambda b, i: (0, b, i, 0))
    shared_shape = jax.ShapeDtypeStruct((batch, seq, w), F32)
    per_dir_shape = jax.ShapeDtypeStruct((2, batch, seq, w), F32)
    return pl.pallas_call(
        _rwkv_prep_kernel,
        name="rwkv_prep",
        grid=(batch, seq // tt),
        in_specs=in_specs,
        out_specs=[shared] * 4 + [per_dir] * 4,
        out_shape=[shared_shape] * 4 + [per_dir_shape] * 4,
        compiler_params=_params("parallel", "parallel"),
    )(*([proj] * 12), mu, w0, wup2, a0, aup2, g_up, k_k, k_a, r_k.reshape(1, w), gsum)


def _rwkv_scan_kernel(rf_ref, rb_ref, vf_ref, vb_ref, kkf_ref, kkb_ref, bbf_ref, bbb_ref, kdf_ref, kdb_ref,
                      lwf_ref, lwb_ref, yf_ref, yb_ref, state_ref):
    chunk, hd, pack = RWKV_CHUNK, HEAD_DIM, RWKV_PACK
    pw = pack * hd
    groups = RWKV_HEADS // pack

    @pl.when(pl.program_id(1) == 0)
    def _():
        state_ref[...] = jnp.zeros_like(state_ref)

    ti = lax.broadcasted_iota(jnp.int32, (chunk, pw), 0)
    tj = lax.broadcasted_iota(jnp.int32, (chunk, pw), 1) % chunk
    eye = (ti == tj).astype(F32)
    same16 = (ti // 16) == (tj // 16)
    same32 = (ti // 32) == (tj // 32)
    diag = (lax.broadcasted_iota(jnp.int32, (pw, pw), 0) // hd) == (lax.broadcasted_iota(jnp.int32, (pw, pw), 1) // hd)
    diag_bf16 = diag.astype(BF16)

    def block_diag(x):
        return jnp.concatenate([x.astype(BF16)] * pack, axis=0) * diag_bf16

    def stacked(top, bottom):
        return jnp.concatenate([top, bottom], axis=0).astype(BF16)

    chains = []
    per_direction = ((rf_ref, vf_ref, kkf_ref, bbf_ref, kdf_ref, lwf_ref), (rb_ref, vb_ref, kkb_ref, bbb_ref, kdb_ref, lwb_ref))
    for n, (d, refs) in ((n, dr) for n in range(rf_ref.shape[0]) for dr in enumerate(per_direction)):
        r_ref, v_ref, kk_ref, bb_ref, kd_ref, lw_ref = (ref.at[n] for ref in refs)
        lag = ti - tj if d == 0 else tj - ti
        before, upto = lag > 0, lag >= 0
        lw = lw_ref[...]
        tri = upto[:, :chunk].astype(BF16)
        lw_hi = lw.astype(BF16)
        lw_rest = lw - lw_hi.astype(F32)
        lw_mid = lw_rest.astype(BF16)
        lw_lo = (lw_rest - lw_mid.astype(F32)).astype(BF16)
        cum = _dot(tri, lw_hi) + _dot(tri, lw_mid) + _dot(tri, lw_lo)
        total = jnp.sum(lw, axis=0, keepdims=True)
        p_inv = jnp.exp(-cum)
        p_rest = jnp.exp(total - cum)
        p_total = jnp.exp(total)
        kk, bb, kd = kk_ref[...], bb_ref[...], kd_ref[...]
        a_all = -kk * jnp.exp(cum - lw)
        b_all = bb * p_inv
        k_all = kd * p_inv
        r_all = r_ref[...] * jnp.exp(cum)
        b_end = bb * p_rest
        k_end = kd * p_rest
        v_all = v_ref[...]
        for g in range(groups):
            sl = slice(g * pw, (g + 1) * pw)
            chains.append(dict(
                n=n, d=d, g=g, before=before, upto=upto, ar=stacked(a_all[:, sl], r_all[:, sl]), b=b_all[:, sl],
                k=k_all[:, sl], v=v_all[:, sl], ends=stacked(b_end[:, sl], k_end[:, sl]), p_total=p_total[:, sl]))

    for c in chains:
        c["by_b"] = _dot(c["ar"], block_diag(c["b"]), _NT)
        c["by_k"] = _dot(c["ar"], block_diag(c["k"]), _NT)
    for c in chains:
        c["m_ab"] = jnp.where(c["before"], c["by_b"][:chunk], 0.0)
        c["m_rb"] = jnp.where(c["upto"], c["by_b"][chunk:], 0.0)
        c["m_kv"] = stacked(jnp.where(c["before"], c["by_k"][:chunk], 0.0), jnp.where(c["upto"], c["by_k"][chunk:], 0.0))
        c["x"] = jnp.where(same16, c["m_ab"], 0.0)
        c["inv"] = eye + c["x"]
    for c in chains:
        c["x"] = _dot(c["x"].astype(BF16), block_diag(c["x"]))
    for _ in range(2):
        for c in chains:
            both = _dot(stacked(c["x"], c["inv"]), block_diag(c["x"]))
            c["x"] = both[:chunk]
            c["inv"] = c["inv"] + both[chunk:]
    for c in chains:
        c["inv"] = c["inv"] + _dot(c["inv"].astype(BF16), block_diag(c["x"]))
    for level in range(2):
        for c in chains:
            off = jnp.where(same32 & ~same16, c["m_ab"], 0.0) if level == 0 else jnp.where(~same32, c["m_ab"], 0.0)
            c["inner"] = _dot(off.astype(BF16), block_diag(c["inv"]))
        for c in chains:
            c["inv"] = c["inv"] + _dot(c["inv"].astype(BF16), block_diag(c["inner"]))
    for c in chains:
        c["state"] = state_ref[c["n"], c["d"], c["g"]]
        c["by_state"] = _dot(c["ar"], c["state"].astype(BF16), _NT)
        c["by_v"] = _dot(c["m_kv"], block_diag(c["v"]))
    for c in chains:
        c["u"] = _dot(c["inv"].astype(BF16), block_diag(c["by_state"][:chunk] + c["by_v"][:chunk]))
    for c in chains:
        c["y"] = c["by_state"][chunk:] + c["by_v"][chunk:] + _dot(c["m_rb"].astype(BF16), block_diag(c["u"]))
        update = _dot(stacked(c["u"], c["v"]), c["ends"], _TN)
        state_ref[c["n"], c["d"], c["g"]] = c["state"] * c["p_total"] + jnp.where(diag, update, 0.0)
    for n in range(rf_ref.shape[0]):
        for d, y_ref in enumerate((yf_ref, yb_ref)):
            y_ref[n] = jnp.concatenate([c["y"] for c in chains if (c["n"], c["d"]) == (n, d)], axis=-1)


def _rwkv_scan(r, v, kk, bb, kd, lw):
    batch, seq, w = r.shape
    chunk = RWKV_CHUNK
    assert chunk == HEAD_DIM and RWKV_HEADS % RWKV_PACK == 0
    nc = seq // chunk
    pw = RWKV_PACK * HEAD_DIM
    groups = RWKV_HEADS // RWKV_PACK
    nb = RWKV_BATCH_BLOCK
    fwd = pl.BlockSpec((nb, chunk, w), lambda b, c: (b, c, 0))
    bwd = pl.BlockSpec((nb, chunk, w), lambda b, c: (b, nc - 1 - c, 0))
    fwd_dir = pl.BlockSpec((None, nb, chunk, w), lambda b, c: (0, b, c, 0))
    bwd_dir = pl.BlockSpec((None, nb, chunk, w), lambda b, c: (1, b, nc - 1 - c, 0))
    return pl.pallas_call(
        _rwkv_scan_kernel,
        name="rwkv_scan",
        grid=(batch // nb, nc),
        in_specs=[fwd, bwd, fwd, bwd] + [fwd_dir, bwd_dir] * 4,
        out_specs=[fwd, bwd],
        out_shape=[jax.ShapeDtypeStruct((batch, seq, w), F32)] * 2,
        scratch_shapes=[pltpu.VMEM((nb, 2, groups, pw, pw), F32)],
        compiler_params=_params("parallel", "arbitrary"),
    )(r, r, v, v, kk, kk, bb, bb, kd, kd, lw, lw)


def _rwkv_post_kernel(yf_ref, yb_ref, bonus_ref, g_ref, lng_ref, lnb_ref, gsum_ref, o_ref):
    gsum = gsum_ref[...]
    y = yf_ref[...] + yb_ref[...]
    mean = _head_sum(y, gsum) * (1.0 / HEAD_DIM)
    yc = y - mean
    var = _head_sum(yc * yc, gsum) * (1.0 / HEAD_DIM)
    yn = yc * lax.rsqrt(var + RWKV_GN_EPS) * lng_ref[...] + lnb_ref[...]
    o_ref[...] = ((yn + bonus_ref[...]) * g_ref[...]).astype(o_ref.dtype)


def _rwkv_post(y_fwd, y_bwd, bonus, g, ln_g, ln_b, gsum):
    batch, seq, w = y_fwd.shape
    tt = min(512, seq)
    shared = pl.BlockSpec((None, tt, w), lambda b, i: (b, i, 0))
    row = pl.BlockSpec((1, w), lambda b, i: (0, 0))
    return pl.pallas_call(
        _rwkv_post_kernel,
        name="rwkv_post",
        grid=(batch, seq // tt),
        in_specs=[shared, shared, shared, shared, row, row,
                  pl.BlockSpec(gsum.shape, lambda b, i: (0, 0))],
        out_specs=shared,
        out_shape=jax.ShapeDtypeStruct((batch, seq, w), BF16),
        compiler_params=_params("parallel", "parallel"),
    )(y_fwd, y_bwd, bonus, g, ln_g.reshape(1, w), ln_b.reshape(1, w), gsum)


def _merge_kernel(ga_ref, gb_ref, gc_ref, oa0_ref, oa1_ref, oa2_ref, la0_ref, la1_ref, la2_ref,
                  ob_ref, oc_ref, x_ref, wa_ref, wb_ref, wc_ref, wo_ref, mod_ref, g2_ref, rw_ref, rb_ref,
                  x_out, h_out, idx_out, prob_out, *token_order):
    in_token_order = []
    for src_ref, dst_ref in zip((oa1_ref, la1_ref, oa2_ref, la2_ref), token_order):
        dil, per = src_ref.shape[0], src_ref.shape[1]
        for r in range(dil):
            for cb in range(dst_ref.shape[0]):
                dst_ref[cb, pl.ds(r, per, stride=dil), :] = src_ref[r, :, cb * LANES:(cb + 1) * LANES]
        in_token_order.append(jnp.concatenate([dst_ref[cb] for cb in range(dst_ref.shape[0])], axis=-1))
    oa1, la1, oa2, la2 = in_token_order
    lses = [la0_ref[...], la1, la2]
    m = jnp.maximum(jnp.maximum(lses[0], lses[1]), lses[2])
    es = [jnp.exp(l - m) for l in lses]
    inv = 1.0 / (es[0] + es[1] + es[2])
    o_a = (es[0] * oa0_ref[...] + es[1] * oa1 + es[2] * oa2) * inv
    merged = (_sigmoid(ga_ref[...].astype(F32)) * _dot(o_a.astype(BF16), wa_ref[...])
              + _sigmoid(gb_ref[...].astype(F32)) * _dot(ob_ref[...], wb_ref[...])
              + _sigmoid(gc_ref[...].astype(F32)) * _dot(oc_ref[...], wc_ref[...]))
    x = x_ref[...] + mod_ref[2] * _dot(merged.astype(BF16), wo_ref[...])
    x_out[...] = x
    h = _modulated_norm(x, g2_ref[...], mod_ref[4], mod_ref[3])
    h_out[...] = h.astype(h_out.dtype)

    logits = _dot3(h, rw_ref[...]) + rb_ref[...]
    lane = lax.broadcasted_iota(jnp.int32, logits.shape, 1)
    work = logits
    vals, idxs = [], []
    for _ in range(TOP_K):
        top = jnp.max(work, axis=-1, keepdims=True)
        first = jnp.min(jnp.where(work == top, lane, N_EXPERTS), axis=-1, keepdims=True)
        vals.append(top)
        idxs.append(first)
        work = jnp.where(lane == first, -jnp.inf, work)
    exps = [jnp.exp(t - vals[0]) for t in vals]
    denom = exps[0] + exps[1] + exps[2] + exps[3]
    idx_out[...] = jnp.concatenate(idxs, axis=-1)
    prob_out[...] = jnp.concatenate([e / denom for e in exps], axis=-1)


def _merge(proj, swa_outs, o_b, o_c, x, wa, wb, wc, wo, mod_l, g2, router_w, router_b):
    batch, seq, d = x.shape
    tm = min(512, seq)
    gw = SWA_GROUP_WIDTH

    def rows(width, blk=0):
        return pl.BlockSpec((None, tm, width), lambda b, i: (b, i, blk))

    def full(shape):
        return pl.BlockSpec(shape, lambda b, i: (0,) * len(shape))

    def residue_rows(dil):
        return pl.BlockSpec((None, dil, tm // dil, gw), lambda b, i: (b, 0, i, 0))

    o_list = [o for o, _ in swa_outs]
    l_list = [l for _, l in swa_outs]
    swa_specs = [rows(gw), residue_rows(SWA_PATTERNS[1][1]), residue_rows(SWA_PATTERNS[2][1])]
    return pl.pallas_call(
        _merge_kernel,
        name="merge",
        grid=(batch, seq // tm),
        in_specs=[rows(d, 0), rows(d, 1), rows(d, 2)] + swa_specs * 2
        + [rows(RWKV_WIDTH), rows(DIFF_WIDTH), rows(d), full(wa.shape), full(wb.shape), full(wc.shape),
           full(wo.shape), pl.BlockSpec((None, 6, 1, d), lambda b, i: (b, 0, 0, 0)), full((1, d)),
           full(router_w.shape), full((1, N_EXPERTS))],
        out_specs=[rows(d), rows(d), rows(TOP_K), rows(TOP_K)],
        out_shape=[jax.ShapeDtypeStruct((batch, seq, d), F32), jax.ShapeDtypeStruct((batch, seq, d), F32),
                   jax.ShapeDtypeStruct((batch, seq, TOP_K), jnp.int32),
                   jax.ShapeDtypeStruct((batch, seq, TOP_K), F32)],
        scratch_shapes=[pltpu.VMEM((gw // LANES, tm, LANES), F32)] * 4,
        compiler_params=_params("parallel", "parallel"),
    )(proj, proj, proj, *o_list, *l_list, o_b, o_c, x, wa, wb, wc, wo, mod_l, g2.reshape(1, d),
      router_w, router_b.reshape(1, N_EXPERTS))


def _expert_kernel(tile_expert_ref, tile_valid_ref, tile_first_ref, x_ref, w1_ref, b1_ref, w2_ref, b2_ref, y_ref,
                   o_ref, w1_split_ref, b1_split_ref, w2_bf16_ref):
    del y_ref
    t = pl.program_id(0)
    half = SWIGLU_BLOCK // 2

    @pl.when(tile_first_ref[t] != 0)
    def _():
        src = lax.broadcasted_iota(jnp.int32, (SWIGLU_BLOCK, SWIGLU_BLOCK), 0)
        dst = lax.broadcasted_iota(jnp.int32, (SWIGLU_BLOCK, SWIGLU_BLOCK), 1)
        perm = (src == jnp.where(dst < half, 2 * dst, 2 * (dst - half) + 1)).astype(BF16)
        b1 = jnp.broadcast_to(b1_ref[...], (8, b1_ref.shape[1]))
        b1_hi = b1.astype(BF16)
        b1_rest = b1 - b1_hi.astype(F32)
        b1_mid = b1_rest.astype(BF16)
        b1_lo = (b1_rest - b1_mid.astype(F32)).astype(BF16)
        for m in range(w1_ref.shape[1] // SWIGLU_BLOCK):
            cols = slice(m * SWIGLU_BLOCK, (m + 1) * SWIGLU_BLOCK)
            w1_split_ref[:, cols] = _dot(w1_ref[:, cols].astype(BF16), perm).astype(BF16)
            b1_split_ref[:, cols] = (_dot(b1_hi[:, cols], perm) + _dot(b1_mid[:, cols], perm)
                                     + _dot(b1_lo[:, cols], perm))
        w2_bf16_ref[...] = w2_ref[...].astype(BF16)

    @pl.when(tile_valid_ref[t] != 0)
    def _():
        hh = _dot(x_ref[...].astype(BF16), w1_split_ref[...]) + b1_split_ref[0:1, :]
        acts = []
        for m in range(hh.shape[1] // SWIGLU_BLOCK):
            glu = jnp.minimum(hh[:, m * SWIGLU_BLOCK:m * SWIGLU_BLOCK + half], SWIGLU_LIMIT)
            lin = jnp.clip(hh[:, m * SWIGLU_BLOCK + half:(m + 1) * SWIGLU_BLOCK], -SWIGLU_LIMIT, SWIGLU_LIMIT)
            acts.append((glu * _sigmoid(SWIGLU_ALPHA * glu) * (lin + 1.0)).astype(BF16))
        act = jnp.concatenate(acts, axis=-1)
        o_ref[...] = (_dot(act, w2_bf16_ref[...]) + b2_ref[...]).astype(o_ref.dtype)

    @pl.when(tile_valid_ref[t] == 0)
    def _():
        o_ref[...] = jnp.zeros_like(o_ref)


def _experts(xg, y, first_tile, tile_expert, tile_valid, tile_first, w1, b1, w2, b2, layer):
    rows, d = xg.shape
    f2 = w1.shape[3]
    tm = MOE_ROW_TILE
    grid_spec = pltpu.PrefetchScalarGridSpec(
        num_scalar_prefetch=3,
        grid=(rows // tm,),
        in_specs=[
            pl.BlockSpec((tm, d), lambda t, te, tv, tf: (t, 0)),
            pl.BlockSpec((None, None, d, f2), lambda t, te, tv, tf: (layer, te[t], 0, 0)),
            pl.BlockSpec((None, None, 1, f2), lambda t, te, tv, tf: (layer, te[t], 0, 0)),
            pl.BlockSpec((None, None, f2 // 2, d), lambda t, te, tv, tf: (layer, te[t], 0, 0)),
            pl.BlockSpec((None, None, 1, d), lambda t, te, tv, tf: (layer, te[t], 0, 0)),
            pl.BlockSpec(memory_space=pl.ANY),
        ],
        out_specs=pl.BlockSpec((tm, d), lambda t, te, tv, tf: (first_tile + t, 0)),
        scratch_shapes=[pltpu.VMEM((d, f2), BF16), pltpu.VMEM((8, f2), F32), pltpu.VMEM((f2 // 2, d), BF16)],
    )
    return pl.pallas_call(
        _expert_kernel,
        name="moe_experts",
        grid_spec=grid_spec,
        out_shape=jax.ShapeDtypeStruct(y.shape, y.dtype),
        input_output_aliases={8: 0},
        compiler_params=pltpu.CompilerParams(dimension_semantics=("arbitrary",), vmem_limit_bytes=EXPERT_VMEM_LIMIT),
    )(tile_expert, tile_valid, tile_first, xg, w1, b1, w2, b2, y)


def _combine_kernel(y_ref, p_ref, x_ref, mod_ref, g_ref, o_ref, *, final):
    p = p_ref[...]
    acc = p[:, 0:1] * y_ref[0].astype(F32)
    for j in range(1, TOP_K):
        acc = acc + p[:, j:j + 1] * y_ref[j].astype(F32)
    x = x_ref[...] + mod_ref[5] * acc
    if final:
        x = x * lax.rsqrt(jnp.mean(x * x, axis=-1, keepdims=True) + NORM_EPS) * g_ref[...]
    o_ref[...] = x


def _combine(y4, probs, x, mod_l, final_g, final):
    batch, seq, d = x.shape
    tm = min(512, seq)
    return pl.pallas_call(
        functools.partial(_combine_kernel, final=final),
        name="moe_combine",
        grid=(batch, seq // tm),
        in_specs=[
            pl.BlockSpec((TOP_K, None, tm, d), lambda b, i: (0, b, i, 0)),
            pl.BlockSpec((None, tm, TOP_K), lambda b, i: (b, i, 0)),
            pl.BlockSpec((None, tm, d), lambda b, i: (b, i, 0)),
            pl.BlockSpec((None, 6, 1, d), lambda b, i: (b, 0, 0, 0)),
            pl.BlockSpec((1, d), lambda b, i: (0, 0)),
        ],
        out_specs=pl.BlockSpec((None, tm, d), lambda b, i: (b, i, 0)),
        out_shape=jax.ShapeDtypeStruct((batch, seq, d), F32),
        compiler_params=_params("parallel", "parallel"),
    )(y4, probs, x, mod_l, final_g.reshape(1, d))


def _dispatch_plan(idx):
    tm = MOE_ROW_TILE
    flat = idx.reshape(-1)
    n_slots = flat.shape[0]
    experts = jnp.arange(N_EXPERTS, dtype=jnp.int32)
    counts = jnp.sum((flat[:, None] == experts[None, :]).astype(jnp.int32), axis=0)
    padded = ((counts + tm - 1) // tm) * tm
    padded_end = jnp.cumsum(padded)
    spare_used = jnp.arange(tm, dtype=jnp.int32)[None, :] < (padded - counts)[:, None]
    spare_key = jnp.where(spare_used, 2 * experts[:, None] + 1, 2 * N_EXPERTS).reshape(-1)
    order = jnp.argsort(jnp.concatenate([2 * flat, spare_key]), stable=True).astype(jnp.int32)
    src_token = jnp.where(order < n_slots, order // TOP_K, 0)
    slot_row = jnp.argsort(order).astype(jnp.int32)[:n_slots]
    tile_start = jnp.arange(order.shape[0] // tm, dtype=jnp.int32) * tm
    tile_expert = jnp.minimum(jnp.sum((padded_end[None, :] <= tile_start[:, None]).astype(jnp.int32), axis=1),
                              N_EXPERTS - 1)
    tile_valid = (tile_start < padded_end[-1]).astype(jnp.int32)
    previous = jnp.concatenate([jnp.full((1,), -1, jnp.int32), tile_expert[:-1]])
    tile_first = tile_valid * (tile_expert != previous).astype(jnp.int32)
    return src_token, slot_row, tile_expert, tile_valid, tile_first


def _moe(h2, idx, probs, x, mod_l, w1, b1, w2, b2, layer, final_g, final):
    batch, seq, d = x.shape
    n_tok = batch * seq
    src_token, slot_row, tile_expert, tile_valid, tile_first = _dispatch_plan(idx)
    tokens = h2.reshape(n_tok, d)
    n_tiles = tile_expert.shape[0]
    per_chunk = n_tiles // MOE_ROW_CHUNKS
    y = jnp.zeros((n_tiles * MOE_ROW_TILE, d), BF16)
    for k in range(MOE_ROW_CHUNKS):
        tiles = slice(k * per_chunk, (k + 1) * per_chunk)
        rows = slice(k * per_chunk * MOE_ROW_TILE, (k + 1) * per_chunk * MOE_ROW_TILE)
        first = tile_first[tiles].at[0].set(tile_valid[k * per_chunk])
        xg = tokens.at[src_token[rows]].get(mode="promise_in_bounds")
        y = _experts(xg, y, k * per_chunk, tile_expert[tiles], tile_valid[tiles], first, w1, b1, w2, b2, layer)
    rows_by_slot = slot_row.reshape(n_tok, TOP_K).T.reshape(-1)
    y4 = y.at[rows_by_slot].get(mode="promise_in_bounds").reshape(TOP_K, batch, seq, d)
    return _combine(y4, probs, x, mod_l, final_g, final)


def _t5_bucket(rel):
    nb = REL_BUCKETS // 2
    max_exact = nb // 2
    ret = jnp.where(rel > 0, nb, 0)
    n = jnp.abs(rel)
    nf = jnp.maximum(n, 1).astype(F32)
    large = max_exact + (jnp.log(nf / max_exact) / math.log(REL_MAX_DIST / max_exact)
                         * (nb - max_exact)).astype(jnp.int32)
    large = jnp.minimum(large, nb - 1)
    return ret + jnp.where(n < max_exact, n, large)


def _swa_bias_tiles(rel_bias):
    half, qb = SWA_HALF, SWA_QUERY_BLOCK
    win = qb + 2 * half
    n = win + qb
    tiles = []
    for g, (_, dil) in enumerate(SWA_PATTERNS):
        offs = jnp.arange(-half, half + 1) * dil
        table = rel_bias[_t5_bucket(offs)][:, g * SWA_GROUP_HEADS:(g + 1) * SWA_GROUP_HEADS].T
        by_lag = jnp.concatenate(
            [table, jnp.full((SWA_GROUP_HEADS, n - table.shape[1]), NEG_INF, table.dtype)], axis=1)
        rows = jnp.broadcast_to(by_lag[:, None, :], (SWA_GROUP_HEADS, qb, n)).reshape(SWA_GROUP_HEADS, qb * n)
        tiles.append(rows[:, :qb * (n - 1)].reshape(SWA_GROUP_HEADS, qb, n - 1)[:, :, :win].astype(F32))
    return tiles


def _diff_bias_segments(rel_bias, seq):
    tq = min(DIFF_Q_TILE, seq)
    table = rel_bias[:, len(SWA_PATTERNS) * SWA_GROUP_HEADS:].T
    by_rel = table[:, _t5_bucket(jnp.arange(2 * seq) - seq)]
    segs = [by_rel[:, seq - (i + 1) * tq:2 * seq - i * tq] for i in range(seq // tq)]
    return jnp.stack(segs, axis=1)[:, :, None, :].astype(F32)


def _pack_w_in(w_in_l):
    a_w = 3 * SWA_WIDTH
    b_w = 3 * RWKV_WIDTH + RWKV_LORA
    c_w = 3 * DIFF_WIDTH
    a, b, c, gates = (w_in_l[:, :a_w], w_in_l[:, a_w:a_w + b_w], w_in_l[:, a_w + b_w:a_w + b_w + c_w],
                      w_in_l[:, a_w + b_w + c_w:])
    gw = SWA_GROUP_WIDTH

    def group(g):
        return [a[:, t * SWA_WIDTH + g * gw:t * SWA_WIDTH + (g + 1) * gw] for t in range(3)]

    pad = jnp.zeros((w_in_l.shape[0], PROJ_COLS_PADDED - PROJ_COLS), w_in_l.dtype)
    return jnp.concatenate([gates] + group(1) + group(2) + group(0) + [c, b, pad], axis=1).astype(BF16)


def _direction_padded(w_up):
    z = jnp.zeros_like(w_up[0])
    return jnp.stack([jnp.concatenate([w_up[0], z], axis=0), jnp.concatenate([z, w_up[1]], axis=0)])


def kernel(x, c, w_mod, b_mod, norm1_g, norm2_g, w_in, rwkv_mu, rwkv_w0, rwkv_w_up, rwkv_a0, rwkv_a_up, rwkv_g_up, rwkv_k_k, rwkv_k_a, rwkv_r_k, rwkv_ln_g, rwkv_ln_b, diff_lambda, diff_subln_g, rel_bias, w_branch_a, w_branch_b, w_branch_c, w_out, router_w, router_b, moe_w1, moe_b1, moe_w2, moe_b2, final_norm_g):
    batch, seq, d = x.shape
    depth = w_mod.shape[0]
    mod = _modulation(c, w_mod, b_mod)
    swa_tiles = _swa_bias_tiles(rel_bias)
    bias_segs = _diff_bias_segments(rel_bias, seq)
    head_of = jnp.arange(RWKV_WIDTH) // HEAD_DIM
    gsum = (head_of[:, None] == head_of[None, :]).astype(BF16)
    b1 = moe_b1[:, :, None, :]
    b2 = moe_b2[:, :, None, :]

    for l in range(depth):
        mod_l = mod[l]
        proj, res1, res2 = _norm_proj(x, norm1_g[l], mod_l, _pack_w_in(w_in[l]))
        swa_outs = [_swa_group(proj, swa_tiles[0], 1, COL_A0 // SWA_GROUP_WIDTH),
                    _swa_group(res1, swa_tiles[1], SWA_PATTERNS[1][1], 0),
                    _swa_group(res2, swa_tiles[2], SWA_PATTERNS[2][1], 0)]
        r, v, g, bonus, kk, bb, kd, lw = _rwkv_prep(
            proj, rwkv_mu[l], rwkv_w0[l], _direction_padded(rwkv_w_up[l]), rwkv_a0[l],
            _direction_padded(rwkv_a_up[l]), rwkv_g_up[l], rwkv_k_k[l], rwkv_k_a[l], rwkv_r_k[l], gsum)
        y_fwd, y_bwd = _rwkv_scan(r, v, kk, bb, kd, lw)
        o_b = _rwkv_post(y_fwd, y_bwd, bonus, g, rwkv_ln_g[l], rwkv_ln_b[l], gsum)
        lambda_init = 0.8 - 0.6 * math.exp(-0.3 * l)
        o_c = _diff_attention(proj, bias_segs, diff_lambda[l], diff_subln_g[l], lambda_init)
        x, h2, idx, probs = _merge(
            proj, swa_outs, o_b, o_c, x, w_branch_a[l].astype(BF16), w_branch_b[l].astype(BF16),
            w_branch_c[l].astype(BF16), w_out[l].astype(BF16), mod_l, norm2_g[l], router_w[l], router_b[l])
        x = _moe(h2, idx, probs, x, mod_l, moe_w1, b1, moe_w2, b2, l, final_norm_g, l == depth - 1)
    return x
```

```python
import functools
import math

import jax
import jax.numpy as jnp
from jax import lax
from jax.experimental import pallas as pl
from jax.experimental.pallas import tpu as pltpu

F32 = jnp.float32
BF16 = jnp.bfloat16
HIGHEST = lax.Precision.HIGHEST

D_MODEL = 1024
HEAD_DIM = 64
LANES = 128
NORM_EPS = 1e-6
NEG_INF = -1e30
LOG2_E = 1.4426950408889634

SWA_PATTERNS = ((128, 1), (512, 4), (2048, 16))
SWA_GROUP_HEADS = 4
SWA_GROUP_WIDTH = SWA_GROUP_HEADS * HEAD_DIM
SWA_WIDTH = len(SWA_PATTERNS) * SWA_GROUP_WIDTH
SWA_HALF = 64
SWA_QUERY_BLOCK = 256

RWKV_HEADS = 12
RWKV_WIDTH = RWKV_HEADS * HEAD_DIM
RWKV_LORA = 384
RWKV_GN_EPS = 64e-5
RWKV_CHUNK = 64
RWKV_BATCH_BLOCK = 2
RWKV_PACK = 4

DIFF_HEADS = 6
DIFF_WIDTH = DIFF_HEADS * 2 * HEAD_DIM
DIFF_SUBLN_EPS = 1e-5
DIFF_Q_TILE = 512

REL_BUCKETS = 32
REL_MAX_DIST = 128

N_EXPERTS = 32
TOP_K = 4
D_EXPERT = 1024
SWIGLU_LIMIT = 7.0
SWIGLU_ALPHA = 1.702
MOE_ROW_TILE = 256
SWIGLU_BLOCK = 256
MOE_ROW_CHUNKS = 4

COL_GATES = 0
COL_A_DILATED = 3 * D_MODEL
COL_A0 = COL_A_DILATED + 2 * SWA_WIDTH
COL_C = COL_A0 + SWA_WIDTH
COL_R = COL_C + 3 * DIFF_WIDTH
COL_LORA = COL_R + 3 * RWKV_WIDTH
PROJ_COLS = COL_LORA + RWKV_LORA
PROJ_COLS_PADDED = 10752
PROJ_COL_TILE = 1536
STRIDED_COL_TILE = COL_A_DILATED // PROJ_COL_TILE
PROJ_ROW_TILE = 1024

VMEM_LIMIT = 48 * 1024 * 1024
EXPERT_VMEM_LIMIT = 56 * 1024 * 1024

_NT = ((1,), (1,))
_TN = ((0,), (0,))


def _params(*sem):
    return pltpu.CompilerParams(dimension_semantics=sem, vmem_limit_bytes=VMEM_LIMIT)


def _sigmoid(x):
    return 1.0 / (1.0 + jnp.exp(-x))


def _dot(a, b, dims=((1,), (0,)), precision=None):
    return lax.dot_general(a, b, (dims, ((), ())), precision=precision, preferred_element_type=F32)


def _hi_lo(x):
    hi = x.astype(BF16)
    return hi, (x - hi.astype(F32)).astype(BF16)


def _dot3(a, b):
    a_hi, a_lo = _hi_lo(a)
    b_hi, b_lo = _hi_lo(b)
    return _dot(a_hi, b_hi) + _dot(a_hi, b_lo) + _dot(a_lo, b_hi)


def _head_sum(x, ones_bf16):
    hi, lo = _hi_lo(x)
    g = ones_bf16.shape[0]
    return jnp.concatenate(
        [_dot(hi[:, c:c + g], ones_bf16) + _dot(lo[:, c:c + g], ones_bf16) for c in range(0, x.shape[1], g)], axis=-1)


def _mod_kernel(c_ref, w_ref, b_ref, o_ref):
    c = c_ref[...]
    cond = c * _sigmoid(c)
    o_ref[...] = _dot(cond, w_ref[...], precision=HIGHEST) + b_ref[...]


def _modulation(c, w_mod, b_mod):
    n_layers, d, n = w_mod.shape
    batch = c.shape[0]
    tn = 1536
    out = pl.pallas_call(
        _mod_kernel,
        name="modulation",
        grid=(n_layers, n // tn),
        in_specs=[
            pl.BlockSpec((batch, d), lambda l, j: (0, 0)),
            pl.BlockSpec((None, d, tn), lambda l, j: (l, 0, j)),
            pl.BlockSpec((None, 1, tn), lambda l, j: (l, 0, j)),
        ],
        out_specs=pl.BlockSpec((None, batch, tn), lambda l, j: (l, 0, j)),
        out_shape=jax.ShapeDtypeStruct((n_layers, batch, n), F32),
        compiler_params=_params("parallel", "parallel"),
    )(c, w_mod, b_mod.reshape(n_layers, 1, n))
    return out.reshape(n_layers, batch, 6, 1, d)


def _modulated_norm(x, g, scale, shift):
    y = x * lax.rsqrt(jnp.mean(x * x, axis=-1, keepdims=True) + NORM_EPS) * g
    return y * (1.0 + scale) + shift


def _norm_proj_kernel(x_ref, g_ref, mod_ref, w_ref, o_ref, res4_ref, res16_ref, h_ref, acc_ref):
    j = pl.program_id(2)

    @pl.when(j == 0)
    def _():
        h_ref[...] = _modulated_norm(x_ref[...], g_ref[...], mod_ref[1], mod_ref[0]).astype(BF16)

    @pl.when(j != STRIDED_COL_TILE)
    def _():
        o_ref[...] = _dot(h_ref[...], w_ref[...]).astype(o_ref.dtype)

    @pl.when(j == STRIDED_COL_TILE)
    def _():
        acc = _dot(h_ref[...], w_ref[...])
        o_ref[...] = acc.astype(o_ref.dtype)
        lanes = acc_ref.shape[2]
        tm = acc_ref.shape[1]
        for cb in range(acc_ref.shape[0]):
            acc_ref[cb] = acc[:, cb * lanes:(cb + 1) * lanes]
        per_group = SWA_WIDTH // lanes
        for res_ref, dil, first in ((res4_ref, SWA_PATTERNS[1][1], 0), (res16_ref, SWA_PATTERNS[2][1], per_group)):
            for r in range(dil):
                for cb in range(per_group):
                    res_ref[r, :, cb * lanes:(cb + 1) * lanes] = (
                        acc_ref[first + cb, pl.ds(r, tm // dil, stride=dil), :].astype(res_ref.dtype))


def _norm_proj(x, g, mod_l, w):
    batch, seq, d = x.shape
    n = w.shape[1]
    tm, tn = min(PROJ_ROW_TILE, seq), PROJ_COL_TILE
    d1, d2 = SWA_PATTERNS[1][1], SWA_PATTERNS[2][1]

    def res_spec(dil):
        return pl.BlockSpec((None, dil, tm // dil, SWA_WIDTH), lambda b, i, j: (b, 0, i, 0))

    return pl.pallas_call(
        _norm_proj_kernel,
        name="norm_proj",
        grid=(batch, seq // tm, n // tn),
        in_specs=[
            pl.BlockSpec((None, tm, d), lambda b, i, j: (b, i, 0)),
            pl.BlockSpec((1, d), lambda b, i, j: (0, 0)),
            pl.BlockSpec((None, 6, 1, d), lambda b, i, j: (b, 0, 0, 0)),
            pl.BlockSpec((d, tn), lambda b, i, j: (0, j)),
        ],
        out_specs=[pl.BlockSpec((None, tm, tn), lambda b, i, j: (b, i, j)), res_spec(d1), res_spec(d2)],
        out_shape=[jax.ShapeDtypeStruct((batch, seq, n), BF16),
                   jax.ShapeDtypeStruct((batch, d1, seq // d1, SWA_WIDTH), BF16),
                   jax.ShapeDtypeStruct((batch, d2, seq // d2, SWA_WIDTH), BF16)],
        scratch_shapes=[pltpu.VMEM((tm, d), BF16), pltpu.VMEM((tn // LANES, tm, LANES), F32)],
        compiler_params=_params("parallel", "parallel", "arbitrary"),
    )(x, g.reshape(1, d), mod_l, w)


def _swa_kernel(q_ref, k_ref, v_ref, bias_ref, o_ref, lse_ref, kpad_ref, vpad_ref, *, length):
    half, hd = SWA_HALF, HEAD_DIM
    qb = bias_ref.shape[1]
    win = qb + 2 * half
    zeros = jnp.zeros((half, SWA_GROUP_WIDTH), BF16)
    for pad_ref, src_ref in ((kpad_ref, k_ref), (vpad_ref, v_ref)):
        pad_ref[0:half, :] = zeros
        pad_ref[half + length:half + length + half, :] = zeros
        pad_ref[half:half + length, :] = src_ref[...]

    def block(n, carry):
        q0 = pl.multiple_of(n * qb, qb)
        q = q_ref[pl.ds(q0, qb), :]
        kw = kpad_ref[pl.ds(q0, win), :]
        vw = vpad_ref[pl.ds(q0, win), :]
        key_pos = q0 - half + lax.broadcasted_iota(jnp.int32, (1, win), 1)
        in_range = (key_pos >= 0) & (key_pos < length)
        heads = [slice(h * hd, (h + 1) * hd) for h in range(SWA_GROUP_HEADS)]
        scores = [jnp.where(in_range, _dot(q[:, sl], kw[:, sl], _NT) * (hd ** -0.5) + bias_ref[h], NEG_INF)
                  for h, sl in enumerate(heads)]
        maxes = [jnp.max(s, axis=-1, keepdims=True) for s in scores]
        probs = [jnp.exp(s - m) for s, m in zip(scores, maxes)]
        sums = [jnp.sum(p, axis=-1, keepdims=True) for p in probs]
        outs = [_dot(p.astype(BF16), vw[:, sl]) / l for p, sl, l in zip(probs, heads, sums)]
        lses = [jnp.broadcast_to(m + jnp.log(l), (qb, hd)) for m, l in zip(maxes, sums)]
        o_ref[pl.ds(q0, qb), :] = jnp.concatenate(outs, axis=-1)
        lse_ref[pl.ds(q0, qb), :] = jnp.concatenate(lses, axis=-1)
        return carry

    lax.fori_loop(0, length // qb, block, 0)


def _swa_group(qkv, bias_tile, dilation, col_block):
    gw = SWA_GROUP_WIDTH
    if dilation == 1:
        batch, length, _ = qkv.shape
        grid = (batch, 1)
        block = (None, length, gw)
        out_dims = (batch, length, gw)

        def at(col):
            return lambda b, r: (b, 0, col)
    else:
        batch, _, length, _ = qkv.shape
        grid = (batch, dilation)
        block = (None, None, length, gw)
        out_dims = (batch, dilation, length, gw)

        def at(col):
            return lambda b, r: (b, r, 0, col)

    out_spec = pl.BlockSpec(block, at(0))
    out_shape = jax.ShapeDtypeStruct(out_dims, F32)
    return pl.pallas_call(
        functools.partial(_swa_kernel, length=length),
        name="swa",
        grid=grid,
        in_specs=[pl.BlockSpec(block, at(col_block + which)) for which in range(3)]
        + [pl.BlockSpec(bias_tile.shape, lambda b, r: (0, 0, 0))],
        out_specs=[out_spec, out_spec],
        out_shape=[out_shape, out_shape],
        scratch_shapes=[pltpu.VMEM((length + 2 * SWA_HALF, gw), BF16)] * 2,
        compiler_params=_params("parallel", "parallel"),
    )(qkv, qkv, qkv, bias_tile)


def _diff_kernel(q_ref, k_ref, v_ref, seg_ref, dl_ref, g_ref, o_ref, bias_ref, *, lambda_init):
    hd = HEAD_DIM
    tq, seq = bias_ref.shape

    @pl.when(pl.program_id(2) == 0)
    def _():
        seg = jnp.broadcast_to(seg_ref[...] * LOG2_E, (tq, seq + tq))
        bias_ref[...] = pltpu.roll(seg, seq, 1, stride=1, stride_axis=0)[:, :seq]

    dl = dl_ref[...]
    lam = (jnp.exp(jnp.sum(dl[0:1] * dl[1:2], axis=-1, keepdims=True))
           - jnp.exp(jnp.sum(dl[2:3] * dl[3:4], axis=-1, keepdims=True)) + lambda_init)
    k = k_ref[...]
    v = v_ref[...]
    q = (q_ref[...].astype(F32) * (hd ** -0.5 * LOG2_E)).astype(BF16)
    chains = [(slice(r0, r0 + 128), slice(comp * hd, (comp + 1) * hd)) for r0 in range(0, tq, 128) for comp in range(2)]
    scores = [_dot(q[rows, sl], k[:, sl], _NT) + bias_ref[rows, :] for rows, sl in chains]
    probs = [jnp.exp2(s - jnp.max(s, axis=-1, keepdims=True)) for s in scores]
    outs = [_dot(p.astype(BF16), v) * (1.0 / jnp.sum(p, axis=-1, keepdims=True)) for p in probs]
    for i in range(0, len(chains), 2):
        o = outs[i] - lam * outs[i + 1]
        o = o * lax.rsqrt(jnp.mean(o * o, axis=-1, keepdims=True) + DIFF_SUBLN_EPS) * g_ref[...]
        o_ref[chains[i][0], :] = (o * (1.0 - lambda_init)).astype(o_ref.dtype)


def _diff_attention(proj, bias_segs, diff_lambda_l, subln_g, lambda_init):
    batch, seq, _ = proj.shape
    hw = 2 * HEAD_DIM
    tq = min(DIFF_Q_TILE, seq)
    base = COL_C // hw
    return pl.pallas_call(
        functools.partial(_diff_kernel, lambda_init=lambda_init),
        name="diff_attn",
        grid=(DIFF_HEADS, seq // tq, batch),
        in_specs=[
            pl.BlockSpec((None, tq, hw), lambda h, i, b: (b, i, base + h)),
            pl.BlockSpec((None, seq, hw), lambda h, i, b: (b, 0, base + DIFF_HEADS + h)),
            pl.BlockSpec((None, seq, hw), lambda h, i, b: (b, 0, base + 2 * DIFF_HEADS + h)),
            pl.BlockSpec((None, None, 1, seq + tq), lambda h, i, b: (h, i, 0, 0)),
            pl.BlockSpec((4, HEAD_DIM), lambda h, i, b: (0, 0)),
            pl.BlockSpec((1, hw), lambda h, i, b: (0, 0)),
        ],
        out_specs=pl.BlockSpec((None, tq, hw), lambda h, i, b: (b, i, h)),
        out_shape=jax.ShapeDtypeStruct((batch, seq, DIFF_WIDTH), BF16),
        scratch_shapes=[pltpu.VMEM((tq, seq), F32)],
        compiler_params=_params("parallel", "parallel", "arbitrary"),
    )(proj, proj, proj, bias_segs, diff_lambda_l, subln_g.reshape(1, hw))


def _shifted(cur_ref, prev_ref, next_ref, mu, first, last):
    x = cur_ref[...].astype(F32)
    rows = x.shape[0]
    halo = prev_ref.shape[0]
    before = jnp.where(first, 0.0, prev_ref[halo - 1:halo, :].astype(F32))
    after = jnp.where(last, 0.0, next_ref[0:1, :].astype(F32))
    row = lax.broadcasted_iota(jnp.int32, (rows, 1), 0)
    prev = jnp.where(row == 0, before, pltpu.roll(x, 1, axis=0))
    nxt = jnp.where(row == rows - 1, after, pltpu.roll(x, rows - 1, axis=0))
    return x + mu[0:1] * (prev - x) + mu[1:2] * (nxt - x)


def _rwkv_prep_kernel(r_ref, rp_ref, rn_ref, k_ref, kp_ref, kn_ref, v_ref, vp_ref, vn_ref,
                      lo_ref, lop_ref, lon_ref, mu_ref, w0_ref, wup_ref, a0_ref, aup_ref, gup_ref,
                      kk_scale_ref, ka_ref, rk_ref, gsum_ref,
                      r_out, v_out, g_out, bonus_out, kk_out, bb_out, kd_out, lw_out):
    w = RWKV_WIDTH
    first = pl.program_id(1) == 0
    last = pl.program_id(1) == pl.num_programs(1) - 1
    mu = mu_ref[...]
    r = _shifted(r_ref, rp_ref, rn_ref, mu[:, 0:w], first, last)
    k = _shifted(k_ref, kp_ref, kn_ref, mu[:, w:2 * w], first, last)
    v = _shifted(v_ref, vp_ref, vn_ref, mu[:, 2 * w:3 * w], first, last)
    lora = _shifted(lo_ref, lop_ref, lon_ref, mu[:, 3 * w:3 * w + RWKV_LORA], first, last)
    decay_in = jnp.tanh(lora[:, 0:128])
    iclr_in = lora[:, 128:256]
    gsum = gsum_ref[...]
    r_out[...] = r
    v_out[...] = v
    g_out[...] = _dot3(_sigmoid(lora[:, 256:384]), gup_ref[...])
    bonus = jnp.zeros_like(r)
    for di in range(2):
        z = w0_ref[di:di + 1, :] + _dot3(decay_in, wup_ref[di])
        u = -z
        softplus = jnp.maximum(u, 0.0) + jnp.log(1.0 + jnp.exp(-jnp.abs(u)))
        lw_out[di] = -jnp.exp(-softplus - 0.5)
        a = _sigmoid(a0_ref[di:di + 1, :] + _dot3(iclr_in, aup_ref[di]))
        kk = k * kk_scale_ref[di:di + 1, :]
        kk = kk * lax.rsqrt(jnp.maximum(_head_sum(kk * kk, gsum), 1e-24))
        kd = k * (1.0 + (a - 1.0) * ka_ref[di:di + 1, :])
        kk_out[di] = kk
        bb_out[di] = kk * a
        kd_out[di] = kd
        bonus = bonus + _head_sum(r * kd * rk_ref[...], gsum) * v
    bonus_out[...] = bonus


def _rwkv_prep(proj, mu, w0, wup2, a0, aup2, g_up, k_k, k_a, r_k, gsum):
    batch, seq, _ = proj.shape
    w = RWKV_WIDTH
    tt = min(256, seq)
    halo = 16
    hb = tt // halo
    last_halo = seq // halo - 1

    def cur(width, blk):
        return pl.BlockSpec((None, tt, width), lambda b, i: (b, i, blk))

    def prev(width, blk):
        return pl.BlockSpec((None, halo, width), lambda b, i: (b, jnp.maximum(i * hb - 1, 0), blk))

    def nxt(width, blk):
        return pl.BlockSpec((None, halo, width), lambda b, i: (b, jnp.minimum((i + 1) * hb, last_halo), blk))

    def full(shape):
        return pl.BlockSpec(shape, lambda b, i: (0,) * len(shape))

    in_specs = []
    for blk in (COL_R // w, COL_R // w + 1, COL_R // w + 2):
        in_specs += [cur(w, blk), prev(w, blk), nxt(w, blk)]
    lb = COL_LORA // RWKV_LORA
    in_specs += [cur(RWKV_LORA, lb), prev(RWKV_LORA, lb), nxt(RWKV_LORA, lb)]
    in_specs += [full(mu.shape), full(w0.shape), full(wup2.shape), full(a0.shape), full(aup2.shape),
                 full(g_up.shape), full(k_k.shape), full(k_a.shape), full((1, w)), full(gsum.shape)]
    shared = pl.BlockSpec((None, tt, w), lambda b, i: (b, i, 0))
    per_dir = pl.BlockSpec((2, None, tt, w), lambda b, i: (0, b, i, 0))
    shared_shape = jax.ShapeDtypeStruct((batch, seq, w), F32)
    per_dir_shape = jax.ShapeDtypeStruct((2, batch, seq, w), F32)
    return pl.pallas_call(
        _rwkv_prep_kernel,
        name="rwkv_prep",
        grid=(batch, seq // tt),
        in_specs=in_specs,
        out_specs=[shared] * 4 + [per_dir] * 4,
        out_shape=[shared_shape] * 4 + [per_dir_shape] * 4,
        compiler_params=_params("parallel", "parallel"),
    )(*([proj] * 12), mu, w0, wup2, a0, aup2, g_up, k_k, k_a, r_k.reshape(1, w), gsum)


def _rwkv_scan_kernel(rf_ref, rb_ref, vf_ref, vb_ref, kkf_ref, kkb_ref, bbf_ref, bbb_ref, kdf_ref, kdb_ref,
                      lwf_ref, lwb_ref, yf_ref, yb_ref, state_ref):
    chunk, hd, pack = RWKV_CHUNK, HEAD_DIM, RWKV_PACK
    pw = pack * hd
    groups = RWKV_HEADS // pack

    @pl.when(pl.program_id(1) == 0)
    def _():
        state_ref[...] = jnp.zeros_like(state_ref)

    ti = lax.broadcasted_iota(jnp.int32, (chunk, pw), 0)
    tj = lax.broadcasted_iota(jnp.int32, (chunk, pw), 1) % chunk
    eye = (ti == tj).astype(F32)
    same16 = (ti // 16) == (tj // 16)
    same32 = (ti // 32) == (tj // 32)
    diag = (lax.broadcasted_iota(jnp.int32, (pw, pw), 0) // hd) == (lax.broadcasted_iota(jnp.int32, (pw, pw), 1) // hd)
    diag_bf16 = diag.astype(BF16)

    def block_diag(x):
        return jnp.concatenate([x.astype(BF16)] * pack, axis=0) * diag_bf16

    def stacked(top, bottom):
        return jnp.concatenate([top, bottom], axis=0).astype(BF16)

    chains = []
    per_direction = ((rf_ref, vf_ref, kkf_ref, bbf_ref, kdf_ref, lwf_ref), (rb_ref, vb_ref, kkb_ref, bbb_ref, kdb_ref, lwb_ref))
    for n, (d, refs) in ((n, dr) for n in range(rf_ref.shape[0]) for dr in enumerate(per_direction)):
        r_ref, v_ref, kk_ref, bb_ref, kd_ref, lw_ref = (ref.at[n] for ref in refs)
        lag = ti - tj if d == 0 else tj - ti
        before, upto = lag > 0, lag >= 0
        lw = lw_ref[...]
        tri = upto[:, :chunk].astype(BF16)
        lw_hi = lw.astype(BF16)
        lw_rest = lw - lw_hi.astype(F32)
        lw_mid = lw_rest.astype(BF16)
        lw_lo = (lw_rest - lw_mid.astype(F32)).astype(BF16)
        cum = _dot(tri, lw_hi) + _dot(tri, lw_mid) + _dot(tri, lw_lo)
        total = jnp.sum(lw, axis=0, keepdims=True)
        p_inv = jnp.exp(-cum)
        p_rest = jnp.exp(total - cum)
        p_total = jnp.exp(total)
        kk, bb, kd = kk_ref[...], bb_ref[...], kd_ref[...]
        a_all = -kk * jnp.exp(cum - lw)
        b_all = bb * p_inv
        k_all = kd * p_inv
        r_all = r_ref[...] * jnp.exp(cum)
        b_end = bb * p_rest
        k_end = kd * p_rest
        v_all = v_ref[...]
        for g in range(groups):
            sl = slice(g * pw, (g + 1) * pw)
            chains.append(dict(
                n=n, d=d, g=g, before=before, upto=upto, ar=stacked(a_all[:, sl], r_all[:, sl]), b=b_all[:, sl],
                k=k_all[:, sl], v=v_all[:, sl], ends=stacked(b_end[:, sl], k_end[:, sl]), p_total=p_total[:, sl]))

    for c in chains:
        c["by_b"] = _dot(c["ar"], block_diag(c["b"]), _NT)
        c["by_k"] = _dot(c["ar"], block_diag(c["k"]), _NT)
    for c in chains:
        c["m_ab"] = jnp.where(c["before"], c["by_b"][:chunk], 0.0)
        c["m_rb"] = jnp.where(c["upto"], c["by_b"][chunk:], 0.0)
        c["m_kv"] = stacked(jnp.where(c["before"], c["by_k"][:chunk], 0.0), jnp.where(c["upto"], c["by_k"][chunk:], 0.0))
        c["x"] = jnp.where(same16, c["m_ab"], 0.0)
        c["inv"] = eye + c["x"]
    for c in chains:
        c["x"] = _dot(c["x"].astype(BF16), block_diag(c["x"]))
    for _ in range(2):
        for c in chains:
            both = _dot(stacked(c["x"], c["inv"]), block_diag(c["x"]))
            c["x"] = both[:chunk]
            c["inv"] = c["inv"] + both[chunk:]
    for c in chains:
        c["inv"] = c["inv"] + _dot(c["inv"].astype(BF16), block_diag(c["x"]))
    for level in range(2):
        for c in chains:
            off = jnp.where(same32 & ~same16, c["m_ab"], 0.0) if level == 0 else jnp.where(~same32, c["m_ab"], 0.0)
            c["inner"] = _dot(off.astype(BF16), block_diag(c["inv"]))
        for c in chains:
            c["inv"] = c["inv"] + _dot(c["inv"].astype(BF16), block_diag(c["inner"]))
    for c in chains:
        c["state"] = state_ref[c["n"], c["d"], c["g"]]
        c["by_state"] = _dot(c["ar"], c["state"].astype(BF16), _NT)
        c["by_v"] = _dot(c["m_kv"], block_diag(c["v"]))
    for c in chains:
        c["u"] = _dot(c["inv"].astype(BF16), block_diag(c["by_state"][:chunk] + c["by_v"][:chunk]))
    for c in chains:
        c["y"] = c["by_state"][chunk:] + c["by_v"][chunk:] + _dot(c["m_rb"].astype(BF16), block_diag(c["u"]))
        update = _dot(stacked(c["u"], c["v"]), c["ends"], _TN)
        state_ref[c["n"], c["d"], c["g"]] = c["state"] * c["p_total"] + jnp.where(diag, update, 0.0)
    for n in range(rf_ref.shape[0]):
        for d, y_ref in enumerate((yf_ref, yb_ref)):
            y_ref[n] = jnp.concatenate([c["y"] for c in chains if (c["n"], c["d"]) == (n, d)], axis=-1)


def _rwkv_scan(r, v, kk, bb, kd, lw):
    batch, seq, w = r.shape
    chunk = RWKV_CHUNK
    assert chunk == HEAD_DIM and RWKV_HEADS % RWKV_PACK == 0
    nc = seq // chunk
    pw = RWKV_PACK * HEAD_DIM
    groups = RWKV_HEADS // RWKV_PACK
    nb = RWKV_BATCH_BLOCK
    fwd = pl.BlockSpec((nb, chunk, w), lambda b, c: (b, c, 0))
    bwd = pl.BlockSpec((nb, chunk, w), lambda b, c: (b, nc - 1 - c, 0))
    fwd_dir = pl.BlockSpec((None, nb, chunk, w), lambda b, c: (0, b, c, 0))
    bwd_dir = pl.BlockSpec((None, nb, chunk, w), lambda b, c: (1, b, nc - 1 - c, 0))
    return pl.pallas_call(
        _rwkv_scan_kernel,
        name="rwkv_scan",
        grid=(batch // nb, nc),
        in_specs=[fwd, bwd, fwd, bwd] + [fwd_dir, bwd_dir] * 4,
        out_specs=[fwd, bwd],
        out_shape=[jax.ShapeDtypeStruct((batch, seq, w), F32)] * 2,
        scratch_shapes=[pltpu.VMEM((nb, 2, groups, pw, pw), F32)],
        compiler_params=_params("parallel", "arbitrary"),
    )(r, r, v, v, kk, kk, bb, bb, kd, kd, lw, lw)


def _rwkv_post_kernel(yf_ref, yb_ref, bonus_ref, g_ref, lng_ref, lnb_ref, gsum_ref, o_ref):
    gsum = gsum_ref[...]
    y = yf_ref[...] + yb_ref[...]
    mean = _head_sum(y, gsum) * (1.0 / HEAD_DIM)
    yc = y - mean
    var = _head_sum(yc * yc, gsum) * (1.0 / HEAD_DIM)
    yn = yc * lax.rsqrt(var + RWKV_GN_EPS) * lng_ref[...] + lnb_ref[...]
    o_ref[...] = ((yn + bonus_ref[...]) * g_ref[...]).astype(o_ref.dtype)


def _rwkv_post(y_fwd, y_bwd, bonus, g, ln_g, ln_b, gsum):
    batch, seq, w = y_fwd.shape
    tt = min(512, seq)
    shared = pl.BlockSpec((None, tt, w), lambda b, i: (b, i, 0))
    row = pl.BlockSpec((1, w), lambda b, i: (0, 0))
    return pl.pallas_call(
        _rwkv_post_kernel,
        name="rwkv_post",
        grid=(batch, seq // tt),
        in_specs=[shared, shared, shared, shared, row, row,
                  pl.BlockSpec(gsum.shape, lambda b, i: (0, 0))],
        out_specs=shared,
        out_shape=jax.ShapeDtypeStruct((batch, seq, w), BF16),
        compiler_params=_params("parallel", "parallel"),
    )(y_fwd, y_bwd, bonus, g, ln_g.reshape(1, w), ln_b.reshape(1, w), gsum)


def _merge_kernel(ga_ref, gb_ref, gc_ref, oa0_ref, oa1_ref, oa2_ref, la0_ref, la1_ref, la2_ref,
                  ob_ref, oc_ref, x_ref, wa_ref, wb_ref, wc_ref, wo_ref, mod_ref, g2_ref, rw_ref, rb_ref,
                  x_out, h_out, idx_out, prob_out, *token_order):
    in_token_order = []
    for src_ref, dst_ref in zip((oa1_ref, la1_ref, oa2_ref, la2_ref), token_order):
        dil, per = src_ref.shape[0], src_ref.shape[1]
        for r in range(dil):
            for cb in range(dst_ref.shape[0]):
                dst_ref[cb, pl.ds(r, per, stride=dil), :] = src_ref[r, :, cb * LANES:(cb + 1) * LANES]
        in_token_order.append(jnp.concatenate([dst_ref[cb] for cb in range(dst_ref.shape[0])], axis=-1))
    oa1, la1, oa2, la2 = in_token_order
    lses = [la0_ref[...], la1, la2]
    m = jnp.maximum(jnp.maximum(lses[0], lses[1]), lses[2])
    es = [jnp.exp(l - m) for l in lses]
    inv = 1.0 / (es[0] + es[1] + es[2])
    o_a = (es[0] * oa0_ref[...] + es[1] * oa1 + es[2] * oa2) * inv
    merged = (_sigmoid(ga_ref[...].astype(F32)) * _dot(o_a.astype(BF16), wa_ref[...])
              + _sigmoid(gb_ref[...].astype(F32)) * _dot(ob_ref[...], wb_ref[...])
              + _sigmoid(gc_ref[...].astype(F32)) * _dot(oc_ref[...], wc_ref[...]))
    x = x_ref[...] + mod_ref[2] * _dot(merged.astype(BF16), wo_ref[...])
    x_out[...] = x
    h = _modulated_norm(x, g2_ref[...], mod_ref[4], mod_ref[3])
    h_out[...] = h.astype(h_out.dtype)

    logits = _dot3(h, rw_ref[...]) + rb_ref[...]
    lane = lax.broadcasted_iota(jnp.int32, logits.shape, 1)
    work = logits
    vals, idxs = [], []
    for _ in range(TOP_K):
        top = jnp.max(work, axis=-1, keepdims=True)
        first = jnp.min(jnp.where(work == top, lane, N_EXPERTS), axis=-1, keepdims=True)
        vals.append(top)
        idxs.append(first)
        work = jnp.where(lane == first, -jnp.inf, work)
    exps = [jnp.exp(t - vals[0]) for t in vals]
    denom = exps[0] + exps[1] + exps[2] + exps[3]
    idx_out[...] = jnp.concatenate(idxs, axis=-1)
    prob_out[...] = jnp.concatenate([e / denom for e in exps], axis=-1)


def _merge(proj, swa_outs, o_b, o_c, x, wa, wb, wc, wo, mod_l, g2, router_w, router_b):
    batch, seq, d = x.shape
    tm = min(512, seq)
    gw = SWA_GROUP_WIDTH

    def rows(width, blk=0):
        return pl.BlockSpec((None, tm, width), lambda b, i: (b, i, blk))

    def full(shape):
        return pl.BlockSpec(shape, lambda b, i: (0,) * len(shape))

    def residue_rows(dil):
        return pl.BlockSpec((None, dil, tm // dil, gw), lambda b, i: (b, 0, i, 0))

    o_list = [o for o, _ in swa_outs]
    l_list = [l for _, l in swa_outs]
    swa_specs = [rows(gw), residue_rows(SWA_PATTERNS[1][1]), residue_rows(SWA_PATTERNS[2][1])]
    return pl.pallas_call(
        _merge_kernel,
        name="merge",
        grid=(batch, seq // tm),
        in_specs=[rows(d, 0), rows(d, 1), rows(d, 2)] + swa_specs * 2
        + [rows(RWKV_WIDTH), rows(DIFF_WIDTH), rows(d), full(wa.shape), full(wb.shape), full(wc.shape),
           full(wo.shape), pl.BlockSpec((None, 6, 1, d), lambda b, i: (b, 0, 0, 0)), full((1, d)),
           full(router_w.shape), full((1, N_EXPERTS))],
        out_specs=[rows(d), rows(d), rows(TOP_K), rows(TOP_K)],
        out_shape=[jax.ShapeDtypeStruct((batch, seq, d), F32), jax.ShapeDtypeStruct((batch, seq, d), F32),
                   jax.ShapeDtypeStruct((batch, seq, TOP_K), jnp.int32),
                   jax.ShapeDtypeStruct((batch, seq, TOP_K), F32)],
        scratch_shapes=[pltpu.VMEM((gw // LANES, tm, LANES), F32)] * 4,
        compiler_params=_params("parallel", "parallel"),
    )(proj, proj, proj, *o_list, *l_list, o_b, o_c, x, wa, wb, wc, wo, mod_l, g2.reshape(1, d),
      router_w, router_b.reshape(1, N_EXPERTS))


def _expert_kernel(tile_expert_ref, tile_valid_ref, tile_first_ref, x_ref, w1_ref, b1_ref, w2_ref, b2_ref, *refs):
    o_ref, w1_split_ref, b1_split_ref, w2_bf16_ref = refs[-4:]
    t = pl.program_id(0)
    half = SWIGLU_BLOCK // 2

    @pl.when(tile_first_ref[t] != 0)
    def _():
        src = lax.broadcasted_iota(jnp.int32, (SWIGLU_BLOCK, SWIGLU_BLOCK), 0)
        dst = lax.broadcasted_iota(jnp.int32, (SWIGLU_BLOCK, SWIGLU_BLOCK), 1)
        perm = (src == jnp.where(dst < half, 2 * dst, 2 * (dst - half) + 1)).astype(BF16)
        b1 = jnp.broadcast_to(b1_ref[...], (8, b1_ref.shape[1]))
        b1_hi = b1.astype(BF16)
        b1_rest = b1 - b1_hi.astype(F32)
        b1_mid = b1_rest.astype(BF16)
        b1_lo = (b1_rest - b1_mid.astype(F32)).astype(BF16)
        for m in range(w1_ref.shape[1] // SWIGLU_BLOCK):
            cols = slice(m * SWIGLU_BLOCK, (m + 1) * SWIGLU_BLOCK)
            w1_split_ref[:, cols] = _dot(w1_ref[:, cols].astype(BF16), perm).astype(BF16)
            b1_split_ref[:, cols] = (_dot(b1_hi[:, cols], perm) + _dot(b1_mid[:, cols], perm)
                                     + _dot(b1_lo[:, cols], perm))
        w2_bf16_ref[...] = w2_ref[...].astype(BF16)

    @pl.when(tile_valid_ref[t] != 0)
    def _():
        hh = _dot(x_ref[...].astype(BF16), w1_split_ref[...]) + b1_split_ref[0:1, :]
        acts = []
        for m in range(hh.shape[1] // SWIGLU_BLOCK):
            glu = jnp.minimum(hh[:, m * SWIGLU_BLOCK:m * SWIGLU_BLOCK + half], SWIGLU_LIMIT)
            lin = jnp.clip(hh[:, m * SWIGLU_BLOCK + half:(m + 1) * SWIGLU_BLOCK], -SWIGLU_LIMIT, SWIGLU_LIMIT)
            acts.append((glu * _sigmoid(SWIGLU_ALPHA * glu) * (lin + 1.0)).astype(BF16))
        act = jnp.concatenate(acts, axis=-1)
        o_ref[...] = (_dot(act, w2_bf16_ref[...]) + b2_ref[...]).astype(o_ref.dtype)

    @pl.when(tile_valid_ref[t] == 0)
    def _():
        o_ref[...] = jnp.zeros_like(o_ref)


def _experts(xg, y, total_rows, first_tile, tile_expert, tile_valid, tile_first, w1, b1, w2, b2, layer):
    rows, d = xg.shape
    f2 = w1.shape[3]
    tm = MOE_ROW_TILE
    grid_spec = pltpu.PrefetchScalarGridSpec(
        num_scalar_prefetch=3,
        grid=(rows // tm,),
        in_specs=[
            pl.BlockSpec((tm, d), lambda t, te, tv, tf: (t, 0)),
            pl.BlockSpec((None, None, d, f2), lambda t, te, tv, tf: (layer, te[t], 0, 0)),
            pl.BlockSpec((None, None, 1, f2), lambda t, te, tv, tf: (layer, te[t], 0, 0)),
            pl.BlockSpec((None, None, f2 // 2, d), lambda t, te, tv, tf: (layer, te[t], 0, 0)),
            pl.BlockSpec((None, None, 1, d), lambda t, te, tv, tf: (layer, te[t], 0, 0)),
        ] + ([] if y is None else [pl.BlockSpec(memory_space=pl.ANY)]),
        out_specs=pl.BlockSpec((tm, d), lambda t, te, tv, tf: (first_tile + t, 0)),
        scratch_shapes=[pltpu.VMEM((d, f2), BF16), pltpu.VMEM((8, f2), F32), pltpu.VMEM((f2 // 2, d), BF16)],
    )
    return pl.pallas_call(
        _expert_kernel,
        name="moe_experts",
        grid_spec=grid_spec,
        out_shape=jax.ShapeDtypeStruct((total_rows, d), BF16),
        input_output_aliases={} if y is None else {8: 0},
        compiler_params=pltpu.CompilerParams(dimension_semantics=("arbitrary",), vmem_limit_bytes=EXPERT_VMEM_LIMIT),
    )(tile_expert, tile_valid, tile_first, xg, w1, b1, w2, b2, *(() if y is None else (y,)))


def _combine_kernel(y_ref, p_ref, x_ref, mod_ref, g_ref, o_ref, *, final):
    p = p_ref[...]
    acc = p[:, 0:1] * y_ref[0].astype(F32)
    for j in range(1, TOP_K):
        acc = acc + p[:, j:j + 1] * y_ref[j].astype(F32)
    x = x_ref[...] + mod_ref[5] * acc
    if final:
        x = x * lax.rsqrt(jnp.mean(x * x, axis=-1, keepdims=True) + NORM_EPS) * g_ref[...]
    o_ref[...] = x


def _combine(y4, probs, x, mod_l, final_g, final):
    batch, seq, d = x.shape
    tm = min(512, seq)
    return pl.pallas_call(
        functools.partial(_combine_kernel, final=final),
        name="moe_combine",
        grid=(batch, seq // tm),
        in_specs=[
            pl.BlockSpec((TOP_K, None, tm, d), lambda b, i: (0, b, i, 0)),
            pl.BlockSpec((None, tm, TOP_K), lambda b, i: (b, i, 0)),
            pl.BlockSpec((None, tm, d), lambda b, i: (b, i, 0)),
            pl.BlockSpec((None, 6, 1, d), lambda b, i: (b, 0, 0, 0)),
            pl.BlockSpec((1, d), lambda b, i: (0, 0)),
        ],
        out_specs=pl.BlockSpec((None, tm, d), lambda b, i: (b, i, 0)),
        out_shape=jax.ShapeDtypeStruct((batch, seq, d), F32),
        compiler_params=_params("parallel", "parallel"),
    )(y4, probs, x, mod_l, final_g.reshape(1, d))


def _dispatch_plan(idx):
    tm = MOE_ROW_TILE
    flat = idx.reshape(-1)
    n_slots = flat.shape[0]
    experts = jnp.arange(N_EXPERTS, dtype=jnp.int32)
    counts = jnp.sum((flat[:, None] == experts[None, :]).astype(jnp.int32), axis=0)
    padded = ((counts + tm - 1) // tm) * tm
    padded_end = jnp.cumsum(padded)
    spare_used = jnp.arange(tm, dtype=jnp.int32)[None, :] < (padded - counts)[:, None]
    spare_key = jnp.where(spare_used, 2 * experts[:, None] + 1, 2 * N_EXPERTS).reshape(-1)
    order = jnp.argsort(jnp.concatenate([2 * flat, spare_key]), stable=True).astype(jnp.int32)
    src_token = jnp.where(order < n_slots, order // TOP_K, 0)
    slot_row = jnp.argsort(order).astype(jnp.int32)[:n_slots]
    tile_start = jnp.arange(order.shape[0] // tm, dtype=jnp.int32) * tm
    tile_expert = jnp.minimum(jnp.sum((padded_end[None, :] <= tile_start[:, None]).astype(jnp.int32), axis=1),
                              N_EXPERTS - 1)
    tile_valid = (tile_start < padded_end[-1]).astype(jnp.int32)
    previous = jnp.concatenate([jnp.full((1,), -1, jnp.int32), tile_expert[:-1]])
    tile_first = tile_valid * (tile_expert != previous).astype(jnp.int32)
    return src_token, slot_row, tile_expert, tile_valid, tile_first


def _moe(h2, idx, probs, x, mod_l, w1, b1, w2, b2, layer, final_g, final):
    batch, seq, d = x.shape
    n_tok = batch * seq
    src_token, slot_row, tile_expert, tile_valid, tile_first = _dispatch_plan(idx)
    tokens = h2.reshape(n_tok, d)
    n_tiles = tile_expert.shape[0]
    per_chunk = n_tiles // MOE_ROW_CHUNKS
    y = None
    for k in range(MOE_ROW_CHUNKS):
        tiles = slice(k * per_chunk, (k + 1) * per_chunk)
        rows = slice(k * per_chunk * MOE_ROW_TILE, (k + 1) * per_chunk * MOE_ROW_TILE)
        first = tile_first[tiles].at[0].set(tile_valid[k * per_chunk])
        xg = tokens.at[src_token[rows]].get(mode="promise_in_bounds")
        y = _experts(xg, y, n_tiles * MOE_ROW_TILE, k * per_chunk, tile_expert[tiles], tile_valid[tiles], first,
                     w1, b1, w2, b2, layer)
    rows_by_slot = slot_row.reshape(n_tok, TOP_K).T.reshape(-1)
    y4 = y.at[rows_by_slot].get(mode="promise_in_bounds").reshape(TOP_K, batch, seq, d)
    return _combine(y4, probs, x, mod_l, final_g, final)


def _t5_bucket(rel):
    nb = REL_BUCKETS // 2
    max_exact = nb // 2
    ret = jnp.where(rel > 0, nb, 0)
    n = jnp.abs(rel)
    nf = jnp.maximum(n, 1).astype(F32)
    large = max_exact + (jnp.log(nf / max_exact) / math.log(REL_MAX_DIST / max_exact)
                         * (nb - max_exact)).astype(jnp.int32)
    large = jnp.minimum(large, nb - 1)
    return ret + jnp.where(n < max_exact, n, large)


def _swa_bias_tiles(rel_bias, seq):
    half = SWA_HALF
    tiles = []
    for g, (_, dil) in enumerate(SWA_PATTERNS):
        qb = min(SWA_QUERY_BLOCK, seq // dil)
        win = qb + 2 * half
        n = win + qb
        offs = jnp.arange(-half, half + 1) * dil
        table = rel_bias[_t5_bucket(offs)][:, g * SWA_GROUP_HEADS:(g + 1) * SWA_GROUP_HEADS].T
        by_lag = jnp.concatenate(
            [table, jnp.full((SWA_GROUP_HEADS, n - table.shape[1]), NEG_INF, table.dtype)], axis=1)
        rows = jnp.broadcast_to(by_lag[:, None, :], (SWA_GROUP_HEADS, qb, n)).reshape(SWA_GROUP_HEADS, qb * n)
        tiles.append(rows[:, :qb * (n - 1)].reshape(SWA_GROUP_HEADS, qb, n - 1)[:, :, :win].astype(F32))
    return tiles


def _diff_bias_segments(rel_bias, seq):
    tq = min(DIFF_Q_TILE, seq)
    table = rel_bias[:, len(SWA_PATTERNS) * SWA_GROUP_HEADS:].T
    by_rel = table[:, _t5_bucket(jnp.arange(2 * seq) - seq)]
    segs = [by_rel[:, seq - (i + 1) * tq:2 * seq - i * tq] for i in range(seq // tq)]
    return jnp.stack(segs, axis=1)[:, :, None, :].astype(F32)


def _pack_w_in(w_in_l):
    a_w = 3 * SWA_WIDTH
    b_w = 3 * RWKV_WIDTH + RWKV_LORA
    c_w = 3 * DIFF_WIDTH
    a, b, c, gates = (w_in_l[:, :a_w], w_in_l[:, a_w:a_w + b_w], w_in_l[:, a_w + b_w:a_w + b_w + c_w],
                      w_in_l[:, a_w + b_w + c_w:])
    gw = SWA_GROUP_WIDTH

    def group(g):
        return [a[:, t * SWA_WIDTH + g * gw:t * SWA_WIDTH + (g + 1) * gw] for t in range(3)]

    pad = jnp.zeros((w_in_l.shape[0], PROJ_COLS_PADDED - PROJ_COLS), w_in_l.dtype)
    return jnp.concatenate([gates] + group(1) + group(2) + group(0) + [c, b, pad], axis=1).astype(BF16)


def _direction_padded(w_up):
    z = jnp.zeros_like(w_up[0])
    return jnp.stack([jnp.concatenate([w_up[0], z], axis=0), jnp.concatenate([z, w_up[1]], axis=0)])


def kernel(x, c, w_mod, b_mod, norm1_g, norm2_g, w_in, rwkv_mu, rwkv_w0, rwkv_w_up, rwkv_a0, rwkv_a_up, rwkv_g_up, rwkv_k_k, rwkv_k_a, rwkv_r_k, rwkv_ln_g, rwkv_ln_b, diff_lambda, diff_subln_g, rel_bias, w_branch_a, w_branch_b, w_branch_c, w_out, router_w, router_b, moe_w1, moe_b1, moe_w2, moe_b2, final_norm_g):
    batch, seq, d = x.shape
    depth = w_mod.shape[0]
    mod = _modulation(c, w_mod, b_mod)
    swa_tiles = _swa_bias_tiles(rel_bias, seq)
    bias_segs = _diff_bias_segments(rel_bias, seq)
    head_of = jnp.arange(RWKV_PACK * HEAD_DIM) // HEAD_DIM
    gsum = (head_of[:, None] == head_of[None, :]).astype(BF16)
    b1 = moe_b1[:, :, None, :]
    b2 = moe_b2[:, :, None, :]

    for l in range(depth):
        mod_l = mod[l]
        proj, res1, res2 = _norm_proj(x, norm1_g[l], mod_l, _pack_w_in(w_in[l]))
        swa_outs = [_swa_group(proj, swa_tiles[0], 1, COL_A0 // SWA_GROUP_WIDTH),
                    _swa_group(res1, swa_tiles[1], SWA_PATTERNS[1][1], 0),
                    _swa_group(res2, swa_tiles[2], SWA_PATTERNS[2][1], 0)]
        r, v, g, bonus, kk, bb, kd, lw = _rwkv_prep(
            proj, rwkv_mu[l], rwkv_w0[l], _direction_padded(rwkv_w_up[l]), rwkv_a0[l],
            _direction_padded(rwkv_a_up[l]), rwkv_g_up[l], rwkv_k_k[l], rwkv_k_a[l], rwkv_r_k[l], gsum)
        y_fwd, y_bwd = _rwkv_scan(r, v, kk, bb, kd, lw)
        o_b = _rwkv_post(y_fwd, y_bwd, bonus, g, rwkv_ln_g[l], rwkv_ln_b[l], gsum)
        lambda_init = 0.8 - 0.6 * math.exp(-0.3 * l)
        o_c = _diff_attention(proj, bias_segs, diff_lambda[l], diff_subln_g[l], lambda_init)
        x, h2, idx, probs = _merge(
            proj, swa_outs, o_b, o_c, x, w_branch_a[l].astype(BF16), w_branch_b[l].astype(BF16),
            w_branch_c[l].astype(BF16), w_out[l].astype(BF16), mod_l, norm2_g[l], router_w[l], router_b[l])
        x = _moe(h2, idx, probs, x, mod_l, moe_w1, b1, moe_w2, b2, l, final_norm_g, l == depth - 1)
    return x
```

```python
import functools
import math

import jax
import jax.numpy as jnp
from jax import lax
from jax.experimental import pallas as pl
from jax.experimental.pallas import tpu as pltpu

F32 = jnp.float32
BF16 = jnp.bfloat16
HIGHEST = lax.Precision.HIGHEST

D_MODEL = 1024
HEAD_DIM = 64
LANES = 128
NORM_EPS = 1e-6
NEG_INF = -1e30
LOG2_E = 1.4426950408889634

SWA_PATTERNS = ((128, 1), (512, 4), (2048, 16))
SWA_GROUP_HEADS = 4
SWA_GROUP_WIDTH = SWA_GROUP_HEADS * HEAD_DIM
SWA_WIDTH = len(SWA_PATTERNS) * SWA_GROUP_WIDTH
SWA_HALF = 64
SWA_QUERY_BLOCK = 256

RWKV_HEADS = 12
RWKV_WIDTH = RWKV_HEADS * HEAD_DIM
RWKV_LORA = 384
RWKV_GN_EPS = 64e-5
RWKV_CHUNK = 64
RWKV_BATCH_BLOCK = 2
RWKV_PACK = 4

DIFF_HEADS = 6
DIFF_WIDTH = DIFF_HEADS * 2 * HEAD_DIM
DIFF_SUBLN_EPS = 1e-5
DIFF_Q_TILE = 512

REL_BUCKETS = 32
REL_MAX_DIST = 128

N_EXPERTS = 32
TOP_K = 4
D_EXPERT = 1024
SWIGLU_LIMIT = 7.0
SWIGLU_ALPHA = 1.702
MOE_ROW_TILE = 256
TOKEN_TABLE_COPIES = 2
SWIGLU_BLOCK = 256
MOE_ROW_CHUNKS = 4

COL_GATES = 0
COL_A_DILATED = 3 * D_MODEL
COL_A0 = COL_A_DILATED + 2 * SWA_WIDTH
COL_C = COL_A0 + SWA_WIDTH
COL_R = COL_C + 3 * DIFF_WIDTH
COL_LORA = COL_R + 3 * RWKV_WIDTH
PROJ_COLS = COL_LORA + RWKV_LORA
PROJ_COLS_PADDED = 10752
PROJ_COL_TILE = 1536
STRIDED_COL_TILE = COL_A_DILATED // PROJ_COL_TILE
PROJ_ROW_TILE = 1024

VMEM_LIMIT = 48 * 1024 * 1024
EXPERT_VMEM_LIMIT = 56 * 1024 * 1024

_NT = ((1,), (1,))
_TN = ((0,), (0,))


def _params(*sem):
    return pltpu.CompilerParams(dimension_semantics=sem, vmem_limit_bytes=VMEM_LIMIT)


def _sigmoid(x):
    return 1.0 / (1.0 + jnp.exp(-x))


def _dot(a, b, dims=((1,), (0,)), precision=None):
    return lax.dot_general(a, b, (dims, ((), ())), precision=precision, preferred_element_type=F32)


def _hi_lo(x):
    hi = x.astype(BF16)
    return hi, (x - hi.astype(F32)).astype(BF16)


def _dot3(a, b):
    a_hi, a_lo = _hi_lo(a)
    b_hi, b_lo = _hi_lo(b)
    return _dot(a_hi, b_hi) + _dot(a_hi, b_lo) + _dot(a_lo, b_hi)


def _head_sum(x, ones_bf16):
    hi, lo = _hi_lo(x)
    g = ones_bf16.shape[0]
    return jnp.concatenate(
        [_dot(hi[:, c:c + g], ones_bf16) + _dot(lo[:, c:c + g], ones_bf16) for c in range(0, x.shape[1], g)], axis=-1)


def _mod_kernel(c_ref, w_ref, b_ref, o_ref):
    c = c_ref[...]
    cond = c * _sigmoid(c)
    o_ref[...] = _dot(cond, w_ref[...], precision=HIGHEST) + b_ref[...]


def _modulation(c, w_mod, b_mod):
    n_layers, d, n = w_mod.shape
    batch = c.shape[0]
    tn = 1536
    out = pl.pallas_call(
        _mod_kernel,
        name="modulation",
        grid=(n_layers, n // tn),
        in_specs=[
            pl.BlockSpec((batch, d), lambda l, j: (0, 0)),
            pl.BlockSpec((None, d, tn), lambda l, j: (l, 0, j)),
            pl.BlockSpec((None, 1, tn), lambda l, j: (l, 0, j)),
        ],
        out_specs=pl.BlockSpec((None, batch, tn), lambda l, j: (l, 0, j)),
        out_shape=jax.ShapeDtypeStruct((n_layers, batch, n), F32),
        compiler_params=_params("parallel", "parallel"),
    )(c, w_mod, b_mod.reshape(n_layers, 1, n))
    return out.reshape(n_layers, batch, 6, 1, d)


def _modulated_norm(x, g, scale, shift):
    y = x * lax.rsqrt(jnp.mean(x * x, axis=-1, keepdims=True) + NORM_EPS) * g
    return y * (1.0 + scale) + shift


def _norm_proj_kernel(x_ref, g_ref, mod_ref, w_ref, o_ref, res4_ref, res16_ref, h_ref, acc_ref):
    j = pl.program_id(2)

    @pl.when(j == 0)
    def _():
        h_ref[...] = _modulated_norm(x_ref[...], g_ref[...], mod_ref[1], mod_ref[0]).astype(BF16)

    @pl.when(j != STRIDED_COL_TILE)
    def _():
        o_ref[...] = _dot(h_ref[...], w_ref[...]).astype(o_ref.dtype)

    @pl.when(j == STRIDED_COL_TILE)
    def _():
        acc = _dot(h_ref[...], w_ref[...])
        o_ref[...] = acc.astype(o_ref.dtype)
        lanes = acc_ref.shape[2]
        tm = acc_ref.shape[1]
        for cb in range(acc_ref.shape[0]):
            acc_ref[cb] = acc[:, cb * lanes:(cb + 1) * lanes]
        per_group = SWA_WIDTH // lanes
        for res_ref, dil, first in ((res4_ref, SWA_PATTERNS[1][1], 0), (res16_ref, SWA_PATTERNS[2][1], per_group)):
            for r in range(dil):
                for cb in range(per_group):
                    res_ref[r, :, cb * lanes:(cb + 1) * lanes] = (
                        acc_ref[first + cb, pl.ds(r, tm // dil, stride=dil), :].astype(res_ref.dtype))


def _norm_proj(x, g, mod_l, w):
    batch, seq, d = x.shape
    n = w.shape[1]
    tm, tn = min(PROJ_ROW_TILE, seq), PROJ_COL_TILE
    d1, d2 = SWA_PATTERNS[1][1], SWA_PATTERNS[2][1]

    def res_spec(dil):
        return pl.BlockSpec((None, dil, tm // dil, SWA_WIDTH), lambda b, i, j: (b, 0, i, 0))

    return pl.pallas_call(
        _norm_proj_kernel,
        name="norm_proj",
        grid=(batch, seq // tm, n // tn),
        in_specs=[
            pl.BlockSpec((None, tm, d), lambda b, i, j: (b, i, 0)),
            pl.BlockSpec((1, d), lambda b, i, j: (0, 0)),
            pl.BlockSpec((None, 6, 1, d), lambda b, i, j: (b, 0, 0, 0)),
            pl.BlockSpec((d, tn), lambda b, i, j: (0, j)),
        ],
        out_specs=[pl.BlockSpec((None, tm, tn), lambda b, i, j: (b, i, j)), res_spec(d1), res_spec(d2)],
        out_shape=[jax.ShapeDtypeStruct((batch, seq, n), BF16),
                   jax.ShapeDtypeStruct((batch, d1, seq // d1, SWA_WIDTH), BF16),
                   jax.ShapeDtypeStruct((batch, d2, seq // d2, SWA_WIDTH), BF16)],
        scratch_shapes=[pltpu.VMEM((tm, d), BF16), pltpu.VMEM((tn // LANES, tm, LANES), F32)],
        compiler_params=_params("parallel", "parallel", "arbitrary"),
    )(x, g.reshape(1, d), mod_l, w)


def _swa_kernel(q_ref, k_ref, v_ref, bias_ref, o_ref, lse_ref, kpad_ref, vpad_ref, *, length):
    half, hd = SWA_HALF, HEAD_DIM
    qb = bias_ref.shape[1]
    win = qb + 2 * half
    zeros = jnp.zeros((half, SWA_GROUP_WIDTH), BF16)
    for pad_ref, src_ref in ((kpad_ref, k_ref), (vpad_ref, v_ref)):
        pad_ref[0:half, :] = zeros
        pad_ref[half + length:half + length + half, :] = zeros
        pad_ref[half:half + length, :] = src_ref[...]

    def block(n, carry):
        q0 = pl.multiple_of(n * qb, qb)
        q = q_ref[pl.ds(q0, qb), :]
        kw = kpad_ref[pl.ds(q0, win), :]
        vw = vpad_ref[pl.ds(q0, win), :]
        key_pos = q0 - half + lax.broadcasted_iota(jnp.int32, (1, win), 1)
        in_range = (key_pos >= 0) & (key_pos < length)
        heads = [slice(h * hd, (h + 1) * hd) for h in range(SWA_GROUP_HEADS)]
        scores = [jnp.where(in_range, _dot(q[:, sl], kw[:, sl], _NT) * (hd ** -0.5) + bias_ref[h], NEG_INF)
                  for h, sl in enumerate(heads)]
        maxes = [jnp.max(s, axis=-1, keepdims=True) for s in scores]
        probs = [jnp.exp(s - m) for s, m in zip(scores, maxes)]
        sums = [jnp.sum(p, axis=-1, keepdims=True) for p in probs]
        outs = [_dot(p.astype(BF16), vw[:, sl]) / l for p, sl, l in zip(probs, heads, sums)]
        lses = [jnp.broadcast_to(m + jnp.log(l), (qb, hd)) for m, l in zip(maxes, sums)]
        o_ref[pl.ds(q0, qb), :] = jnp.concatenate(outs, axis=-1)
        lse_ref[pl.ds(q0, qb), :] = jnp.concatenate(lses, axis=-1)
        return carry

    lax.fori_loop(0, length // qb, block, 0)


def _swa_group(qkv, bias_tile, dilation, col_block):
    gw = SWA_GROUP_WIDTH
    if dilation == 1:
        batch, length, _ = qkv.shape
        grid = (batch, 1)
        block = (None, length, gw)
        out_dims = (batch, length, gw)

        def at(col):
            return lambda b, r: (b, 0, col)
    else:
        batch, _, length, _ = qkv.shape
        grid = (batch, dilation)
        block = (None, None, length, gw)
        out_dims = (batch, dilation, length, gw)

        def at(col):
            return lambda b, r: (b, r, 0, col)

    out_spec = pl.BlockSpec(block, at(0))
    out_shape = jax.ShapeDtypeStruct(out_dims, F32)
    return pl.pallas_call(
        functools.partial(_swa_kernel, length=length),
        name="swa",
        grid=grid,
        in_specs=[pl.BlockSpec(block, at(col_block + which)) for which in range(3)]
        + [pl.BlockSpec(bias_tile.shape, lambda b, r: (0, 0, 0))],
        out_specs=[out_spec, out_spec],
        out_shape=[out_shape, out_shape],
        scratch_shapes=[pltpu.VMEM((length + 2 * SWA_HALF, gw), BF16)] * 2,
        compiler_params=_params("parallel", "parallel"),
    )(qkv, qkv, qkv, bias_tile)


def _diff_kernel(q_ref, k_ref, v_ref, seg_ref, dl_ref, g_ref, o_ref, bias_ref, *, lambda_init):
    hd = HEAD_DIM
    tq, seq = bias_ref.shape

    @pl.when(pl.program_id(2) == 0)
    def _():
        seg = jnp.broadcast_to(seg_ref[...] * LOG2_E, (tq, seq + tq))
        bias_ref[...] = pltpu.roll(seg, seq, 1, stride=1, stride_axis=0)[:, :seq]

    dl = dl_ref[...]
    lam = (jnp.exp(jnp.sum(dl[0:1] * dl[1:2], axis=-1, keepdims=True))
           - jnp.exp(jnp.sum(dl[2:3] * dl[3:4], axis=-1, keepdims=True)) + lambda_init)
    k = k_ref[...]
    v = v_ref[...]
    q = (q_ref[...].astype(F32) * (hd ** -0.5 * LOG2_E)).astype(BF16)
    chains = [(slice(r0, r0 + 128), slice(comp * hd, (comp + 1) * hd)) for r0 in range(0, tq, 128) for comp in range(2)]
    scores = [_dot(q[rows, sl], k[:, sl], _NT) + bias_ref[rows, :] for rows, sl in chains]
    probs = [jnp.exp2(s - jnp.max(s, axis=-1, keepdims=True)) for s in scores]
    outs = [_dot(p.astype(BF16), v) * (1.0 / jnp.sum(p, axis=-1, keepdims=True)) for p in probs]
    for i in range(0, len(chains), 2):
        o = outs[i] - lam * outs[i + 1]
        o = o * lax.rsqrt(jnp.mean(o * o, axis=-1, keepdims=True) + DIFF_SUBLN_EPS) * g_ref[...]
        o_ref[chains[i][0], :] = (o * (1.0 - lambda_init)).astype(o_ref.dtype)


def _diff_attention(proj, bias_segs, diff_lambda_l, subln_g, lambda_init):
    batch, seq, _ = proj.shape
    hw = 2 * HEAD_DIM
    tq = min(DIFF_Q_TILE, seq)
    base = COL_C // hw
    return pl.pallas_call(
        functools.partial(_diff_kernel, lambda_init=lambda_init),
        name="diff_attn",
        grid=(DIFF_HEADS, seq // tq, batch),
        in_specs=[
            pl.BlockSpec((None, tq, hw), lambda h, i, b: (b, i, base + h)),
            pl.BlockSpec((None, seq, hw), lambda h, i, b: (b, 0, base + DIFF_HEADS + h)),
            pl.BlockSpec((None, seq, hw), lambda h, i, b: (b, 0, base + 2 * DIFF_HEADS + h)),
            pl.BlockSpec((None, None, 1, seq + tq), lambda h, i, b: (h, i, 0, 0)),
            pl.BlockSpec((4, HEAD_DIM), lambda h, i, b: (0, 0)),
            pl.BlockSpec((1, hw), lambda h, i, b: (0, 0)),
        ],
        out_specs=pl.BlockSpec((None, tq, hw), lambda h, i, b: (b, i, h)),
        out_shape=jax.ShapeDtypeStruct((batch, seq, DIFF_WIDTH), BF16),
        scratch_shapes=[pltpu.VMEM((tq, seq), F32)],
        compiler_params=_params("parallel", "parallel", "arbitrary"),
    )(proj, proj, proj, bias_segs, diff_lambda_l, subln_g.reshape(1, hw))


def _shifted(cur_ref, prev_ref, next_ref, mu, first, last):
    x = cur_ref[...].astype(F32)
    rows = x.shape[0]
    halo = prev_ref.shape[0]
    before = jnp.where(first, 0.0, prev_ref[halo - 1:halo, :].astype(F32))
    after = jnp.where(last, 0.0, next_ref[0:1, :].astype(F32))
    row = lax.broadcasted_iota(jnp.int32, (rows, 1), 0)
    prev = jnp.where(row == 0, before, pltpu.roll(x, 1, axis=0))
    nxt = jnp.where(row == rows - 1, after, pltpu.roll(x, rows - 1, axis=0))
    return x + mu[0:1] * (prev - x) + mu[1:2] * (nxt - x)


def _rwkv_prep_kernel(r_ref, rp_ref, rn_ref, k_ref, kp_ref, kn_ref, v_ref, vp_ref, vn_ref,
                      lo_ref, lop_ref, lon_ref, mu_ref, w0_ref, wup_ref, a0_ref, aup_ref, gup_ref,
                      kk_scale_ref, ka_ref, rk_ref, gsum_ref,
                      r_out, v_out, g_out, bonus_out, kk_out, bb_out, kd_out, lw_out):
    w = RWKV_WIDTH
    first = pl.program_id(1) == 0
    last = pl.program_id(1) == pl.num_programs(1) - 1
    mu = mu_ref[...]
    r = _shifted(r_ref, rp_ref, rn_ref, mu[:, 0:w], first, last)
    k = _shifted(k_ref, kp_ref, kn_ref, mu[:, w:2 * w], first, last)
    v = _shifted(v_ref, vp_ref, vn_ref, mu[:, 2 * w:3 * w], first, last)
    lora = _shifted(lo_ref, lop_ref, lon_ref, mu[:, 3 * w:3 * w + RWKV_LORA], first, last)
    decay_in = jnp.tanh(lora[:, 0:128])
    iclr_in = lora[:, 128:256]
    gsum = gsum_ref[...]
    r_out[...] = r
    v_out[...] = v
    g_out[...] = _dot3(_sigmoid(lora[:, 256:384]), gup_ref[...])
    bonus = jnp.zeros_like(r)
    for di in range(2):
        z = w0_ref[di:di + 1, :] + _dot3(decay_in, wup_ref[di])
        u = -z
        softplus = jnp.maximum(u, 0.0) + jnp.log(1.0 + jnp.exp(-jnp.abs(u)))
        lw_out[di] = -jnp.exp(-softplus - 0.5)
        a = _sigmoid(a0_ref[di:di + 1, :] + _dot3(iclr_in, aup_ref[di]))
        kk = k * kk_scale_ref[di:di + 1, :]
        kk = kk * lax.rsqrt(jnp.maximum(_head_sum(kk * kk, gsum), 1e-24))
        kd = k * (1.0 + (a - 1.0) * ka_ref[di:di + 1, :])
        kk_out[di] = kk
        bb_out[di] = kk * a
        kd_out[di] = kd
        bonus = bonus + _head_sum(r * kd * rk_ref[...], gsum) * v
    bonus_out[...] = bonus


def _rwkv_prep(proj, mu, w0, wup2, a0, aup2, g_up, k_k, k_a, r_k, gsum):
    batch, seq, _ = proj.shape
    w = RWKV_WIDTH
    tt = min(256, seq)
    halo = 16
    hb = tt // halo
    last_halo = seq // halo - 1

    def cur(width, blk):
        return pl.BlockSpec((None, tt, width), lambda b, i: (b, i, blk))

    def prev(width, blk):
        return pl.BlockSpec((None, halo, width), lambda b, i: (b, jnp.maximum(i * hb - 1, 0), blk))

    def nxt(width, blk):
        return pl.BlockSpec((None, halo, width), lambda b, i: (b, jnp.minimum((i + 1) * hb, last_halo), blk))

    def full(shape):
        return pl.BlockSpec(shape, lambda b, i: (0,) * len(shape))

    in_specs = []
    for blk in (COL_R // w, COL_R // w + 1, COL_R // w + 2):
        in_specs += [cur(w, blk), prev(w, blk), nxt(w, blk)]
    lb = COL_LORA // RWKV_LORA
    in_specs += [cur(RWKV_LORA, lb), prev(RWKV_LORA, lb), nxt(RWKV_LORA, lb)]
    in_specs += [full(mu.shape), full(w0.shape), full(wup2.shape), full(a0.shape), full(aup2.shape),
                 full(g_up.shape), full(k_k.shape), full(k_a.shape), full((1, w)), full(gsum.shape)]
    shared = pl.BlockSpec((None, tt, w), lambda b, i: (b, i, 0))
    per_dir = pl.BlockSpec((2, None, tt, w), lambda b, i: (0, b, i, 0))
    shared_shape = jax.ShapeDtypeStruct((batch, seq, w), F32)
    per_dir_shape = jax.ShapeDtypeStruct((2, batch, seq, w), F32)
    return pl.pallas_call(
        _rwkv_prep_kernel,
        name="rwkv_prep",
        grid=(batch, seq // tt),
        in_specs=in_specs,
        out_specs=[shared] * 4 + [per_dir] * 4,
        out_shape=[shared_shape] * 4 + [per_dir_shape] * 4,
        compiler_params=_params("parallel", "parallel"),
    )(*([proj] * 12), mu, w0, wup2, a0, aup2, g_up, k_k, k_a, r_k.reshape(1, w), gsum)


def _rwkv_scan_kernel(rf_ref, rb_ref, vf_ref, vb_ref, kkf_ref, kkb_ref, bbf_ref, bbb_ref, kdf_ref, kdb_ref,
                      lwf_ref, lwb_ref, yf_ref, yb_ref, state_ref):
    chunk, hd, pack = RWKV_CHUNK, HEAD_DIM, RWKV_PACK
    pw = pack * hd
    groups = RWKV_HEADS // pack

    @pl.when(pl.program_id(1) == 0)
    def _():
        state_ref[...] = jnp.zeros_like(state_ref)

    ti = lax.broadcasted_iota(jnp.int32, (chunk, pw), 0)
    tj = lax.broadcasted_iota(jnp.int32, (chunk, pw), 1) % chunk
    eye = (ti == tj).astype(F32)
    same16 = (ti // 16) == (tj // 16)
    same32 = (ti // 32) == (tj // 32)
    diag = (lax.broadcasted_iota(jnp.int32, (pw, pw), 0) // hd) == (lax.broadcasted_iota(jnp.int32, (pw, pw), 1) // hd)
    diag_bf16 = diag.astype(BF16)

    def block_diag(x):
        return jnp.concatenate([x.astype(BF16)] * pack, axis=0) * diag_bf16

    def stacked(top, bottom):
        return jnp.concatenate([top, bottom], axis=0).astype(BF16)

    chains = []
    per_direction = ((rf_ref, vf_ref, kkf_ref, bbf_ref, kdf_ref, lwf_ref), (rb_ref, vb_ref, kkb_ref, bbb_ref, kdb_ref, lwb_ref))
    for n, (d, refs) in ((n, dr) for n in range(rf_ref.shape[0]) for dr in enumerate(per_direction)):
        r_ref, v_ref, kk_ref, bb_ref, kd_ref, lw_ref = (ref.at[n] for ref in refs)
        lag = ti - tj if d == 0 else tj - ti
        before, upto = lag > 0, lag >= 0
        lw = lw_ref[...]
        tri = upto[:, :chunk].astype(BF16)
        lw_hi = lw.astype(BF16)
        lw_rest = lw - lw_hi.astype(F32)
        lw_mid = lw_rest.astype(BF16)
        lw_lo = (lw_rest - lw_mid.astype(F32)).astype(BF16)
        cum = _dot(tri, lw_hi) + _dot(tri, lw_mid) + _dot(tri, lw_lo)
        total = jnp.sum(lw, axis=0, keepdims=True)
        p_inv = jnp.exp(-cum)
        p_rest = jnp.exp(total - cum)
        p_total = jnp.exp(total)
        kk, bb, kd = kk_ref[...], bb_ref[...], kd_ref[...]
        a_all = -kk * jnp.exp(cum - lw)
        b_all = bb * p_inv
        k_all = kd * p_inv
        r_all = r_ref[...] * jnp.exp(cum)
        b_end = bb * p_rest
        k_end = kd * p_rest
        v_all = v_ref[...]
        for g in range(groups):
            sl = slice(g * pw, (g + 1) * pw)
            chains.append(dict(
                n=n, d=d, g=g, before=before, upto=upto, ar=stacked(a_all[:, sl], r_all[:, sl]), b=b_all[:, sl],
                k=k_all[:, sl], v=v_all[:, sl], ends=stacked(b_end[:, sl], k_end[:, sl]), p_total=p_total[:, sl]))

    for c in chains:
        c["by_b"] = _dot(c["ar"], block_diag(c["b"]), _NT)
        c["by_k"] = _dot(c["ar"], block_diag(c["k"]), _NT)
    for c in chains:
        c["m_ab"] = jnp.where(c["before"], c["by_b"][:chunk], 0.0)
        c["m_rb"] = jnp.where(c["upto"], c["by_b"][chunk:], 0.0)
        c["m_kv"] = stacked(jnp.where(c["before"], c["by_k"][:chunk], 0.0), jnp.where(c["upto"], c["by_k"][chunk:], 0.0))
        c["x"] = jnp.where(same16, c["m_ab"], 0.0)
        c["inv"] = eye + c["x"]
    for c in chains:
        c["x"] = _dot(c["x"].astype(BF16), block_diag(c["x"]))
    for _ in range(2):
        for c in chains:
            both = _dot(stacked(c["x"], c["inv"]), block_diag(c["x"]))
            c["x"] = both[:chunk]
            c["inv"] = c["inv"] + both[chunk:]
    for c in chains:
        c["inv"] = c["inv"] + _dot(c["inv"].astype(BF16), block_diag(c["x"]))
    for level in range(2):
        for c in chains:
            off = jnp.where(same32 & ~same16, c["m_ab"], 0.0) if level == 0 else jnp.where(~same32, c["m_ab"], 0.0)
            c["inner"] = _dot(off.astype(BF16), block_diag(c["inv"]))
        for c in chains:
            c["inv"] = c["inv"] + _dot(c["inv"].astype(BF16), block_diag(c["inner"]))
    for c in chains:
        c["state"] = state_ref[c["n"], c["d"], c["g"]]
        c["by_state"] = _dot(c["ar"], c["state"].astype(BF16), _NT)
        c["by_v"] = _dot(c["m_kv"], block_diag(c["v"]))
    for c in chains:
        c["u"] = _dot(c["inv"].astype(BF16), block_diag(c["by_state"][:chunk] + c["by_v"][:chunk]))
    for c in chains:
        c["y"] = c["by_state"][chunk:] + c["by_v"][chunk:] + _dot(c["m_rb"].astype(BF16), block_diag(c["u"]))
        update = _dot(stacked(c["u"], c["v"]), c["ends"], _TN)
        state_ref[c["n"], c["d"], c["g"]] = c["state"] * c["p_total"] + jnp.where(diag, update, 0.0)
    for n in range(rf_ref.shape[0]):
        for d, y_ref in enumerate((yf_ref, yb_ref)):
            y_ref[n] = jnp.concatenate([c["y"] for c in chains if (c["n"], c["d"]) == (n, d)], axis=-1)


def _rwkv_scan(r, v, kk, bb, kd, lw):
    batch, seq, w = r.shape
    chunk = RWKV_CHUNK
    assert chunk == HEAD_DIM and RWKV_HEADS % RWKV_PACK == 0
    nc = seq // chunk
    pw = RWKV_PACK * HEAD_DIM
    groups = RWKV_HEADS // RWKV_PACK
    nb = RWKV_BATCH_BLOCK
    fwd = pl.BlockSpec((nb, chunk, w), lambda b, c: (b, c, 0))
    bwd = pl.BlockSpec((nb, chunk, w), lambda b, c: (b, nc - 1 - c, 0))
    fwd_dir = pl.BlockSpec((None, nb, chunk, w), lambda b, c: (0, b, c, 0))
    bwd_dir = pl.BlockSpec((None, nb, chunk, w), lambda b, c: (1, b, nc - 1 - c, 0))
    return pl.pallas_call(
        _rwkv_scan_kernel,
        name="rwkv_scan",
        grid=(batch // nb, nc),
        in_specs=[fwd, bwd, fwd, bwd] + [fwd_dir, bwd_dir] * 4,
        out_specs=[fwd, bwd],
        out_shape=[jax.ShapeDtypeStruct((batch, seq, w), F32)] * 2,
        scratch_shapes=[pltpu.VMEM((nb, 2, groups, pw, pw), F32)],
        compiler_params=_params("parallel", "arbitrary"),
    )(r, r, v, v, kk, kk, bb, bb, kd, kd, lw, lw)


def _rwkv_post_kernel(yf_ref, yb_ref, bonus_ref, g_ref, lng_ref, lnb_ref, gsum_ref, o_ref):
    gsum = gsum_ref[...]
    y = yf_ref[...] + yb_ref[...]
    mean = _head_sum(y, gsum) * (1.0 / HEAD_DIM)
    yc = y - mean
    var = _head_sum(yc * yc, gsum) * (1.0 / HEAD_DIM)
    yn = yc * lax.rsqrt(var + RWKV_GN_EPS) * lng_ref[...] + lnb_ref[...]
    o_ref[...] = ((yn + bonus_ref[...]) * g_ref[...]).astype(o_ref.dtype)


def _rwkv_post(y_fwd, y_bwd, bonus, g, ln_g, ln_b, gsum):
    batch, seq, w = y_fwd.shape
    tt = min(512, seq)
    shared = pl.BlockSpec((None, tt, w), lambda b, i: (b, i, 0))
    row = pl.BlockSpec((1, w), lambda b, i: (0, 0))
    return pl.pallas_call(
        _rwkv_post_kernel,
        name="rwkv_post",
        grid=(batch, seq // tt),
        in_specs=[shared, shared, shared, shared, row, row,
                  pl.BlockSpec(gsum.shape, lambda b, i: (0, 0))],
        out_specs=shared,
        out_shape=jax.ShapeDtypeStruct((batch, seq, w), BF16),
        compiler_params=_params("parallel", "parallel"),
    )(y_fwd, y_bwd, bonus, g, ln_g.reshape(1, w), ln_b.reshape(1, w), gsum)


def _merge_kernel(ga_ref, gb_ref, gc_ref, oa0_ref, oa1_ref, oa2_ref, la0_ref, la1_ref, la2_ref,
                  ob_ref, oc_ref, x_ref, wa_ref, wb_ref, wc_ref, wo_ref, mod_ref, g2_ref, rw_ref, rb_ref,
                  x_out, h_out, idx_out, prob_out, *token_order):
    in_token_order = []
    for src_ref, dst_ref in zip((oa1_ref, la1_ref, oa2_ref, la2_ref), token_order):
        dil, per = src_ref.shape[0], src_ref.shape[1]
        for r in range(dil):
            for cb in range(dst_ref.shape[0]):
                dst_ref[cb, pl.ds(r, per, stride=dil), :] = src_ref[r, :, cb * LANES:(cb + 1) * LANES]
        in_token_order.append(jnp.concatenate([dst_ref[cb] for cb in range(dst_ref.shape[0])], axis=-1))
    oa1, la1, oa2, la2 = in_token_order
    lses = [la0_ref[...], la1, la2]
    m = jnp.maximum(jnp.maximum(lses[0], lses[1]), lses[2])
    es = [jnp.exp(l - m) for l in lses]
    inv = 1.0 / (es[0] + es[1] + es[2])
    o_a = (es[0] * oa0_ref[...] + es[1] * oa1 + es[2] * oa2) * inv
    merged = (_sigmoid(ga_ref[...].astype(F32)) * _dot(o_a.astype(BF16), wa_ref[...])
              + _sigmoid(gb_ref[...].astype(F32)) * _dot(ob_ref[...], wb_ref[...])
              + _sigmoid(gc_ref[...].astype(F32)) * _dot(oc_ref[...], wc_ref[...]))
    x = x_ref[...] + mod_ref[2] * _dot(merged.astype(BF16), wo_ref[...])
    x_out[...] = x
    h = _modulated_norm(x, g2_ref[...], mod_ref[4], mod_ref[3])
    for copy in range(h_out.shape[0]):
        h_out[copy] = h.astype(h_out.dtype)

    logits = _dot3(h, rw_ref[...]) + rb_ref[...]
    lane = lax.broadcasted_iota(jnp.int32, logits.shape, 1)
    work = logits
    vals, idxs = [], []
    for _ in range(TOP_K):
        top = jnp.max(work, axis=-1, keepdims=True)
        first = jnp.min(jnp.where(work == top, lane, N_EXPERTS), axis=-1, keepdims=True)
        vals.append(top)
        idxs.append(first)
        work = jnp.where(lane == first, -jnp.inf, work)
    exps = [jnp.exp(t - vals[0]) for t in vals]
    denom = exps[0] + exps[1] + exps[2] + exps[3]
    idx_out[...] = jnp.concatenate(idxs, axis=-1)
    prob_out[...] = jnp.concatenate([e / denom for e in exps], axis=-1)


def _merge(proj, swa_outs, o_b, o_c, x, wa, wb, wc, wo, mod_l, g2, router_w, router_b):
    batch, seq, d = x.shape
    tm = min(512, seq)
    gw = SWA_GROUP_WIDTH

    def rows(width, blk=0):
        return pl.BlockSpec((None, tm, width), lambda b, i: (b, i, blk))

    def full(shape):
        return pl.BlockSpec(shape, lambda b, i: (0,) * len(shape))

    def residue_rows(dil):
        return pl.BlockSpec((None, dil, tm // dil, gw), lambda b, i: (b, 0, i, 0))

    o_list = [o for o, _ in swa_outs]
    l_list = [l for _, l in swa_outs]
    swa_specs = [rows(gw), residue_rows(SWA_PATTERNS[1][1]), residue_rows(SWA_PATTERNS[2][1])]
    return pl.pallas_call(
        _merge_kernel,
        name="merge",
        grid=(batch, seq // tm),
        in_specs=[rows(d, 0), rows(d, 1), rows(d, 2)] + swa_specs * 2
        + [rows(RWKV_WIDTH), rows(DIFF_WIDTH), rows(d), full(wa.shape), full(wb.shape), full(wc.shape),
           full(wo.shape), pl.BlockSpec((None, 6, 1, d), lambda b, i: (b, 0, 0, 0)), full((1, d)),
           full(router_w.shape), full((1, N_EXPERTS))],
        out_specs=[rows(d), pl.BlockSpec((None, TOKEN_TABLE_COPIES, tm, d), lambda b, i: (b, 0, i, 0)), rows(TOP_K),
                   rows(TOP_K)],
        out_shape=[jax.ShapeDtypeStruct((batch, seq, d), F32),
                   jax.ShapeDtypeStruct((batch, TOKEN_TABLE_COPIES, seq, d), BF16),
                   jax.ShapeDtypeStruct((batch, seq, TOP_K), jnp.int32),
                   jax.ShapeDtypeStruct((batch, seq, TOP_K), F32)],
        scratch_shapes=[pltpu.VMEM((gw // LANES, tm, LANES), F32)] * 4,
        compiler_params=_params("parallel", "parallel"),
    )(proj, proj, proj, *o_list, *l_list, o_b, o_c, x, wa, wb, wc, wo, mod_l, g2.reshape(1, d),
      router_w, router_b.reshape(1, N_EXPERTS))


def _expert_kernel(tile_expert_ref, tile_valid_ref, tile_first_ref, x_ref, w1_ref, b1_ref, w2_ref, b2_ref, *refs):
    o_ref, w1_split_ref, b1_split_ref, w2_bf16_ref = refs[-4:]
    t = pl.program_id(0)
    half = SWIGLU_BLOCK // 2

    @pl.when(tile_first_ref[t] != 0)
    def _():
        src = lax.broadcasted_iota(jnp.int32, (SWIGLU_BLOCK, SWIGLU_BLOCK), 0)
        dst = lax.broadcasted_iota(jnp.int32, (SWIGLU_BLOCK, SWIGLU_BLOCK), 1)
        perm = (src == jnp.where(dst < half, 2 * dst, 2 * (dst - half) + 1)).astype(BF16)
        b1 = jnp.broadcast_to(b1_ref[...], (8, b1_ref.shape[1]))
        b1_hi = b1.astype(BF16)
        b1_rest = b1 - b1_hi.astype(F32)
        b1_mid = b1_rest.astype(BF16)
        b1_lo = (b1_rest - b1_mid.astype(F32)).astype(BF16)
        for m in range(w1_ref.shape[1] // SWIGLU_BLOCK):
            cols = slice(m * SWIGLU_BLOCK, (m + 1) * SWIGLU_BLOCK)
            w1_split_ref[:, cols] = _dot(w1_ref[:, cols].astype(BF16), perm).astype(BF16)
            b1_split_ref[:, cols] = (_dot(b1_hi[:, cols], perm) + _dot(b1_mid[:, cols], perm)
                                     + _dot(b1_lo[:, cols], perm))
        w2_bf16_ref[...] = w2_ref[...].astype(BF16)

    @pl.when(tile_valid_ref[t] != 0)
    def _():
        hh = _dot(x_ref[...], w1_split_ref[...]) + b1_split_ref[0:1, :]
        acts = []
        for m in range(hh.shape[1] // SWIGLU_BLOCK):
            glu = jnp.minimum(hh[:, m * SWIGLU_BLOCK:m * SWIGLU_BLOCK + half], SWIGLU_LIMIT)
            lin = jnp.clip(hh[:, m * SWIGLU_BLOCK + half:(m + 1) * SWIGLU_BLOCK], -SWIGLU_LIMIT, SWIGLU_LIMIT)
            acts.append((glu * _sigmoid(SWIGLU_ALPHA * glu) * (lin + 1.0)).astype(BF16))
        act = jnp.concatenate(acts, axis=-1)
        o_ref[...] = (_dot(act, w2_bf16_ref[...]) + b2_ref[...]).astype(o_ref.dtype)

    @pl.when(tile_valid_ref[t] == 0)
    def _():
        o_ref[...] = jnp.zeros_like(o_ref)


def _experts(xg, y, total_rows, first_tile, tile_expert, tile_valid, tile_first, w1, b1, w2, b2, layer):
    rows, d = xg.shape
    f2 = w1.shape[3]
    tm = MOE_ROW_TILE
    grid_spec = pltpu.PrefetchScalarGridSpec(
        num_scalar_prefetch=3,
        grid=(rows // tm,),
        in_specs=[
            pl.BlockSpec((tm, d), lambda t, te, tv, tf: (t, 0)),
            pl.BlockSpec((None, None, d, f2), lambda t, te, tv, tf: (layer, te[t], 0, 0)),
            pl.BlockSpec((None, None, 1, f2), lambda t, te, tv, tf: (layer, te[t], 0, 0)),
            pl.BlockSpec((None, None, f2 // 2, d), lambda t, te, tv, tf: (layer, te[t], 0, 0)),
            pl.BlockSpec((None, None, 1, d), lambda t, te, tv, tf: (layer, te[t], 0, 0)),
        ] + ([] if y is None else [pl.BlockSpec(memory_space=pl.ANY)]),
        out_specs=pl.BlockSpec((tm, d), lambda t, te, tv, tf: (first_tile + t, 0)),
        scratch_shapes=[pltpu.VMEM((d, f2), BF16), pltpu.VMEM((8, f2), F32), pltpu.VMEM((f2 // 2, d), BF16)],
    )
    return pl.pallas_call(
        _expert_kernel,
        name="moe_experts",
        grid_spec=grid_spec,
        out_shape=jax.ShapeDtypeStruct((total_rows, d), BF16),
        input_output_aliases={} if y is None else {8: 0},
        compiler_params=pltpu.CompilerParams(dimension_semantics=("arbitrary",), vmem_limit_bytes=EXPERT_VMEM_LIMIT),
    )(tile_expert, tile_valid, tile_first, xg, w1, b1, w2, b2, *(() if y is None else (y,)))


def _combine_kernel(y_ref, p_ref, x_ref, mod_ref, g_ref, o_ref, *, final):
    p = p_ref[...]
    acc = p[:, 0:1] * y_ref[0].astype(F32)
    for j in range(1, TOP_K):
        acc = acc + p[:, j:j + 1] * y_ref[j].astype(F32)
    x = x_ref[...] + mod_ref[5] * acc
    if final:
        x = x * lax.rsqrt(jnp.mean(x * x, axis=-1, keepdims=True) + NORM_EPS) * g_ref[...]
    o_ref[...] = x


def _combine(y4, probs, x, mod_l, final_g, final):
    batch, seq, d = x.shape
    tm = min(512, seq)
    return pl.pallas_call(
        functools.partial(_combine_kernel, final=final),
        name="moe_combine",
        grid=(batch, seq // tm),
        in_specs=[
            pl.BlockSpec((TOP_K, None, tm, d), lambda b, i: (0, b, i, 0)),
            pl.BlockSpec((None, tm, TOP_K), lambda b, i: (b, i, 0)),
            pl.BlockSpec((None, tm, d), lambda b, i: (b, i, 0)),
            pl.BlockSpec((None, 6, 1, d), lambda b, i: (b, 0, 0, 0)),
            pl.BlockSpec((1, d), lambda b, i: (0, 0)),
        ],
        out_specs=pl.BlockSpec((None, tm, d), lambda b, i: (b, i, 0)),
        out_shape=jax.ShapeDtypeStruct((batch, seq, d), F32),
        compiler_params=_params("parallel", "parallel"),
    )(y4, probs, x, mod_l, final_g.reshape(1, d))


def _dispatch_plan(idx):
    tm = MOE_ROW_TILE
    flat = idx.reshape(-1)
    n_slots = flat.shape[0]
    experts = jnp.arange(N_EXPERTS, dtype=jnp.int32)
    counts = jnp.sum((flat[:, None] == experts[None, :]).astype(jnp.int32), axis=0)
    padded = ((counts + tm - 1) // tm) * tm
    padded_end = jnp.cumsum(padded)
    spare_used = jnp.arange(tm, dtype=jnp.int32)[None, :] < (padded - counts)[:, None]
    spare_key = jnp.where(spare_used, 2 * experts[:, None] + 1, 2 * N_EXPERTS).reshape(-1)
    order = jnp.argsort(jnp.concatenate([2 * flat, spare_key]), stable=True).astype(jnp.int32)
    src_token = jnp.where(order < n_slots, order // TOP_K, 0)
    slot_row = jnp.argsort(order).astype(jnp.int32)[:n_slots]
    tile_start = jnp.arange(order.shape[0] // tm, dtype=jnp.int32) * tm
    tile_expert = jnp.minimum(jnp.sum((padded_end[None, :] <= tile_start[:, None]).astype(jnp.int32), axis=1),
                              N_EXPERTS - 1)
    tile_valid = (tile_start < padded_end[-1]).astype(jnp.int32)
    previous = jnp.concatenate([jnp.full((1,), -1, jnp.int32), tile_expert[:-1]])
    tile_first = tile_valid * (tile_expert != previous).astype(jnp.int32)
    return src_token, slot_row, tile_expert, tile_valid, tile_first


def _moe(h2, idx, probs, x, mod_l, w1, b1, w2, b2, layer, final_g, final):
    batch, seq, d = x.shape
    n_tok = batch * seq
    src_token, slot_row, tile_expert, tile_valid, tile_first = _dispatch_plan(idx)
    tokens = h2.reshape(-1, d)
    src_row = (src_token // seq) * (h2.shape[1] * seq) + src_token % seq
    n_tiles = tile_expert.shape[0]
    per_chunk = n_tiles // MOE_ROW_CHUNKS
    y = None
    for k in range(MOE_ROW_CHUNKS):
        tiles = slice(k * per_chunk, (k + 1) * per_chunk)
        rows = slice(k * per_chunk * MOE_ROW_TILE, (k + 1) * per_chunk * MOE_ROW_TILE)
        first = tile_first[tiles].at[0].set(tile_valid[k * per_chunk])
        xg = tokens.at[src_row[rows]].get(mode="promise_in_bounds")
        y = _experts(xg, y, n_tiles * MOE_ROW_TILE, k * per_chunk, tile_expert[tiles], tile_valid[tiles], first,
                     w1, b1, w2, b2, layer)
    rows_by_slot = slot_row.reshape(n_tok, TOP_K).T.reshape(-1)
    y4 = y.at[rows_by_slot].get(mode="promise_in_bounds").reshape(TOP_K, batch, seq, d)
    return _combine(y4, probs, x, mod_l, final_g, final)


def _t5_bucket(rel):
    nb = REL_BUCKETS // 2
    max_exact = nb // 2
    ret = jnp.where(rel > 0, nb, 0)
    n = jnp.abs(rel)
    nf = jnp.maximum(n, 1).astype(F32)
    large = max_exact + (jnp.log(nf / max_exact) / math.log(REL_MAX_DIST / max_exact)
                         * (nb - max_exact)).astype(jnp.int32)
    large = jnp.minimum(large, nb - 1)
    return ret + jnp.where(n < max_exact, n, large)


def _swa_bias_tiles(rel_bias, seq):
    half = SWA_HALF
    tiles = []
    for g, (_, dil) in enumerate(SWA_PATTERNS):
        qb = min(SWA_QUERY_BLOCK, seq // dil)
        win = qb + 2 * half
        n = win + qb
        offs = jnp.arange(-half, half + 1) * dil
        table = rel_bias[_t5_bucket(offs)][:, g * SWA_GROUP_HEADS:(g + 1) * SWA_GROUP_HEADS].T
        by_lag = jnp.concatenate(
            [table, jnp.full((SWA_GROUP_HEADS, n - table.shape[1]), NEG_INF, table.dtype)], axis=1)
        rows = jnp.broadcast_to(by_lag[:, None, :], (SWA_GROUP_HEADS, qb, n)).reshape(SWA_GROUP_HEADS, qb * n)
        tiles.append(rows[:, :qb * (n - 1)].reshape(SWA_GROUP_HEADS, qb, n - 1)[:, :, :win].astype(F32))
    return tiles


def _diff_bias_segments(rel_bias, seq):
    tq = min(DIFF_Q_TILE, seq)
    table = rel_bias[:, len(SWA_PATTERNS) * SWA_GROUP_HEADS:].T
    by_rel = table[:, _t5_bucket(jnp.arange(2 * seq) - seq)]
    segs = [by_rel[:, seq - (i + 1) * tq:2 * seq - i * tq] for i in range(seq // tq)]
    return jnp.stack(segs, axis=1)[:, :, None, :].astype(F32)


def _pack_w_in(w_in_l):
    a_w = 3 * SWA_WIDTH
    b_w = 3 * RWKV_WIDTH + RWKV_LORA
    c_w = 3 * DIFF_WIDTH
    a, b, c, gates = (w_in_l[:, :a_w], w_in_l[:, a_w:a_w + b_w], w_in_l[:, a_w + b_w:a_w + b_w + c_w],
                      w_in_l[:, a_w + b_w + c_w:])
    gw = SWA_GROUP_WIDTH

    def group(g):
        return [a[:, t * SWA_WIDTH + g * gw:t * SWA_WIDTH + (g + 1) * gw] for t in range(3)]

    pad = jnp.zeros((w_in_l.shape[0], PROJ_COLS_PADDED - PROJ_COLS), w_in_l.dtype)
    return jnp.concatenate([gates] + group(1) + group(2) + group(0) + [c, b, pad], axis=1).astype(BF16)


def _direction_padded(w_up):
    z = jnp.zeros_like(w_up[0])
    return jnp.stack([jnp.concatenate([w_up[0], z], axis=0), jnp.concatenate([z, w_up[1]], axis=0)])


def kernel(x, c, w_mod, b_mod, norm1_g, norm2_g, w_in, rwkv_mu, rwkv_w0, rwkv_w_up, rwkv_a0, rwkv_a_up, rwkv_g_up, rwkv_k_k, rwkv_k_a, rwkv_r_k, rwkv_ln_g, rwkv_ln_b, diff_lambda, diff_subln_g, rel_bias, w_branch_a, w_branch_b, w_branch_c, w_out, router_w, router_b, moe_w1, moe_b1, moe_w2, moe_b2, final_norm_g):
    batch, seq, d = x.shape
    depth = w_mod.shape[0]
    mod = _modulation(c, w_mod, b_mod)
    swa_tiles = _swa_bias_tiles(rel_bias, seq)
    bias_segs = _diff_bias_segments(rel_bias, seq)
    head_of = jnp.arange(RWKV_PACK * HEAD_DIM) // HEAD_DIM
    gsum = (head_of[:, None] == head_of[None, :]).astype(BF16)
    b1 = moe_b1[:, :, None, :]
    b2 = moe_b2[:, :, None, :]

    for l in range(depth):
        mod_l = mod[l]
        proj, res1, res2 = _norm_proj(x, norm1_g[l], mod_l, _pack_w_in(w_in[l]))
        swa_outs = [_swa_group(proj, swa_tiles[0], 1, COL_A0 // SWA_GROUP_WIDTH),
                    _swa_group(res1, swa_tiles[1], SWA_PATTERNS[1][1], 0),
                    _swa_group(res2, swa_tiles[2], SWA_PATTERNS[2][1], 0)]
        r, v, g, bonus, kk, bb, kd, lw = _rwkv_prep(
            proj, rwkv_mu[l], rwkv_w0[l], _direction_padded(rwkv_w_up[l]), rwkv_a0[l],
            _direction_padded(rwkv_a_up[l]), rwkv_g_up[l], rwkv_k_k[l], rwkv_k_a[l], rwkv_r_k[l], gsum)
        y_fwd, y_bwd = _rwkv_scan(r, v, kk, bb, kd, lw)
        o_b = _rwkv_post(y_fwd, y_bwd, bonus, g, rwkv_ln_g[l], rwkv_ln_b[l], gsum)
        lambda_init = 0.8 - 0.6 * math.exp(-0.3 * l)
        o_c = _diff_attention(proj, bias_segs, diff_lambda[l], diff_subln_g[l], lambda_init)
        x, h2, idx, probs = _merge(
            proj, swa_outs, o_b, o_c, x, w_branch_a[l].astype(BF16), w_branch_b[l].astype(BF16),
            w_branch_c[l].astype(BF16), w_out[l].astype(BF16), mod_l, norm2_g[l], router_w[l], router_b[l])
        x = _moe(h2, idx, probs, x, mod_l, moe_w1, b1, moe_w2, b2, l, final_norm_g, l == depth - 1)
    return x
```

```python
import functools
import math

import jax
import jax.numpy as jnp
from jax import lax
from jax.experimental import pallas as pl
from jax.experimental.pallas import tpu as pltpu

F32 = jnp.float32
BF16 = jnp.bfloat16
HIGHEST = lax.Precision.HIGHEST

D_MODEL = 1024
HEAD_DIM = 64
LANES = 128
NORM_EPS = 1e-6
NEG_INF = -1e30
LOG2_E = 1.4426950408889634

SWA_PATTERNS = ((128, 1), (512, 4), (2048, 16))
SWA_GROUP_HEADS = 4
SWA_GROUP_WIDTH = SWA_GROUP_HEADS * HEAD_DIM
SWA_WIDTH = len(SWA_PATTERNS) * SWA_GROUP_WIDTH
SWA_HALF = 64
SWA_QUERY_BLOCK = 256

RWKV_HEADS = 12
RWKV_WIDTH = RWKV_HEADS * HEAD_DIM
RWKV_LORA = 384
RWKV_GN_EPS = 64e-5
RWKV_CHUNK = 64
RWKV_BATCH_BLOCK = 2
RWKV_PACK = 4

DIFF_HEADS = 6
DIFF_WIDTH = DIFF_HEADS * 2 * HEAD_DIM
DIFF_SUBLN_EPS = 1e-5
DIFF_Q_TILE = 512

REL_BUCKETS = 32
REL_MAX_DIST = 128

N_EXPERTS = 32
TOP_K = 4
D_EXPERT = 1024
SWIGLU_LIMIT = 7.0
SWIGLU_ALPHA = 1.702
MOE_ROW_TILE = 256
TOKEN_TABLE_COPIES = 2
SWIGLU_BLOCK = 256
MOE_ROW_CHUNKS = 4

COL_GATES = 0
COL_A_DILATED = 3 * D_MODEL
COL_A0 = COL_A_DILATED + 2 * SWA_WIDTH
COL_C = COL_A0 + SWA_WIDTH
COL_R = COL_C + 3 * DIFF_WIDTH
COL_LORA = COL_R + 3 * RWKV_WIDTH
PROJ_COLS = COL_LORA + RWKV_LORA
PROJ_COLS_PADDED = 10752
PROJ_COL_TILE = 1536
STRIDED_COL_TILE = COL_A_DILATED // PROJ_COL_TILE
PROJ_ROW_TILE = 1024

VMEM_LIMIT = 48 * 1024 * 1024
EXPERT_VMEM_LIMIT = 56 * 1024 * 1024

_NT = ((1,), (1,))
_TN = ((0,), (0,))


def _params(*sem):
    return pltpu.CompilerParams(dimension_semantics=sem, vmem_limit_bytes=VMEM_LIMIT)


def _sigmoid(x):
    return 1.0 / (1.0 + jnp.exp(-x))


def _dot(a, b, dims=((1,), (0,)), precision=None):
    return lax.dot_general(a, b, (dims, ((), ())), precision=precision, preferred_element_type=F32)


def _hi_lo(x):
    hi = x.astype(BF16)
    return hi, (x - hi.astype(F32)).astype(BF16)


def _dot3(a, b):
    a_hi, a_lo = _hi_lo(a)
    b_hi, b_lo = _hi_lo(b)
    return _dot(a_hi, b_hi) + _dot(a_hi, b_lo) + _dot(a_lo, b_hi)


def _head_sum(x, ones_bf16):
    hi, lo = _hi_lo(x)
    g = ones_bf16.shape[0]
    return jnp.concatenate(
        [_dot(hi[:, c:c + g], ones_bf16) + _dot(lo[:, c:c + g], ones_bf16) for c in range(0, x.shape[1], g)], axis=-1)


def _mod_kernel(c_ref, w_ref, b_ref, o_ref):
    c = c_ref[...]
    cond = c * _sigmoid(c)
    o_ref[...] = _dot(cond, w_ref[...], precision=HIGHEST) + b_ref[...]


def _modulation(c, w_mod, b_mod):
    n_layers, d, n = w_mod.shape
    batch = c.shape[0]
    tn = 1536
    out = pl.pallas_call(
        _mod_kernel,
        name="modulation",
        grid=(n_layers, n // tn),
        in_specs=[
            pl.BlockSpec((batch, d), lambda l, j: (0, 0)),
            pl.BlockSpec((None, d, tn), lambda l, j: (l, 0, j)),
            pl.BlockSpec((None, 1, tn), lambda l, j: (l, 0, j)),
        ],
        out_specs=pl.BlockSpec((None, batch, tn), lambda l, j: (l, 0, j)),
        out_shape=jax.ShapeDtypeStruct((n_layers, batch, n), F32),
        compiler_params=_params("parallel", "parallel"),
    )(c, w_mod, b_mod.reshape(n_layers, 1, n))
    return out.reshape(n_layers, batch, 6, 1, d)


def _modulated_norm(x, g, scale, shift):
    y = x * lax.rsqrt(jnp.mean(x * x, axis=-1, keepdims=True) + NORM_EPS) * g
    return y * (1.0 + scale) + shift


def _norm_proj_kernel(x_ref, g_ref, mod_ref, w_ref, o_ref, res4_ref, res16_ref, h_ref, acc_ref):
    j = pl.program_id(2)

    @pl.when(j == 0)
    def _():
        h_ref[...] = _modulated_norm(x_ref[...], g_ref[...], mod_ref[1], mod_ref[0]).astype(BF16)

    @pl.when(j != STRIDED_COL_TILE)
    def _():
        o_ref[...] = _dot(h_ref[...], w_ref[...]).astype(o_ref.dtype)

    @pl.when(j == STRIDED_COL_TILE)
    def _():
        acc = _dot(h_ref[...], w_ref[...])
        o_ref[...] = acc.astype(o_ref.dtype)
        lanes = acc_ref.shape[2]
        tm = acc_ref.shape[1]
        for cb in range(acc_ref.shape[0]):
            acc_ref[cb] = acc[:, cb * lanes:(cb + 1) * lanes]
        per_group = SWA_WIDTH // lanes
        for res_ref, dil, first in ((res4_ref, SWA_PATTERNS[1][1], 0), (res16_ref, SWA_PATTERNS[2][1], per_group)):
            for r in range(dil):
                for cb in range(per_group):
                    res_ref[r, :, cb * lanes:(cb + 1) * lanes] = (
                        acc_ref[first + cb, pl.ds(r, tm // dil, stride=dil), :].astype(res_ref.dtype))


def _norm_proj(x, g, mod_l, w):
    batch, seq, d = x.shape
    n = w.shape[1]
    tm, tn = min(PROJ_ROW_TILE, seq), PROJ_COL_TILE
    d1, d2 = SWA_PATTERNS[1][1], SWA_PATTERNS[2][1]

    def res_spec(dil):
        return pl.BlockSpec((None, dil, tm // dil, SWA_WIDTH), lambda b, i, j: (b, 0, i, 0))

    return pl.pallas_call(
        _norm_proj_kernel,
        name="norm_proj",
        grid=(batch, seq // tm, n // tn),
        in_specs=[
            pl.BlockSpec((None, tm, d), lambda b, i, j: (b, i, 0)),
            pl.BlockSpec((1, d), lambda b, i, j: (0, 0)),
            pl.BlockSpec((None, 6, 1, d), lambda b, i, j: (b, 0, 0, 0)),
            pl.BlockSpec((d, tn), lambda b, i, j: (0, j)),
        ],
        out_specs=[pl.BlockSpec((None, tm, tn), lambda b, i, j: (b, i, j)), res_spec(d1), res_spec(d2)],
        out_shape=[jax.ShapeDtypeStruct((batch, seq, n), BF16),
                   jax.ShapeDtypeStruct((batch, d1, seq // d1, SWA_WIDTH), BF16),
                   jax.ShapeDtypeStruct((batch, d2, seq // d2, SWA_WIDTH), BF16)],
        scratch_shapes=[pltpu.VMEM((tm, d), BF16), pltpu.VMEM((tn // LANES, tm, LANES), F32)],
        compiler_params=_params("parallel", "parallel", "arbitrary"),
    )(x, g.reshape(1, d), mod_l, w)


def _swa_kernel(q_ref, k_ref, v_ref, bias_ref, o_ref, lse_ref, kpad_ref, vpad_ref):
    half, hd = SWA_HALF, HEAD_DIM
    residues, length, _ = q_ref.shape
    qb = bias_ref.shape[1]
    win = qb + 2 * half
    zeros = jnp.zeros((half, SWA_GROUP_WIDTH), BF16)
    for pad_ref in (kpad_ref, vpad_ref):
        pad_ref[0:half, :] = zeros
        pad_ref[half + length:half + length + half, :] = zeros

    def residue(r, carry):
        kpad_ref[half:half + length, :] = k_ref[r]
        vpad_ref[half:half + length, :] = v_ref[r]

        def block(n, inner):
            q0 = pl.multiple_of(n * qb, qb)
            q = q_ref[r, pl.ds(q0, qb), :]
            kw = kpad_ref[pl.ds(q0, win), :]
            vw = vpad_ref[pl.ds(q0, win), :]
            key_pos = q0 - half + lax.broadcasted_iota(jnp.int32, (1, win), 1)
            in_range = (key_pos >= 0) & (key_pos < length)
            heads = [slice(h * hd, (h + 1) * hd) for h in range(SWA_GROUP_HEADS)]
            scores = [jnp.where(in_range, _dot(q[:, sl], kw[:, sl], _NT) * (hd ** -0.5) + bias_ref[h], NEG_INF)
                      for h, sl in enumerate(heads)]
            maxes = [jnp.max(s, axis=-1, keepdims=True) for s in scores]
            probs = [jnp.exp(s - m) for s, m in zip(scores, maxes)]
            sums = [jnp.sum(p, axis=-1, keepdims=True) for p in probs]
            outs = [_dot(p.astype(BF16), vw[:, sl]) / l for p, sl, l in zip(probs, heads, sums)]
            lses = [jnp.broadcast_to(m + jnp.log(l), (qb, hd)) for m, l in zip(maxes, sums)]
            o_ref[r, pl.ds(q0, qb), :] = jnp.concatenate(outs, axis=-1)
            lse_ref[r, pl.ds(q0, qb), :] = jnp.concatenate(lses, axis=-1)
            return inner

        lax.fori_loop(0, length // qb, block, 0)
        return carry

    lax.fori_loop(0, residues, residue, 0)


def _swa_group(qkv, bias_tile, col_block):
    gw = SWA_GROUP_WIDTH
    batch, residues, length, _ = qkv.shape
    block = (None, residues, length, gw)
    out_spec = pl.BlockSpec(block, lambda b: (b, 0, 0, 0))
    out_shape = jax.ShapeDtypeStruct((batch, residues, length, gw), F32)
    return pl.pallas_call(
        _swa_kernel,
        name="swa",
        grid=(batch,),
        in_specs=[pl.BlockSpec(block, functools.partial(lambda b, col: (b, 0, 0, col), col=col_block + which))
                  for which in range(3)]
        + [pl.BlockSpec(bias_tile.shape, lambda b: (0, 0, 0))],
        out_specs=[out_spec, out_spec],
        out_shape=[out_shape, out_shape],
        scratch_shapes=[pltpu.VMEM((length + 2 * SWA_HALF, gw), BF16)] * 2,
        compiler_params=_params("parallel"),
    )(qkv, qkv, qkv, bias_tile)


def _diff_kernel(q_ref, k_ref, v_ref, seg_ref, dl_ref, g_ref, o_ref, bias_ref, *, lambda_init):
    hd = HEAD_DIM
    tq, seq = bias_ref.shape

    @pl.when(pl.program_id(2) == 0)
    def _():
        seg = jnp.broadcast_to(seg_ref[...] * LOG2_E, (tq, seq + tq))
        bias_ref[...] = pltpu.roll(seg, seq, 1, stride=1, stride_axis=0)[:, :seq]

    dl = dl_ref[...]
    lam = (jnp.exp(jnp.sum(dl[0:1] * dl[1:2], axis=-1, keepdims=True))
           - jnp.exp(jnp.sum(dl[2:3] * dl[3:4], axis=-1, keepdims=True)) + lambda_init)
    k = k_ref[...]
    v = v_ref[...]
    q = (q_ref[...].astype(F32) * (hd ** -0.5 * LOG2_E)).astype(BF16)
    chains = [(slice(r0, r0 + 128), slice(comp * hd, (comp + 1) * hd)) for r0 in range(0, tq, 128) for comp in range(2)]
    ones_col = (lax.broadcasted_iota(jnp.int32, v.shape, 1) == 0).astype(BF16)
    v_wide = jnp.concatenate([v, ones_col], axis=-1)
    scores = [_dot(q[rows, sl], k[:, sl], _NT) + bias_ref[rows, :] for rows, sl in chains]
    probs = [jnp.exp2(s - jnp.max(s, axis=-1, keepdims=True)).astype(BF16) for s in scores]
    wide = [_dot(p, v_wide) for p in probs]
    outs = [w[:, :2 * hd] * (1.0 / w[:, 2 * hd:2 * hd + 1]) for w in wide]
    for i in range(0, len(chains), 2):
        o = outs[i] - lam * outs[i + 1]
        o = o * lax.rsqrt(jnp.mean(o * o, axis=-1, keepdims=True) + DIFF_SUBLN_EPS) * g_ref[...]
        o_ref[chains[i][0], :] = (o * (1.0 - lambda_init)).astype(o_ref.dtype)


def _diff_attention(proj, bias_segs, diff_lambda_l, subln_g, lambda_init):
    batch, seq, _ = proj.shape
    hw = 2 * HEAD_DIM
    tq = min(DIFF_Q_TILE, seq)
    base = COL_C // hw
    return pl.pallas_call(
        functools.partial(_diff_kernel, lambda_init=lambda_init),
        name="diff_attn",
        grid=(DIFF_HEADS, seq // tq, batch),
        in_specs=[
            pl.BlockSpec((None, tq, hw), lambda h, i, b: (b, i, base + h)),
            pl.BlockSpec((None, seq, hw), lambda h, i, b: (b, 0, base + DIFF_HEADS + h)),
            pl.BlockSpec((None, seq, hw), lambda h, i, b: (b, 0, base + 2 * DIFF_HEADS + h)),
            pl.BlockSpec((None, None, 1, seq + tq), lambda h, i, b: (h, i, 0, 0)),
            pl.BlockSpec((4, HEAD_DIM), lambda h, i, b: (0, 0)),
            pl.BlockSpec((1, hw), lambda h, i, b: (0, 0)),
        ],
        out_specs=pl.BlockSpec((None, tq, hw), lambda h, i, b: (b, i, h)),
        out_shape=jax.ShapeDtypeStruct((batch, seq, DIFF_WIDTH), BF16),
        scratch_shapes=[pltpu.VMEM((tq, seq), F32)],
        compiler_params=_params("parallel", "parallel", "arbitrary"),
    )(proj, proj, proj, bias_segs, diff_lambda_l, subln_g.reshape(1, hw))


def _shifted(cur_ref, prev_ref, next_ref, mu, first, last):
    x = cur_ref[...].astype(F32)
    rows = x.shape[0]
    halo = prev_ref.shape[0]
    before = jnp.where(first, 0.0, prev_ref[halo - 1:halo, :].astype(F32))
    after = jnp.where(last, 0.0, next_ref[0:1, :].astype(F32))
    row = lax.broadcasted_iota(jnp.int32, (rows, 1), 0)
    prev = jnp.where(row == 0, before, pltpu.roll(x, 1, axis=0))
    nxt = jnp.where(row == rows - 1, after, pltpu.roll(x, rows - 1, axis=0))
    return x + mu[0:1] * (prev - x) + mu[1:2] * (nxt - x)


def _rwkv_prep_kernel(r_ref, rp_ref, rn_ref, k_ref, kp_ref, kn_ref, v_ref, vp_ref, vn_ref,
                      lo_ref, lop_ref, lon_ref, mu_ref, w0_ref, wup_ref, a0_ref, aup_ref, gup_ref,
                      kk_scale_ref, ka_ref, rk_ref, gsum_ref,
                      r_out, v_out, g_out, bonus_out, kk_out, bb_out, kd_out, lw_out):
    w = RWKV_WIDTH
    first = pl.program_id(1) == 0
    last = pl.program_id(1) == pl.num_programs(1) - 1
    mu = mu_ref[...]
    r = _shifted(r_ref, rp_ref, rn_ref, mu[:, 0:w], first, last)
    k = _shifted(k_ref, kp_ref, kn_ref, mu[:, w:2 * w], first, last)
    v = _shifted(v_ref, vp_ref, vn_ref, mu[:, 2 * w:3 * w], first, last)
    lora = _shifted(lo_ref, lop_ref, lon_ref, mu[:, 3 * w:3 * w + RWKV_LORA], first, last)
    decay_in = jnp.tanh(lora[:, 0:128])
    iclr_in = lora[:, 128:256]
    gsum = gsum_ref[...]
    r_out[...] = r
    v_out[...] = v
    g_out[...] = _dot3(_sigmoid(lora[:, 256:384]), gup_ref[...])
    bonus = jnp.zeros_like(r)
    for di in range(2):
        z = w0_ref[di:di + 1, :] + _dot3(decay_in, wup_ref[di])
        u = -z
        softplus = jnp.maximum(u, 0.0) + jnp.log(1.0 + jnp.exp(-jnp.abs(u)))
        lw_out[di] = -jnp.exp(-softplus - 0.5)
        a = _sigmoid(a0_ref[di:di + 1, :] + _dot3(iclr_in, aup_ref[di]))
        kk = k * kk_scale_ref[di:di + 1, :]
        kk = kk * lax.rsqrt(jnp.maximum(_head_sum(kk * kk, gsum), 1e-24))
        kd = k * (1.0 + (a - 1.0) * ka_ref[di:di + 1, :])
        kk_out[di] = kk
        bb_out[di] = kk * a
        kd_out[di] = kd
        bonus = bonus + _head_sum(r * kd * rk_ref[...], gsum) * v
    bonus_out[...] = bonus


def _rwkv_prep(proj, mu, w0, wup2, a0, aup2, g_up, k_k, k_a, r_k, gsum):
    batch, seq, _ = proj.shape
    w = RWKV_WIDTH
    tt = min(256, seq)
    halo = 16
    hb = tt // halo
    last_halo = seq // halo - 1

    def cur(width, blk):
        return pl.BlockSpec((None, tt, width), lambda b, i: (b, i, blk))

    def prev(width, blk):
        return pl.BlockSpec((None, halo, width), lambda b, i: (b, jnp.maximum(i * hb - 1, 0), blk))

    def nxt(width, blk):
        return pl.BlockSpec((None, halo, width), lambda b, i: (b, jnp.minimum((i + 1) * hb, last_halo), blk))

    def full(shape):
        return pl.BlockSpec(shape, lambda b, i: (0,) * len(shape))

    in_specs = []
    for blk in (COL_R // w, COL_R // w + 1, COL_R // w + 2):
        in_specs += [cur(w, blk), prev(w, blk), nxt(w, blk)]
    lb = COL_LORA // RWKV_LORA
    in_specs += [cur(RWKV_LORA, lb), prev(RWKV_LORA, lb), nxt(RWKV_LORA, lb)]
    in_specs += [full(mu.shape), full(w0.shape), full(wup2.shape), full(a0.shape), full(aup2.shape),
                 full(g_up.shape), full(k_k.shape), full(k_a.shape), full((1, w)), full(gsum.shape)]
    shared = pl.BlockSpec((None, tt, w), lambda b, i: (b, i, 0))
    per_dir = pl.BlockSpec((2, None, tt, w), lambda b, i: (0, b, i, 0))
    shared_shape = jax.ShapeDtypeStruct((batch, seq, w), F32)
    per_dir_shape = jax.ShapeDtypeStruct((2, batch, seq, w), F32)
    return pl.pallas_call(
        _rwkv_prep_kernel,
        name="rwkv_prep",
        grid=(batch, seq // tt),
        in_specs=in_specs,
        out_specs=[shared] * 4 + [per_dir] * 4,
        out_shape=[shared_shape] * 4 + [per_dir_shape] * 4,
        compiler_params=_params("parallel", "parallel"),
    )(*([proj] * 12), mu, w0, wup2, a0, aup2, g_up, k_k, k_a, r_k.reshape(1, w), gsum)


def _rwkv_scan_kernel(rf_ref, rb_ref, vf_ref, vb_ref, kkf_ref, kkb_ref, bbf_ref, bbb_ref, kdf_ref, kdb_ref,
                      lwf_ref, lwb_ref, yf_ref, yb_ref, state_ref):
    chunk, hd, pack = RWKV_CHUNK, HEAD_DIM, RWKV_PACK
    pw = pack * hd
    groups = RWKV_HEADS // pack

    @pl.when(pl.program_id(1) == 0)
    def _():
        state_ref[...] = jnp.zeros_like(state_ref)

    ti = lax.broadcasted_iota(jnp.int32, (chunk, pw), 0)
    tj = lax.broadcasted_iota(jnp.int32, (chunk, pw), 1) % chunk
    eye = (ti == tj).astype(F32)
    same16 = (ti // 16) == (tj // 16)
    same32 = (ti // 32) == (tj // 32)
    diag = (lax.broadcasted_iota(jnp.int32, (pw, pw), 0) // hd) == (lax.broadcasted_iota(jnp.int32, (pw, pw), 1) // hd)
    diag_bf16 = diag.astype(BF16)

    def block_diag(x):
        return jnp.concatenate([x.astype(BF16)] * pack, axis=0) * diag_bf16

    def stacked(top, bottom):
        return jnp.concatenate([top, bottom], axis=0).astype(BF16)

    chains = []
    per_direction = ((rf_ref, vf_ref, kkf_ref, bbf_ref, kdf_ref, lwf_ref), (rb_ref, vb_ref, kkb_ref, bbb_ref, kdb_ref, lwb_ref))
    for n, (d, refs) in ((n, dr) for n in range(rf_ref.shape[0]) for dr in enumerate(per_direction)):
        r_ref, v_ref, kk_ref, bb_ref, kd_ref, lw_ref = (ref.at[n] for ref in refs)
        lag = ti - tj if d == 0 else tj - ti
        before, upto = lag > 0, lag >= 0
        lw = lw_ref[...]
        tri = upto[:, :chunk].astype(BF16)
        lw_hi = lw.astype(BF16)
        lw_rest = lw - lw_hi.astype(F32)
        lw_mid = lw_rest.astype(BF16)
        lw_lo = (lw_rest - lw_mid.astype(F32)).astype(BF16)
        cum = _dot(tri, lw_hi) + _dot(tri, lw_mid) + _dot(tri, lw_lo)
        total = jnp.sum(lw, axis=0, keepdims=True)
        p_inv = jnp.exp(-cum)
        p_rest = jnp.exp(total - cum)
        p_total = jnp.exp(total)
        kk, bb, kd = kk_ref[...], bb_ref[...], kd_ref[...]
        a_all = -kk * jnp.exp(cum - lw)
        b_all = bb * p_inv
        k_all = kd * p_inv
        r_all = r_ref[...] * jnp.exp(cum)
        b_end = bb * p_rest
        k_end = kd * p_rest
        v_all = v_ref[...]
        for g in range(groups):
            sl = slice(g * pw, (g + 1) * pw)
            chains.append(dict(
                n=n, d=d, g=g, before=before, upto=upto, ar=stacked(a_all[:, sl], r_all[:, sl]), b=b_all[:, sl],
                k=k_all[:, sl], v=v_all[:, sl], ends=stacked(b_end[:, sl], k_end[:, sl]), p_total=p_total[:, sl]))

    for c in chains:
        c["by_b"] = _dot(c["ar"], block_diag(c["b"]), _NT)
        c["by_k"] = _dot(c["ar"], block_diag(c["k"]), _NT)
    for c in chains:
        c["m_ab"] = jnp.where(c["before"], c["by_b"][:chunk], 0.0)
        c["m_rb"] = jnp.where(c["upto"], c["by_b"][chunk:], 0.0)
        c["m_kv"] = stacked(jnp.where(c["before"], c["by_k"][:chunk], 0.0), jnp.where(c["upto"], c["by_k"][chunk:], 0.0))
        c["x"] = jnp.where(same16, c["m_ab"], 0.0)
        c["inv"] = eye + c["x"]
    for c in chains:
        c["x"] = _dot(c["x"].astype(BF16), block_diag(c["x"]))
    for _ in range(2):
        for c in chains:
            both = _dot(stacked(c["x"], c["inv"]), block_diag(c["x"]))
            c["x"] = both[:chunk]
            c["inv"] = c["inv"] + both[chunk:]
    for c in chains:
        c["inv"] = c["inv"] + _dot(c["inv"].astype(BF16), block_diag(c["x"]))
    for level in range(2):
        for c in chains:
            off = jnp.where(same32 & ~same16, c["m_ab"], 0.0) if level == 0 else jnp.where(~same32, c["m_ab"], 0.0)
            c["inner"] = _dot(off.astype(BF16), block_diag(c["inv"]))
        for c in chains:
            c["inv"] = c["inv"] + _dot(c["inv"].astype(BF16), block_diag(c["inner"]))
    for c in chains:
        c["state"] = state_ref[c["n"], c["d"], c["g"]]
        c["by_state"] = _dot(c["ar"], c["state"].astype(BF16), _NT)
        c["by_v"] = _dot(c["m_kv"], block_diag(c["v"]))
    for c in chains:
        c["u"] = _dot(c["inv"].astype(BF16), block_diag(c["by_state"][:chunk] + c["by_v"][:chunk]))
    for c in chains:
        c["y"] = c["by_state"][chunk:] + c["by_v"][chunk:] + _dot(c["m_rb"].astype(BF16), block_diag(c["u"]))
        update = _dot(stacked(c["u"], c["v"]), c["ends"], _TN)
        state_ref[c["n"], c["d"], c["g"]] = c["state"] * c["p_total"] + jnp.where(diag, update, 0.0)
    for n in range(rf_ref.shape[0]):
        for d, y_ref in enumerate((yf_ref, yb_ref)):
            y_ref[n] = jnp.concatenate([c["y"] for c in chains if (c["n"], c["d"]) == (n, d)], axis=-1)


def _rwkv_scan(r, v, kk, bb, kd, lw):
    batch, seq, w = r.shape
    chunk = RWKV_CHUNK
    assert chunk == HEAD_DIM and RWKV_HEADS % RWKV_PACK == 0
    nc = seq // chunk
    pw = RWKV_PACK * HEAD_DIM
    groups = RWKV_HEADS // RWKV_PACK
    nb = RWKV_BATCH_BLOCK
    fwd = pl.BlockSpec((nb, chunk, w), lambda b, c: (b, c, 0))
    bwd = pl.BlockSpec((nb, chunk, w), lambda b, c: (b, nc - 1 - c, 0))
    fwd_dir = pl.BlockSpec((None, nb, chunk, w), lambda b, c: (0, b, c, 0))
    bwd_dir = pl.BlockSpec((None, nb, chunk, w), lambda b, c: (1, b, nc - 1 - c, 0))
    return pl.pallas_call(
        _rwkv_scan_kernel,
        name="rwkv_scan",
        grid=(batch // nb, nc),
        in_specs=[fwd, bwd, fwd, bwd] + [fwd_dir, bwd_dir] * 4,
        out_specs=[fwd, bwd],
        out_shape=[jax.ShapeDtypeStruct((batch, seq, w), F32)] * 2,
        scratch_shapes=[pltpu.VMEM((nb, 2, groups, pw, pw), F32)],
        compiler_params=_params("parallel", "arbitrary"),
    )(r, r, v, v, kk, kk, bb, bb, kd, kd, lw, lw)


def _rwkv_post_kernel(yf_ref, yb_ref, bonus_ref, g_ref, lng_ref, lnb_ref, gsum_ref, o_ref):
    gsum = gsum_ref[...]
    y = yf_ref[...] + yb_ref[...]
    mean = _head_sum(y, gsum) * (1.0 / HEAD_DIM)
    yc = y - mean
    var = _head_sum(yc * yc, gsum) * (1.0 / HEAD_DIM)
    yn = yc * lax.rsqrt(var + RWKV_GN_EPS) * lng_ref[...] + lnb_ref[...]
    o_ref[...] = ((yn + bonus_ref[...]) * g_ref[...]).astype(o_ref.dtype)


def _rwkv_post(y_fwd, y_bwd, bonus, g, ln_g, ln_b, gsum):
    batch, seq, w = y_fwd.shape
    tt = min(512, seq)
    shared = pl.BlockSpec((None, tt, w), lambda b, i: (b, i, 0))
    row = pl.BlockSpec((1, w), lambda b, i: (0, 0))
    return pl.pallas_call(
        _rwkv_post_kernel,
        name="rwkv_post",
        grid=(batch, seq // tt),
        in_specs=[shared, shared, shared, shared, row, row,
                  pl.BlockSpec(gsum.shape, lambda b, i: (0, 0))],
        out_specs=shared,
        out_shape=jax.ShapeDtypeStruct((batch, seq, w), BF16),
        compiler_params=_params("parallel", "parallel"),
    )(y_fwd, y_bwd, bonus, g, ln_g.reshape(1, w), ln_b.reshape(1, w), gsum)


def _merge_kernel(ga_ref, gb_ref, gc_ref, oa0_ref, oa1_ref, oa2_ref, la0_ref, la1_ref, la2_ref,
                  ob_ref, oc_ref, x_ref, wa_ref, wb_ref, wc_ref, wo_ref, mod_ref, g2_ref, rw_ref, rb_ref,
                  x_out, h_out, idx_out, prob_out, *token_order):
    in_token_order = []
    for src_ref, dst_ref in zip((oa1_ref, la1_ref, oa2_ref, la2_ref), token_order):
        dil, per = src_ref.shape[0], src_ref.shape[1]
        for r in range(dil):
            for cb in range(dst_ref.shape[0]):
                dst_ref[cb, pl.ds(r, per, stride=dil), :] = src_ref[r, :, cb * LANES:(cb + 1) * LANES]
        in_token_order.append(jnp.concatenate([dst_ref[cb] for cb in range(dst_ref.shape[0])], axis=-1))
    oa1, la1, oa2, la2 = in_token_order
    lses = [la0_ref[...], la1, la2]
    m = jnp.maximum(jnp.maximum(lses[0], lses[1]), lses[2])
    es = [jnp.exp(l - m) for l in lses]
    inv = 1.0 / (es[0] + es[1] + es[2])
    o_a = (es[0] * oa0_ref[...] + es[1] * oa1 + es[2] * oa2) * inv
    merged = (_sigmoid(ga_ref[...].astype(F32)) * _dot(o_a.astype(BF16), wa_ref[...])
              + _sigmoid(gb_ref[...].astype(F32)) * _dot(ob_ref[...], wb_ref[...])
              + _sigmoid(gc_ref[...].astype(F32)) * _dot(oc_ref[...], wc_ref[...]))
    x = x_ref[...] + mod_ref[2] * _dot(merged.astype(BF16), wo_ref[...])
    x_out[...] = x
    h = _modulated_norm(x, g2_ref[...], mod_ref[4], mod_ref[3])
    for copy in range(h_out.shape[0]):
        h_out[copy] = h.astype(h_out.dtype)

    logits = _dot3(h, rw_ref[...]) + rb_ref[...]
    lane = lax.broadcasted_iota(jnp.int32, logits.shape, 1)
    work = logits
    vals, idxs = [], []
    for _ in range(TOP_K):
        top = jnp.max(work, axis=-1, keepdims=True)
        first = jnp.min(jnp.where(work == top, lane, N_EXPERTS), axis=-1, keepdims=True)
        vals.append(top)
        idxs.append(first)
        work = jnp.where(lane == first, -jnp.inf, work)
    exps = [jnp.exp(t - vals[0]) for t in vals]
    denom = exps[0] + exps[1] + exps[2] + exps[3]
    idx_out[...] = jnp.concatenate(idxs, axis=-1)
    prob_out[...] = jnp.concatenate([e / denom for e in exps], axis=-1)


def _merge(proj, swa_outs, o_b, o_c, x, wa, wb, wc, wo, mod_l, g2, router_w, router_b):
    batch, seq, d = x.shape
    tm = min(512, seq)
    gw = SWA_GROUP_WIDTH

    def rows(width, blk=0):
        return pl.BlockSpec((None, tm, width), lambda b, i: (b, i, blk))

    def full(shape):
        return pl.BlockSpec(shape, lambda b, i: (0,) * len(shape))

    def residue_rows(dil):
        return pl.BlockSpec((None, dil, tm // dil, gw), lambda b, i: (b, 0, i, 0))

    o_list = [o for o, _ in swa_outs]
    l_list = [l for _, l in swa_outs]
    swa_specs = [rows(gw), residue_rows(SWA_PATTERNS[1][1]), residue_rows(SWA_PATTERNS[2][1])]
    return pl.pallas_call(
        _merge_kernel,
        name="merge",
        grid=(batch, seq // tm),
        in_specs=[rows(d, 0), rows(d, 1), rows(d, 2)] + swa_specs * 2
        + [rows(RWKV_WIDTH), rows(DIFF_WIDTH), rows(d), full(wa.shape), full(wb.shape), full(wc.shape),
           full(wo.shape), pl.BlockSpec((None, 6, 1, d), lambda b, i: (b, 0, 0, 0)), full((1, d)),
           full(router_w.shape), full((1, N_EXPERTS))],
        out_specs=[rows(d), pl.BlockSpec((None, TOKEN_TABLE_COPIES, tm, d), lambda b, i: (b, 0, i, 0)), rows(TOP_K),
                   rows(TOP_K)],
        out_shape=[jax.ShapeDtypeStruct((batch, seq, d), F32),
                   jax.ShapeDtypeStruct((batch, TOKEN_TABLE_COPIES, seq, d), BF16),
                   jax.ShapeDtypeStruct((batch, seq, TOP_K), jnp.int32),
                   jax.ShapeDtypeStruct((batch, seq, TOP_K), F32)],
        scratch_shapes=[pltpu.VMEM((gw // LANES, tm, LANES), F32)] * 4,
        compiler_params=_params("parallel", "parallel"),
    )(proj, proj, proj, *o_list, *l_list, o_b, o_c, x, wa, wb, wc, wo, mod_l, g2.reshape(1, d),
      router_w, router_b.reshape(1, N_EXPERTS))


def _expert_kernel(tile_expert_ref, tile_valid_ref, tile_first_ref, x_ref, w1_ref, b1_ref, w2_ref, b2_ref, *refs):
    o_ref, w1_split_ref, b1_split_ref, w2_bf16_ref = refs[-4:]
    t = pl.program_id(0)
    half = SWIGLU_BLOCK // 2

    @pl.when(tile_first_ref[t] != 0)
    def _():
        src = lax.broadcasted_iota(jnp.int32, (SWIGLU_BLOCK, SWIGLU_BLOCK), 0)
        dst = lax.broadcasted_iota(jnp.int32, (SWIGLU_BLOCK, SWIGLU_BLOCK), 1)
        perm = (src == jnp.where(dst < half, 2 * dst, 2 * (dst - half) + 1)).astype(BF16)
        b1 = jnp.broadcast_to(b1_ref[...], (8, b1_ref.shape[1]))
        b1_hi = b1.astype(BF16)
        b1_rest = b1 - b1_hi.astype(F32)
        b1_mid = b1_rest.astype(BF16)
        b1_lo = (b1_rest - b1_mid.astype(F32)).astype(BF16)
        for m in range(w1_ref.shape[1] // SWIGLU_BLOCK):
            cols = slice(m * SWIGLU_BLOCK, (m + 1) * SWIGLU_BLOCK)
            w1_split_ref[:, cols] = _dot(w1_ref[:, cols].astype(BF16), perm).astype(BF16)
            b1_split_ref[:, cols] = (_dot(b1_hi[:, cols], perm) + _dot(b1_mid[:, cols], perm)
                                     + _dot(b1_lo[:, cols], perm))
        w2_bf16_ref[...] = w2_ref[...].astype(BF16)

    @pl.when(tile_valid_ref[t] != 0)
    def _():
        hh = _dot(x_ref[...], w1_split_ref[...]) + b1_split_ref[0:1, :]
        acts = []
        for m in range(hh.shape[1] // SWIGLU_BLOCK):
            glu = jnp.minimum(hh[:, m * SWIGLU_BLOCK:m * SWIGLU_BLOCK + half], SWIGLU_LIMIT)
            lin = jnp.clip(hh[:, m * SWIGLU_BLOCK + half:(m + 1) * SWIGLU_BLOCK], -SWIGLU_LIMIT, SWIGLU_LIMIT)
            acts.append((glu * _sigmoid(SWIGLU_ALPHA * glu) * (lin + 1.0)).astype(BF16))
        act = jnp.concatenate(acts, axis=-1)
        o_ref[...] = (_dot(act, w2_bf16_ref[...]) + b2_ref[...]).astype(o_ref.dtype)

    @pl.when(tile_valid_ref[t] == 0)
    def _():
        o_ref[...] = jnp.zeros_like(o_ref)


def _experts(xg, y, total_rows, first_tile, tile_expert, tile_valid, tile_first, w1, b1, w2, b2, layer):
    rows, d = xg.shape
    f2 = w1.shape[3]
    tm = MOE_ROW_TILE
    grid_spec = pltpu.PrefetchScalarGridSpec(
        num_scalar_prefetch=3,
        grid=(rows // tm,),
        in_specs=[
            pl.BlockSpec((tm, d), lambda t, te, tv, tf: (t, 0)),
            pl.BlockSpec((None, None, d, f2), lambda t, te, tv, tf: (layer, te[t], 0, 0)),
            pl.BlockSpec((None, None, 1, f2), lambda t, te, tv, tf: (layer, te[t], 0, 0)),
            pl.BlockSpec((None, None, f2 // 2, d), lambda t, te, tv, tf: (layer, te[t], 0, 0)),
            pl.BlockSpec((None, None, 1, d), lambda t, te, tv, tf: (layer, te[t], 0, 0)),
        ] + ([] if y is None else [pl.BlockSpec(memory_space=pl.ANY)]),
        out_specs=pl.BlockSpec((tm, d), lambda t, te, tv, tf: (first_tile + t, 0)),
        scratch_shapes=[pltpu.VMEM((d, f2), BF16), pltpu.VMEM((8, f2), F32), pltpu.VMEM((f2 // 2, d), BF16)],
    )
    return pl.pallas_call(
        _expert_kernel,
        name="moe_experts",
        grid_spec=grid_spec,
        out_shape=jax.ShapeDtypeStruct((total_rows, d), BF16),
        input_output_aliases={} if y is None else {8: 0},
        compiler_params=pltpu.CompilerParams(dimension_semantics=("arbitrary",), vmem_limit_bytes=EXPERT_VMEM_LIMIT),
    )(tile_expert, tile_valid, tile_first, xg, w1, b1, w2, b2, *(() if y is None else (y,)))


def _combine_kernel(y_ref, p_ref, x_ref, mod_ref, g_ref, o_ref, *, final):
    p = p_ref[...]
    acc = p[:, 0:1] * y_ref[0].astype(F32)
    for j in range(1, TOP_K):
        acc = acc + p[:, j:j + 1] * y_ref[j].astype(F32)
    x = x_ref[...] + mod_ref[5] * acc
    if final:
        x = x * lax.rsqrt(jnp.mean(x * x, axis=-1, keepdims=True) + NORM_EPS) * g_ref[...]
    o_ref[...] = x


def _combine(y4, probs, x, mod_l, final_g, final):
    batch, seq, d = x.shape
    tm = min(512, seq)
    return pl.pallas_call(
        functools.partial(_combine_kernel, final=final),
        name="moe_combine",
        grid=(batch, seq // tm),
        in_specs=[
            pl.BlockSpec((TOP_K, None, tm, d), lambda b, i: (0, b, i, 0)),
            pl.BlockSpec((None, tm, TOP_K), lambda b, i: (b, i, 0)),
            pl.BlockSpec((None, tm, d), lambda b, i: (b, i, 0)),
            pl.BlockSpec((None, 6, 1, d), lambda b, i: (b, 0, 0, 0)),
            pl.BlockSpec((1, d), lambda b, i: (0, 0)),
        ],
        out_specs=pl.BlockSpec((None, tm, d), lambda b, i: (b, i, 0)),
        out_shape=jax.ShapeDtypeStruct((batch, seq, d), F32),
        compiler_params=_params("parallel", "parallel"),
    )(y4, probs, x, mod_l, final_g.reshape(1, d))


def _dispatch_plan(idx):
    tm = MOE_ROW_TILE
    flat = idx.reshape(-1)
    n_slots = flat.shape[0]
    experts = jnp.arange(N_EXPERTS, dtype=jnp.int32)
    counts = jnp.sum((flat[:, None] == experts[None, :]).astype(jnp.int32), axis=0)
    padded = ((counts + tm - 1) // tm) * tm
    padded_end = jnp.cumsum(padded)
    spare_used = jnp.arange(tm, dtype=jnp.int32)[None, :] < (padded - counts)[:, None]
    spare_key = jnp.where(spare_used, 2 * experts[:, None] + 1, 2 * N_EXPERTS).reshape(-1)
    order = jnp.argsort(jnp.concatenate([2 * flat, spare_key]), stable=True).astype(jnp.int32)
    src_token = jnp.where(order < n_slots, order // TOP_K, 0)
    slot_row = jnp.argsort(order).astype(jnp.int32)[:n_slots]
    tile_start = jnp.arange(order.shape[0] // tm, dtype=jnp.int32) * tm
    tile_expert = jnp.minimum(jnp.sum((padded_end[None, :] <= tile_start[:, None]).astype(jnp.int32), axis=1),
                              N_EXPERTS - 1)
    tile_valid = (tile_start < padded_end[-1]).astype(jnp.int32)
    previous = jnp.concatenate([jnp.full((1,), -1, jnp.int32), tile_expert[:-1]])
    tile_first = tile_valid * (tile_expert != previous).astype(jnp.int32)
    return src_token, slot_row, tile_expert, tile_valid, tile_first


def _moe(h2, idx, probs, x, mod_l, w1, b1, w2, b2, layer, final_g, final):
    batch, seq, d = x.shape
    n_tok = batch * seq
    src_token, slot_row, tile_expert, tile_valid, tile_first = _dispatch_plan(idx)
    tokens = h2.reshape(-1, d)
    src_row = (src_token // seq) * (h2.shape[1] * seq) + src_token % seq
    n_tiles = tile_expert.shape[0]
    per_chunk = n_tiles // MOE_ROW_CHUNKS
    y = None
    for k in range(MOE_ROW_CHUNKS):
        tiles = slice(k * per_chunk, (k + 1) * per_chunk)
        rows = slice(k * per_chunk * MOE_ROW_TILE, (k + 1) * per_chunk * MOE_ROW_TILE)
        first = tile_first[tiles].at[0].set(tile_valid[k * per_chunk])
        xg = tokens.at[src_row[rows]].get(mode="promise_in_bounds")
        y = _experts(xg, y, n_tiles * MOE_ROW_TILE, k * per_chunk, tile_expert[tiles], tile_valid[tiles], first,
                     w1, b1, w2, b2, layer)
    rows_by_slot = slot_row.reshape(n_tok, TOP_K).T.reshape(-1)
    y4 = y.at[rows_by_slot].get(mode="promise_in_bounds").reshape(TOP_K, batch, seq, d)
    return _combine(y4, probs, x, mod_l, final_g, final)


def _t5_bucket(rel):
    nb = REL_BUCKETS // 2
    max_exact = nb // 2
    ret = jnp.where(rel > 0, nb, 0)
    n = jnp.abs(rel)
    nf = jnp.maximum(n, 1).astype(F32)
    large = max_exact + (jnp.log(nf / max_exact) / math.log(REL_MAX_DIST / max_exact)
                         * (nb - max_exact)).astype(jnp.int32)
    large = jnp.minimum(large, nb - 1)
    return ret + jnp.where(n < max_exact, n, large)


def _swa_bias_tiles(rel_bias, seq):
    half = SWA_HALF
    tiles = []
    for g, (_, dil) in enumerate(SWA_PATTERNS):
        qb = min(SWA_QUERY_BLOCK, seq // dil)
        win = qb + 2 * half
        n = win + qb
        offs = jnp.arange(-half, half + 1) * dil
        table = rel_bias[_t5_bucket(offs)][:, g * SWA_GROUP_HEADS:(g + 1) * SWA_GROUP_HEADS].T
        by_lag = jnp.concatenate(
            [table, jnp.full((SWA_GROUP_HEADS, n - table.shape[1]), NEG_INF, table.dtype)], axis=1)
        rows = jnp.broadcast_to(by_lag[:, None, :], (SWA_GROUP_HEADS, qb, n)).reshape(SWA_GROUP_HEADS, qb * n)
        tiles.append(rows[:, :qb * (n - 1)].reshape(SWA_GROUP_HEADS, qb, n - 1)[:, :, :win].astype(F32))
    return tiles


def _diff_bias_segments(rel_bias, seq):
    tq = min(DIFF_Q_TILE, seq)
    table = rel_bias[:, len(SWA_PATTERNS) * SWA_GROUP_HEADS:].T
    by_rel = table[:, _t5_bucket(jnp.arange(2 * seq) - seq)]
    segs = [by_rel[:, seq - (i + 1) * tq:2 * seq - i * tq] for i in range(seq // tq)]
    return jnp.stack(segs, axis=1)[:, :, None, :].astype(F32)


def _pack_w_in(w_in_l):
    a_w = 3 * SWA_WIDTH
    b_w = 3 * RWKV_WIDTH + RWKV_LORA
    c_w = 3 * DIFF_WIDTH
    a, b, c, gates = (w_in_l[:, :a_w], w_in_l[:, a_w:a_w + b_w], w_in_l[:, a_w + b_w:a_w + b_w + c_w],
                      w_in_l[:, a_w + b_w + c_w:])
    gw = SWA_GROUP_WIDTH

    def group(g):
        return [a[:, t * SWA_WIDTH + g * gw:t * SWA_WIDTH + (g + 1) * gw] for t in range(3)]

    pad = jnp.zeros((w_in_l.shape[0], PROJ_COLS_PADDED - PROJ_COLS), w_in_l.dtype)
    return jnp.concatenate([gates] + group(1) + group(2) + group(0) + [c, b, pad], axis=1).astype(BF16)


def _direction_padded(w_up):
    z = jnp.zeros_like(w_up[0])
    return jnp.stack([jnp.concatenate([w_up[0], z], axis=0), jnp.concatenate([z, w_up[1]], axis=0)])


def kernel(x, c, w_mod, b_mod, norm1_g, norm2_g, w_in, rwkv_mu, rwkv_w0, rwkv_w_up, rwkv_a0, rwkv_a_up, rwkv_g_up, rwkv_k_k, rwkv_k_a, rwkv_r_k, rwkv_ln_g, rwkv_ln_b, diff_lambda, diff_subln_g, rel_bias, w_branch_a, w_branch_b, w_branch_c, w_out, router_w, router_b, moe_w1, moe_b1, moe_w2, moe_b2, final_norm_g):
    batch, seq, d = x.shape
    depth = w_mod.shape[0]
    mod = _modulation(c, w_mod, b_mod)
    swa_tiles = _swa_bias_tiles(rel_bias, seq)
    bias_segs = _diff_bias_segments(rel_bias, seq)
    head_of = jnp.arange(RWKV_PACK * HEAD_DIM) // HEAD_DIM
    gsum = (head_of[:, None] == head_of[None, :]).astype(BF16)
    b1 = moe_b1[:, :, None, :]
    b2 = moe_b2[:, :, None, :]

    for l in range(depth):
        mod_l = mod[l]
        proj, res1, res2 = _norm_proj(x, norm1_g[l], mod_l, _pack_w_in(w_in[l]))
        o0, lse0 = _swa_group(proj.reshape(batch, 1, seq, -1), swa_tiles[0], COL_A0 // SWA_GROUP_WIDTH)
        swa_outs = [(o0.reshape(batch, seq, -1), lse0.reshape(batch, seq, -1)),
                    _swa_group(res1, swa_tiles[1], 0), _swa_group(res2, swa_tiles[2], 0)]
        r, v, g, bonus, kk, bb, kd, lw = _rwkv_prep(
            proj, rwkv_mu[l], rwkv_w0[l], _direction_padded(rwkv_w_up[l]), rwkv_a0[l],
            _direction_padded(rwkv_a_up[l]), rwkv_g_up[l], rwkv_k_k[l], rwkv_k_a[l], rwkv_r_k[l], gsum)
        y_fwd, y_bwd = _rwkv_scan(r, v, kk, bb, kd, lw)
        o_b = _rwkv_post(y_fwd, y_bwd, bonus, g, rwkv_ln_g[l], rwkv_ln_b[l], gsum)
        lambda_init = 0.8 - 0.6 * math.exp(-0.3 * l)
        o_c = _diff_attention(proj, bias_segs, diff_lambda[l], diff_subln_g[l], lambda_init)
        x, h2, idx, probs = _merge(
            proj, swa_outs, o_b, o_c, x, w_branch_a[l].astype(BF16), w_branch_b[l].astype(BF16),
            w_branch_c[l].astype(BF16), w_out[l].astype(BF16), mod_l, norm2_g[l], router_w[l], router_b[l])
        x = _moe(h2, idx, probs, x, mod_l, moe_w1, b1, moe_w2, b2, l, final_norm_g, l == depth - 1)
    return x
```

```python
import functools
import math

import jax
import jax.numpy as jnp
from jax import lax
from jax.experimental import pallas as pl
from jax.experimental.pallas import tpu as pltpu

F32 = jnp.float32
BF16 = jnp.bfloat16
HIGHEST = lax.Precision.HIGHEST

D_MODEL = 1024
HEAD_DIM = 64
LANES = 128
NORM_EPS = 1e-6
NEG_INF = -1e30
LOG2_E = 1.4426950408889634

SWA_PATTERNS = ((128, 1), (512, 4), (2048, 16))
SWA_GROUP_HEADS = 4
SWA_GROUP_WIDTH = SWA_GROUP_HEADS * HEAD_DIM
SWA_WIDTH = len(SWA_PATTERNS) * SWA_GROUP_WIDTH
SWA_HALF = 64
SWA_QUERY_BLOCK = 256

RWKV_HEADS = 12
RWKV_WIDTH = RWKV_HEADS * HEAD_DIM
RWKV_LORA = 384
RWKV_GN_EPS = 64e-5
RWKV_CHUNK = 64
RWKV_BATCH_BLOCK = 2
RWKV_PACK = 4

DIFF_HEADS = 6
DIFF_WIDTH = DIFF_HEADS * 2 * HEAD_DIM
DIFF_SUBLN_EPS = 1e-5
DIFF_Q_TILE = 512

REL_BUCKETS = 32
REL_MAX_DIST = 128

N_EXPERTS = 32
TOP_K = 4
D_EXPERT = 1024
SWIGLU_LIMIT = 7.0
SWIGLU_ALPHA = 1.702
MOE_ROW_TILE = 256
TOKEN_TABLE_COPIES = 2
SWIGLU_BLOCK = 256
MOE_ROW_CHUNKS = 4

COL_GATES = 0
COL_A_DILATED = 3 * D_MODEL
COL_A0 = COL_A_DILATED + 2 * SWA_WIDTH
COL_C = COL_A0 + SWA_WIDTH
COL_R = COL_C + 3 * DIFF_WIDTH
COL_LORA = COL_R + 3 * RWKV_WIDTH
PROJ_COLS = COL_LORA + RWKV_LORA
PROJ_COLS_PADDED = 10752
PROJ_COL_TILE = 1536
STRIDED_COL_TILE = COL_A_DILATED // PROJ_COL_TILE
PROJ_ROW_TILE = 1024

VMEM_LIMIT = 48 * 1024 * 1024
EXPERT_VMEM_LIMIT = 56 * 1024 * 1024

_NT = ((1,), (1,))
_TN = ((0,), (0,))


def _params(*sem):
    return pltpu.CompilerParams(dimension_semantics=sem, vmem_limit_bytes=VMEM_LIMIT)


def _sigmoid(x):
    return 1.0 / (1.0 + jnp.exp(-x))


def _dot(a, b, dims=((1,), (0,)), precision=None):
    return lax.dot_general(a, b, (dims, ((), ())), precision=precision, preferred_element_type=F32)


def _hi_lo(x):
    hi = x.astype(BF16)
    return hi, (x - hi.astype(F32)).astype(BF16)


def _dot3(a, b):
    a_hi, a_lo = _hi_lo(a)
    b_hi, b_lo = _hi_lo(b)
    return _dot(a_hi, b_hi) + _dot(a_hi, b_lo) + _dot(a_lo, b_hi)


def _head_sum(x, ones_bf16):
    hi, lo = _hi_lo(x)
    g = ones_bf16.shape[0]
    return jnp.concatenate(
        [_dot(hi[:, c:c + g], ones_bf16) + _dot(lo[:, c:c + g], ones_bf16) for c in range(0, x.shape[1], g)], axis=-1)


def _mod_kernel(c_ref, w_ref, b_ref, o_ref):
    c = c_ref[...]
    cond = c * _sigmoid(c)
    o_ref[...] = _dot(cond, w_ref[...], precision=HIGHEST) + b_ref[...]


def _modulation(c, w_mod, b_mod):
    n_layers, d, n = w_mod.shape
    batch = c.shape[0]
    tn = 1536
    out = pl.pallas_call(
        _mod_kernel,
        name="modulation",
        grid=(n_layers, n // tn),
        in_specs=[
            pl.BlockSpec((batch, d), lambda l, j: (0, 0)),
            pl.BlockSpec((None, d, tn), lambda l, j: (l, 0, j)),
            pl.BlockSpec((None, 1, tn), lambda l, j: (l, 0, j)),
        ],
        out_specs=pl.BlockSpec((None, batch, tn), lambda l, j: (l, 0, j)),
        out_shape=jax.ShapeDtypeStruct((n_layers, batch, n), F32),
        compiler_params=_params("parallel", "parallel"),
    )(c, w_mod, b_mod.reshape(n_layers, 1, n))
    return out.reshape(n_layers, batch, 6, 1, d)


def _modulated_norm(x, g, scale, shift):
    y = x * lax.rsqrt(jnp.mean(x * x, axis=-1, keepdims=True) + NORM_EPS) * g
    return y * (1.0 + scale) + shift


def _norm_proj_kernel(x_ref, g_ref, mod_ref, w_ref, o_ref, res4_ref, res16_ref, h_ref, acc_ref):
    j = pl.program_id(2)

    @pl.when(j == 0)
    def _():
        h_ref[...] = _modulated_norm(x_ref[...], g_ref[...], mod_ref[1], mod_ref[0]).astype(BF16)

    @pl.when(j != STRIDED_COL_TILE)
    def _():
        o_ref[...] = _dot(h_ref[...], w_ref[...]).astype(o_ref.dtype)

    @pl.when(j == STRIDED_COL_TILE)
    def _():
        acc = _dot(h_ref[...], w_ref[...])
        o_ref[...] = acc.astype(o_ref.dtype)
        lanes = acc_ref.shape[2]
        tm = acc_ref.shape[1]
        for cb in range(acc_ref.shape[0]):
            acc_ref[cb] = acc[:, cb * lanes:(cb + 1) * lanes]
        per_group = SWA_WIDTH // lanes
        for res_ref, dil, first in ((res4_ref, SWA_PATTERNS[1][1], 0), (res16_ref, SWA_PATTERNS[2][1], per_group)):
            for r in range(dil):
                for cb in range(per_group):
                    res_ref[r, :, cb * lanes:(cb + 1) * lanes] = (
                        acc_ref[first + cb, pl.ds(r, tm // dil, stride=dil), :].astype(res_ref.dtype))


def _norm_proj(x, g, mod_l, w):
    batch, seq, d = x.shape
    n = w.shape[1]
    tm, tn = min(PROJ_ROW_TILE, seq), PROJ_COL_TILE
    d1, d2 = SWA_PATTERNS[1][1], SWA_PATTERNS[2][1]

    def res_spec(dil):
        return pl.BlockSpec((None, dil, tm // dil, SWA_WIDTH), lambda b, i, j: (b, 0, i, 0))

    return pl.pallas_call(
        _norm_proj_kernel,
        name="norm_proj",
        grid=(batch, seq // tm, n // tn),
        in_specs=[
            pl.BlockSpec((None, tm, d), lambda b, i, j: (b, i, 0)),
            pl.BlockSpec((1, d), lambda b, i, j: (0, 0)),
            pl.BlockSpec((None, 6, 1, d), lambda b, i, j: (b, 0, 0, 0)),
            pl.BlockSpec((d, tn), lambda b, i, j: (0, j)),
        ],
        out_specs=[pl.BlockSpec((None, tm, tn), lambda b, i, j: (b, i, j)), res_spec(d1), res_spec(d2)],
        out_shape=[jax.ShapeDtypeStruct((batch, seq, n), BF16),
                   jax.ShapeDtypeStruct((batch, d1, seq // d1, SWA_WIDTH), BF16),
                   jax.ShapeDtypeStruct((batch, d2, seq // d2, SWA_WIDTH), BF16)],
        scratch_shapes=[pltpu.VMEM((tm, d), BF16), pltpu.VMEM((tn // LANES, tm, LANES), F32)],
        compiler_params=_params("parallel", "parallel", "arbitrary"),
    )(x, g.reshape(1, d), mod_l, w)


def _swa_kernel(q_ref, k_ref, v_ref, bias_ref, o_ref, lse_ref, kpad_ref, vpad_ref):
    half, hd = SWA_HALF, HEAD_DIM
    residues, length, _ = q_ref.shape
    qb = bias_ref.shape[1]
    win = qb + 2 * half
    zeros = jnp.zeros((half, SWA_GROUP_WIDTH), BF16)
    for pad_ref in (kpad_ref, vpad_ref):
        pad_ref[0:half, :] = zeros
        pad_ref[half + length:half + length + half, :] = zeros

    def residue(r, carry):
        kpad_ref[half:half + length, :] = k_ref[r]
        vpad_ref[half:half + length, :] = v_ref[r]

        def block(n, inner):
            q0 = pl.multiple_of(n * qb, qb)
            q = q_ref[r, pl.ds(q0, qb), :]
            kw = kpad_ref[pl.ds(q0, win), :]
            vw = vpad_ref[pl.ds(q0, win), :]
            key_pos = q0 - half + lax.broadcasted_iota(jnp.int32, (1, win), 1)
            in_range = (key_pos >= 0) & (key_pos < length)
            heads = [slice(h * hd, (h + 1) * hd) for h in range(SWA_GROUP_HEADS)]
            scores = [jnp.where(in_range, _dot(q[:, sl], kw[:, sl], _NT) * (hd ** -0.5) + bias_ref[h], NEG_INF)
                      for h, sl in enumerate(heads)]
            maxes = [jnp.max(s, axis=-1, keepdims=True) for s in scores]
            probs = [jnp.exp(s - m) for s, m in zip(scores, maxes)]
            sums = [jnp.sum(p, axis=-1, keepdims=True) for p in probs]
            outs = [_dot(p.astype(BF16), vw[:, sl]) / l for p, sl, l in zip(probs, heads, sums)]
            lses = [jnp.broadcast_to(m + jnp.log(l), (qb, hd)) for m, l in zip(maxes, sums)]
            o_ref[r, pl.ds(q0, qb), :] = jnp.concatenate(outs, axis=-1)
            lse_ref[r, pl.ds(q0, qb), :] = jnp.concatenate(lses, axis=-1)
            return inner

        lax.fori_loop(0, length // qb, block, 0)
        return carry

    lax.fori_loop(0, residues, residue, 0)


def _swa_group(qkv, bias_tile, col_block):
    gw = SWA_GROUP_WIDTH
    batch, residues, length, _ = qkv.shape
    block = (None, residues, length, gw)
    out_spec = pl.BlockSpec(block, lambda b: (b, 0, 0, 0))
    out_shape = jax.ShapeDtypeStruct((batch, residues, length, gw), F32)
    return pl.pallas_call(
        _swa_kernel,
        name="swa",
        grid=(batch,),
        in_specs=[pl.BlockSpec(block, functools.partial(lambda b, col: (b, 0, 0, col), col=col_block + which))
                  for which in range(3)]
        + [pl.BlockSpec(bias_tile.shape, lambda b: (0, 0, 0))],
        out_specs=[out_spec, out_spec],
        out_shape=[out_shape, out_shape],
        scratch_shapes=[pltpu.VMEM((length + 2 * SWA_HALF, gw), BF16)] * 2,
        compiler_params=_params("parallel"),
    )(qkv, qkv, qkv, bias_tile)


def _diff_kernel(q_ref, k_ref, v_ref, seg_ref, dl_ref, g_ref, o_ref, bias_ref, *, lambda_init):
    hd = HEAD_DIM
    tq, seq = bias_ref.shape

    @pl.when(pl.program_id(2) == 0)
    def _():
        seg = jnp.broadcast_to(seg_ref[...] * LOG2_E, (tq, seq + tq))
        bias_ref[...] = pltpu.roll(seg, seq, 1, stride=1, stride_axis=0)[:, :seq]

    dl = dl_ref[...]
    lam = (jnp.exp(jnp.sum(dl[0:1] * dl[1:2], axis=-1, keepdims=True))
           - jnp.exp(jnp.sum(dl[2:3] * dl[3:4], axis=-1, keepdims=True)) + lambda_init)
    k = k_ref[...]
    v = v_ref[...]
    q = (q_ref[...].astype(F32) * (hd ** -0.5 * LOG2_E)).astype(BF16)
    chains = [(slice(r0, r0 + 128), slice(comp * hd, (comp + 1) * hd)) for r0 in range(0, tq, 128) for comp in range(2)]
    ones_col = (lax.broadcasted_iota(jnp.int32, v.shape, 1) == 0).astype(BF16)
    v_wide = jnp.concatenate([v, ones_col], axis=-1)
    scores = [_dot(q[rows, sl], k[:, sl], _NT) + bias_ref[rows, :] for rows, sl in chains]
    probs = [jnp.exp2(s - jnp.max(s, axis=-1, keepdims=True)).astype(BF16) for s in scores]
    wide = [_dot(p, v_wide) for p in probs]
    outs = [w[:, :2 * hd] * (1.0 / w[:, 2 * hd:2 * hd + 1]) for w in wide]
    for i in range(0, len(chains), 2):
        o = outs[i] - lam * outs[i + 1]
        o = o * lax.rsqrt(jnp.mean(o * o, axis=-1, keepdims=True) + DIFF_SUBLN_EPS) * g_ref[...]
        o_ref[chains[i][0], :] = (o * (1.0 - lambda_init)).astype(o_ref.dtype)


def _diff_attention(proj, bias_segs, diff_lambda_l, subln_g, lambda_init):
    batch, seq, _ = proj.shape
    hw = 2 * HEAD_DIM
    tq = min(DIFF_Q_TILE, seq)
    base = COL_C // hw
    return pl.pallas_call(
        functools.partial(_diff_kernel, lambda_init=lambda_init),
        name="diff_attn",
        grid=(DIFF_HEADS, seq // tq, batch),
        in_specs=[
            pl.BlockSpec((None, tq, hw), lambda h, i, b: (b, i, base + h)),
            pl.BlockSpec((None, seq, hw), lambda h, i, b: (b, 0, base + DIFF_HEADS + h)),
            pl.BlockSpec((None, seq, hw), lambda h, i, b: (b, 0, base + 2 * DIFF_HEADS + h)),
            pl.BlockSpec((None, None, 1, seq + tq), lambda h, i, b: (h, i, 0, 0)),
            pl.BlockSpec((4, HEAD_DIM), lambda h, i, b: (0, 0)),
            pl.BlockSpec((1, hw), lambda h, i, b: (0, 0)),
        ],
        out_specs=pl.BlockSpec((None, tq, hw), lambda h, i, b: (b, i, h)),
        out_shape=jax.ShapeDtypeStruct((batch, seq, DIFF_WIDTH), BF16),
        scratch_shapes=[pltpu.VMEM((tq, seq), F32)],
        compiler_params=_params("parallel", "parallel", "arbitrary"),
    )(proj, proj, proj, bias_segs, diff_lambda_l, subln_g.reshape(1, hw))


def _shifted(cur_ref, prev_ref, next_ref, mu, first, last):
    x = cur_ref[...].astype(F32)
    rows = x.shape[0]
    halo = prev_ref.shape[0]
    before = jnp.where(first, 0.0, prev_ref[halo - 1:halo, :].astype(F32))
    after = jnp.where(last, 0.0, next_ref[0:1, :].astype(F32))
    row = lax.broadcasted_iota(jnp.int32, (rows, 1), 0)
    prev = jnp.where(row == 0, before, pltpu.roll(x, 1, axis=0))
    nxt = jnp.where(row == rows - 1, after, pltpu.roll(x, rows - 1, axis=0))
    return x + mu[0:1] * (prev - x) + mu[1:2] * (nxt - x)


def _rwkv_prep_kernel(r_ref, rp_ref, rn_ref, k_ref, kp_ref, kn_ref, v_ref, vp_ref, vn_ref,
                      lo_ref, lop_ref, lon_ref, mu_ref, w0_ref, wup_ref, a0_ref, aup_ref, gup_ref,
                      kk_scale_ref, ka_ref, rk_ref, gsum_ref,
                      r_out, v_out, g_out, bonus_out, kk_out, bb_out, kd_out, lw_out):
    w = RWKV_WIDTH
    first = pl.program_id(1) == 0
    last = pl.program_id(1) == pl.num_programs(1) - 1
    mu = mu_ref[...]
    r = _shifted(r_ref, rp_ref, rn_ref, mu[:, 0:w], first, last)
    k = _shifted(k_ref, kp_ref, kn_ref, mu[:, w:2 * w], first, last)
    v = _shifted(v_ref, vp_ref, vn_ref, mu[:, 2 * w:3 * w], first, last)
    lora = _shifted(lo_ref, lop_ref, lon_ref, mu[:, 3 * w:3 * w + RWKV_LORA], first, last)
    decay_in = jnp.tanh(lora[:, 0:128])
    iclr_in = lora[:, 128:256]
    gsum = gsum_ref[...]
    r_out[...] = r
    v_out[...] = v
    g_out[...] = _dot3(_sigmoid(lora[:, 256:384]), gup_ref[...])
    bonus = jnp.zeros_like(r)
    for di in range(2):
        z = w0_ref[di:di + 1, :] + _dot3(decay_in, wup_ref[di])
        u = -z
        softplus = jnp.maximum(u, 0.0) + jnp.log(1.0 + jnp.exp(-jnp.abs(u)))
        lw_out[di] = -jnp.exp(-softplus - 0.5)
        a = _sigmoid(a0_ref[di:di + 1, :] + _dot3(iclr_in, aup_ref[di]))
        kk = k * kk_scale_ref[di:di + 1, :]
        kk = kk * lax.rsqrt(jnp.maximum(_head_sum(kk * kk, gsum), 1e-24))
        kd = k * (1.0 + (a - 1.0) * ka_ref[di:di + 1, :])
        kk_out[di] = kk
        bb_out[di] = kk * a
        kd_out[di] = kd
        bonus = bonus + _head_sum(r * kd * rk_ref[...], gsum) * v
    bonus_out[...] = bonus


def _rwkv_prep(proj, mu, w0, wup2, a0, aup2, g_up, k_k, k_a, r_k, gsum):
    batch, seq, _ = proj.shape
    w = RWKV_WIDTH
    tt = min(256, seq)
    halo = 16
    hb = tt // halo
    last_halo = seq // halo - 1

    def cur(width, blk):
        return pl.BlockSpec((None, tt, width), lambda b, i: (b, i, blk))

    def prev(width, blk):
        return pl.BlockSpec((None, halo, width), lambda b, i: (b, jnp.maximum(i * hb - 1, 0), blk))

    def nxt(width, blk):
        return pl.BlockSpec((None, halo, width), lambda b, i: (b, jnp.minimum((i + 1) * hb, last_halo), blk))

    def full(shape):
        return pl.BlockSpec(shape, lambda b, i: (0,) * len(shape))

    in_specs = []
    for blk in (COL_R // w, COL_R // w + 1, COL_R // w + 2):
        in_specs += [cur(w, blk), prev(w, blk), nxt(w, blk)]
    lb = COL_LORA // RWKV_LORA
    in_specs += [cur(RWKV_LORA, lb), prev(RWKV_LORA, lb), nxt(RWKV_LORA, lb)]
    in_specs += [full(mu.shape), full(w0.shape), full(wup2.shape), full(a0.shape), full(aup2.shape),
                 full(g_up.shape), full(k_k.shape), full(k_a.shape), full((1, w)), full(gsum.shape)]
    shared = pl.BlockSpec((None, tt, w), lambda b, i: (b, i, 0))
    per_dir = pl.BlockSpec((2, None, tt, w), lambda b, i: (0, b, i, 0))
    shared_shape = jax.ShapeDtypeStruct((batch, seq, w), F32)
    per_dir_shape = jax.ShapeDtypeStruct((2, batch, seq, w), F32)
    return pl.pallas_call(
        _rwkv_prep_kernel,
        name="rwkv_prep",
        grid=(batch, seq // tt),
        in_specs=in_specs,
        out_specs=[shared] * 4 + [per_dir] * 4,
        out_shape=[shared_shape] * 4 + [per_dir_shape] * 4,
        compiler_params=_params("parallel", "parallel"),
    )(*([proj] * 12), mu, w0, wup2, a0, aup2, g_up, k_k, k_a, r_k.reshape(1, w), gsum)


def _rwkv_scan_kernel(rf_ref, rb_ref, vf_ref, vb_ref, kkf_ref, kkb_ref, bbf_ref, bbb_ref, kdf_ref, kdb_ref,
                      lwf_ref, lwb_ref, yf_ref, yb_ref, state_ref):
    chunk, hd, pack = RWKV_CHUNK, HEAD_DIM, RWKV_PACK
    pw = pack * hd
    groups = RWKV_HEADS // pack

    @pl.when(pl.program_id(1) == 0)
    def _():
        state_ref[...] = jnp.zeros_like(state_ref)

    ti = lax.broadcasted_iota(jnp.int32, (chunk, pw), 0)
    tj = lax.broadcasted_iota(jnp.int32, (chunk, pw), 1) % chunk
    eye = (ti == tj).astype(F32)
    same16 = (ti // 16) == (tj // 16)
    same32 = (ti // 32) == (tj // 32)
    diag = (lax.broadcasted_iota(jnp.int32, (pw, pw), 0) // hd) == (lax.broadcasted_iota(jnp.int32, (pw, pw), 1) // hd)
    diag_bf16 = diag.astype(BF16)

    def block_diag(x):
        return jnp.concatenate([x.astype(BF16)] * pack, axis=0) * diag_bf16

    def stacked(top, bottom):
        return jnp.concatenate([top, bottom], axis=0).astype(BF16)

    chains = []
    per_direction = ((rf_ref, vf_ref, kkf_ref, bbf_ref, kdf_ref, lwf_ref), (rb_ref, vb_ref, kkb_ref, bbb_ref, kdb_ref, lwb_ref))
    for n, (d, refs) in ((n, dr) for n in range(rf_ref.shape[0]) for dr in enumerate(per_direction)):
        r_ref, v_ref, kk_ref, bb_ref, kd_ref, lw_ref = (ref.at[n] for ref in refs)
        lag = ti - tj if d == 0 else tj - ti
        before, upto = lag > 0, lag >= 0
        lw = lw_ref[...]
        tri = upto[:, :chunk].astype(BF16)
        lw_hi = lw.astype(BF16)
        lw_rest = lw - lw_hi.astype(F32)
        lw_mid = lw_rest.astype(BF16)
        lw_lo = (lw_rest - lw_mid.astype(F32)).astype(BF16)
        cum = _dot(tri, lw_hi) + _dot(tri, lw_mid) + _dot(tri, lw_lo)
        total = jnp.sum(lw, axis=0, keepdims=True)
        p_inv = jnp.exp(-cum)
        p_rest = jnp.exp(total - cum)
        p_total = jnp.exp(total)
        kk, bb, kd = kk_ref[...], bb_ref[...], kd_ref[...]
        a_all = -kk * jnp.exp(cum - lw)
        b_all = bb * p_inv
        k_all = kd * p_inv
        r_all = r_ref[...] * jnp.exp(cum)
        b_end = bb * p_rest
        k_end = kd * p_rest
        v_all = v_ref[...]
        for g in range(groups):
            sl = slice(g * pw, (g + 1) * pw)
            chains.append(dict(
                n=n, d=d, g=g, before=before, upto=upto, ar=stacked(a_all[:, sl], r_all[:, sl]), b=b_all[:, sl],
                k=k_all[:, sl], v=v_all[:, sl], ends=stacked(b_end[:, sl], k_end[:, sl]), p_total=p_total[:, sl]))

    for c in chains:
        c["by_b"] = _dot(c["ar"], block_diag(c["b"]), _NT)
        c["by_k"] = _dot(c["ar"], block_diag(c["k"]), _NT)
    for c in chains:
        c["m_ab"] = jnp.where(c["before"], c["by_b"][:chunk], 0.0)
        c["m_rb"] = jnp.where(c["upto"], c["by_b"][chunk:], 0.0)
        c["m_kv"] = stacked(jnp.where(c["before"], c["by_k"][:chunk], 0.0), jnp.where(c["upto"], c["by_k"][chunk:], 0.0))
        c["x"] = jnp.where(same16, c["m_ab"], 0.0)
        c["inv"] = eye + c["x"]
    for c in chains:
        c["x"] = _dot(c["x"].astype(BF16), block_diag(c["x"]))
    for _ in range(2):
        for c in chains:
            both = _dot(stacked(c["x"], c["inv"]), block_diag(c["x"]))
            c["x"] = both[:chunk]
            c["inv"] = c["inv"] + both[chunk:]
    for c in chains:
        c["inv"] = c["inv"] + _dot(c["inv"].astype(BF16), block_diag(c["x"]))
    for level in range(2):
        for c in chains:
            off = jnp.where(same32 & ~same16, c["m_ab"], 0.0) if level == 0 else jnp.where(~same32, c["m_ab"], 0.0)
            c["inner"] = _dot(off.astype(BF16), block_diag(c["inv"]))
        for c in chains:
            c["inv"] = c["inv"] + _dot(c["inv"].astype(BF16), block_diag(c["inner"]))
    for c in chains:
        c["state"] = state_ref[c["n"], c["d"], c["g"]]
        c["by_state"] = _dot(c["ar"], c["state"].astype(BF16), _NT)
        c["by_v"] = _dot(c["m_kv"], block_diag(c["v"]))
    for c in chains:
        c["u"] = _dot(c["inv"].astype(BF16), block_diag(c["by_state"][:chunk] + c["by_v"][:chunk]))
    for c in chains:
        c["y"] = c["by_state"][chunk:] + c["by_v"][chunk:] + _dot(c["m_rb"].astype(BF16), block_diag(c["u"]))
        update = _dot(stacked(c["u"], c["v"]), c["ends"], _TN)
        state_ref[c["n"], c["d"], c["g"]] = c["state"] * c["p_total"] + jnp.where(diag, update, 0.0)
    for n in range(rf_ref.shape[0]):
        for d, y_ref in enumerate((yf_ref, yb_ref)):
            y_ref[n] = jnp.concatenate([c["y"] for c in chains if (c["n"], c["d"]) == (n, d)], axis=-1)


def _rwkv_scan(r, v, kk, bb, kd, lw):
    batch, seq, w = r.shape
    chunk = RWKV_CHUNK
    assert chunk == HEAD_DIM and RWKV_HEADS % RWKV_PACK == 0
    nc = seq // chunk
    pw = RWKV_PACK * HEAD_DIM
    groups = RWKV_HEADS // RWKV_PACK
    nb = RWKV_BATCH_BLOCK
    fwd = pl.BlockSpec((nb, chunk, w), lambda b, c: (b, c, 0))
    bwd = pl.BlockSpec((nb, chunk, w), lambda b, c: (b, nc - 1 - c, 0))
    fwd_dir = pl.BlockSpec((None, nb, chunk, w), lambda b, c: (0, b, c, 0))
    bwd_dir = pl.BlockSpec((None, nb, chunk, w), lambda b, c: (1, b, nc - 1 - c, 0))
    return pl.pallas_call(
        _rwkv_scan_kernel,
        name="rwkv_scan",
        grid=(batch // nb, nc),
        in_specs=[fwd, bwd, fwd, bwd] + [fwd_dir, bwd_dir] * 4,
        out_specs=[fwd, bwd],
        out_shape=[jax.ShapeDtypeStruct((batch, seq, w), F32)] * 2,
        scratch_shapes=[pltpu.VMEM((nb, 2, groups, pw, pw), F32)],
        compiler_params=_params("parallel", "arbitrary"),
    )(r, r, v, v, kk, kk, bb, bb, kd, kd, lw, lw)


def _rwkv_post_kernel(yf_ref, yb_ref, bonus_ref, g_ref, lng_ref, lnb_ref, gsum_ref, o_ref):
    gsum = gsum_ref[...]
    y = yf_ref[...] + yb_ref[...]
    mean = _head_sum(y, gsum) * (1.0 / HEAD_DIM)
    yc = y - mean
    var = _head_sum(yc * yc, gsum) * (1.0 / HEAD_DIM)
    yn = yc * lax.rsqrt(var + RWKV_GN_EPS) * lng_ref[...] + lnb_ref[...]
    o_ref[...] = ((yn + bonus_ref[...]) * g_ref[...]).astype(o_ref.dtype)


def _rwkv_post(y_fwd, y_bwd, bonus, g, ln_g, ln_b, gsum):
    batch, seq, w = y_fwd.shape
    tt = min(512, seq)
    shared = pl.BlockSpec((None, tt, w), lambda b, i: (b, i, 0))
    row = pl.BlockSpec((1, w), lambda b, i: (0, 0))
    return pl.pallas_call(
        _rwkv_post_kernel,
        name="rwkv_post",
        grid=(batch, seq // tt),
        in_specs=[shared, shared, shared, shared, row, row,
                  pl.BlockSpec(gsum.shape, lambda b, i: (0, 0))],
        out_specs=shared,
        out_shape=jax.ShapeDtypeStruct((batch, seq, w), BF16),
        compiler_params=_params("parallel", "parallel"),
    )(y_fwd, y_bwd, bonus, g, ln_g.reshape(1, w), ln_b.reshape(1, w), gsum)


def _merge_kernel(ga_ref, gb_ref, gc_ref, oa0_ref, oa1_ref, oa2_ref, la0_ref, la1_ref, la2_ref,
                  ob_ref, oc_ref, x_ref, wa_ref, wb_ref, wc_ref, wo_ref, mod_ref, g2_ref, rw_ref, rb_ref,
                  x_out, h_out, idx_out, prob_out, *token_order):
    in_token_order = []
    for src_ref, dst_ref in zip((oa1_ref, la1_ref, oa2_ref, la2_ref), token_order):
        dil, per = src_ref.shape[0], src_ref.shape[1]
        for r in range(dil):
            for cb in range(dst_ref.shape[0]):
                dst_ref[cb, pl.ds(r, per, stride=dil), :] = src_ref[r, :, cb * LANES:(cb + 1) * LANES]
        in_token_order.append(jnp.concatenate([dst_ref[cb] for cb in range(dst_ref.shape[0])], axis=-1))
    oa1, la1, oa2, la2 = in_token_order
    lses = [la0_ref[...], la1, la2]
    m = jnp.maximum(jnp.maximum(lses[0], lses[1]), lses[2])
    es = [jnp.exp(l - m) for l in lses]
    inv = 1.0 / (es[0] + es[1] + es[2])
    o_a = (es[0] * oa0_ref[...] + es[1] * oa1 + es[2] * oa2) * inv
    merged = (_sigmoid(ga_ref[...].astype(F32)) * _dot(o_a.astype(BF16), wa_ref[...])
              + _sigmoid(gb_ref[...].astype(F32)) * _dot(ob_ref[...], wb_ref[...])
              + _sigmoid(gc_ref[...].astype(F32)) * _dot(oc_ref[...], wc_ref[...]))
    x = x_ref[...] + mod_ref[2] * _dot(merged.astype(BF16), wo_ref[...])
    x_out[...] = x
    h = _modulated_norm(x, g2_ref[...], mod_ref[4], mod_ref[3])
    for copy in range(h_out.shape[0]):
        h_out[copy] = h.astype(h_out.dtype)

    logits = _dot3(h, rw_ref[...]) + rb_ref[...]
    lane = lax.broadcasted_iota(jnp.int32, logits.shape, 1)
    work = logits
    vals, idxs = [], []
    for _ in range(TOP_K):
        top = jnp.max(work, axis=-1, keepdims=True)
        first = jnp.min(jnp.where(work == top, lane, N_EXPERTS), axis=-1, keepdims=True)
        vals.append(top)
        idxs.append(first)
        work = jnp.where(lane == first, -jnp.inf, work)
    exps = [jnp.exp(t - vals[0]) for t in vals]
    denom = exps[0] + exps[1] + exps[2] + exps[3]
    idx_out[...] = jnp.concatenate(idxs, axis=-1)
    prob_out[...] = jnp.concatenate([e / denom for e in exps], axis=-1)


def _merge(proj, swa_outs, o_b, o_c, x, wa, wb, wc, wo, mod_l, g2, router_w, router_b):
    batch, seq, d = x.shape
    tm = min(512, seq)
    gw = SWA_GROUP_WIDTH

    def rows(width, blk=0):
        return pl.BlockSpec((None, tm, width), lambda b, i: (b, i, blk))

    def full(shape):
        return pl.BlockSpec(shape, lambda b, i: (0,) * len(shape))

    def residue_rows(dil):
        return pl.BlockSpec((None, dil, tm // dil, gw), lambda b, i: (b, 0, i, 0))

    o_list = [o for o, _ in swa_outs]
    l_list = [l for _, l in swa_outs]
    swa_specs = [rows(gw), residue_rows(SWA_PATTERNS[1][1]), residue_rows(SWA_PATTERNS[2][1])]
    return pl.pallas_call(
        _merge_kernel,
        name="merge",
        grid=(batch, seq // tm),
        in_specs=[rows(d, 0), rows(d, 1), rows(d, 2)] + swa_specs * 2
        + [rows(RWKV_WIDTH), rows(DIFF_WIDTH), rows(d), full(wa.shape), full(wb.shape), full(wc.shape),
           full(wo.shape), pl.BlockSpec((None, 6, 1, d), lambda b, i: (b, 0, 0, 0)), full((1, d)),
           full(router_w.shape), full((1, N_EXPERTS))],
        out_specs=[rows(d), pl.BlockSpec((None, TOKEN_TABLE_COPIES, tm, d), lambda b, i: (b, 0, i, 0)), rows(TOP_K),
                   rows(TOP_K)],
        out_shape=[jax.ShapeDtypeStruct((batch, seq, d), F32),
                   jax.ShapeDtypeStruct((batch, TOKEN_TABLE_COPIES, seq, d), BF16),
                   jax.ShapeDtypeStruct((batch, seq, TOP_K), jnp.int32),
                   jax.ShapeDtypeStruct((batch, seq, TOP_K), F32)],
        scratch_shapes=[pltpu.VMEM((gw // LANES, tm, LANES), F32)] * 4,
        compiler_params=_params("parallel", "parallel"),
    )(proj, proj, proj, *o_list, *l_list, o_b, o_c, x, wa, wb, wc, wo, mod_l, g2.reshape(1, d),
      router_w, router_b.reshape(1, N_EXPERTS))


def _expert_kernel(tile_expert_ref, tile_valid_ref, tile_first_ref, x_ref, w1_ref, b1_ref, w2_ref, b2_ref, *refs):
    o_ref, w1_split_ref, b1_split_ref, w2_bf16_ref = refs[-4:]
    t = pl.program_id(0)
    half = SWIGLU_BLOCK // 2

    @pl.when(tile_first_ref[t] != 0)
    def _():
        src = lax.broadcasted_iota(jnp.int32, (SWIGLU_BLOCK, SWIGLU_BLOCK), 0)
        dst = lax.broadcasted_iota(jnp.int32, (SWIGLU_BLOCK, SWIGLU_BLOCK), 1)
        perm = (src == jnp.where(dst < half, 2 * dst, 2 * (dst - half) + 1)).astype(BF16)
        b1 = jnp.broadcast_to(b1_ref[...], (8, b1_ref.shape[1]))
        b1_hi = b1.astype(BF16)
        b1_rest = b1 - b1_hi.astype(F32)
        b1_mid = b1_rest.astype(BF16)
        b1_lo = (b1_rest - b1_mid.astype(F32)).astype(BF16)
        for m in range(w1_ref.shape[1] // SWIGLU_BLOCK):
            cols = slice(m * SWIGLU_BLOCK, (m + 1) * SWIGLU_BLOCK)
            w1_split_ref[:, cols] = _dot(w1_ref[:, cols].astype(BF16), perm).astype(BF16)
            b1_split_ref[:, cols] = (_dot(b1_hi[:, cols], perm) + _dot(b1_mid[:, cols], perm)
                                     + _dot(b1_lo[:, cols], perm))
        w2_bf16_ref[...] = w2_ref[...].astype(BF16)

    @pl.when(tile_valid_ref[t] != 0)
    def _():
        hh = _dot(x_ref[...], w1_split_ref[...]) + b1_split_ref[0:1, :]
        acts = []
        for m in range(hh.shape[1] // SWIGLU_BLOCK):
            glu = jnp.minimum(hh[:, m * SWIGLU_BLOCK:m * SWIGLU_BLOCK + half], SWIGLU_LIMIT)
            lin = jnp.clip(hh[:, m * SWIGLU_BLOCK + half:(m + 1) * SWIGLU_BLOCK], -SWIGLU_LIMIT, SWIGLU_LIMIT)
            acts.append((glu * _sigmoid(SWIGLU_ALPHA * glu) * (lin + 1.0)).astype(BF16))
        act = jnp.concatenate(acts, axis=-1)
        o_ref[...] = (_dot(act, w2_bf16_ref[...]) + b2_ref[...]).astype(o_ref.dtype)

    @pl.when(tile_valid_ref[t] == 0)
    def _():
        o_ref[...] = jnp.zeros_like(o_ref)


def _experts(xg, y, first_tile, tile_expert, tile_valid, tile_first, w1, b1, w2, b2, layer):
    rows, d = xg.shape
    f2 = w1.shape[3]
    tm = MOE_ROW_TILE
    grid_spec = pltpu.PrefetchScalarGridSpec(
        num_scalar_prefetch=3,
        grid=(rows // tm,),
        in_specs=[
            pl.BlockSpec((tm, d), lambda t, te, tv, tf: (t, 0)),
            pl.BlockSpec((None, None, d, f2), lambda t, te, tv, tf: (layer, te[t], 0, 0)),
            pl.BlockSpec((None, None, 1, f2), lambda t, te, tv, tf: (layer, te[t], 0, 0)),
            pl.BlockSpec((None, None, f2 // 2, d), lambda t, te, tv, tf: (layer, te[t], 0, 0)),
            pl.BlockSpec((None, None, 1, d), lambda t, te, tv, tf: (layer, te[t], 0, 0)),
            pl.BlockSpec(memory_space=pl.ANY),
        ],
        out_specs=pl.BlockSpec((tm, d), lambda t, te, tv, tf: (first_tile + t, 0)),
        scratch_shapes=[pltpu.VMEM((d, f2), BF16), pltpu.VMEM((8, f2), F32), pltpu.VMEM((f2 // 2, d), BF16)],
    )
    return pl.pallas_call(
        _expert_kernel,
        name="moe_experts",
        grid_spec=grid_spec,
        out_shape=jax.ShapeDtypeStruct(y.shape, y.dtype),
        input_output_aliases={8: 0},
        compiler_params=pltpu.CompilerParams(dimension_semantics=("arbitrary",), vmem_limit_bytes=EXPERT_VMEM_LIMIT),
    )(tile_expert, tile_valid, tile_first, xg, w1, b1, w2, b2, y)


def _combine_kernel(y_ref, p_ref, x_ref, mod_ref, g_ref, o_ref, *, final):
    p = p_ref[...]
    acc = p[:, 0:1] * y_ref[0].astype(F32)
    for j in range(1, TOP_K):
        acc = acc + p[:, j:j + 1] * y_ref[j].astype(F32)
    x = x_ref[...] + mod_ref[5] * acc
    if final:
        x = x * lax.rsqrt(jnp.mean(x * x, axis=-1, keepdims=True) + NORM_EPS) * g_ref[...]
    o_ref[...] = x


def _combine(y4, probs, x, mod_l, final_g, final):
    batch, seq, d = x.shape
    tm = min(512, seq)
    return pl.pallas_call(
        functools.partial(_combine_kernel, final=final),
        name="moe_combine",
        grid=(batch, seq // tm),
        in_specs=[
            pl.BlockSpec((TOP_K, None, tm, d), lambda b, i: (0, b, i, 0)),
            pl.BlockSpec((None, tm, TOP_K), lambda b, i: (b, i, 0)),
            pl.BlockSpec((None, tm, d), lambda b, i: (b, i, 0)),
            pl.BlockSpec((None, 6, 1, d), lambda b, i: (b, 0, 0, 0)),
            pl.BlockSpec((1, d), lambda b, i: (0, 0)),
        ],
        out_specs=pl.BlockSpec((None, tm, d), lambda b, i: (b, i, 0)),
        out_shape=jax.ShapeDtypeStruct((batch, seq, d), F32),
        compiler_params=_params("parallel", "parallel"),
    )(y4, probs, x, mod_l, final_g.reshape(1, d))


def _dispatch_plan(idx):
    tm = MOE_ROW_TILE
    flat = idx.reshape(-1)
    n_slots = flat.shape[0]
    experts = jnp.arange(N_EXPERTS, dtype=jnp.int32)
    counts = jnp.sum((flat[:, None] == experts[None, :]).astype(jnp.int32), axis=0)
    padded = ((counts + tm - 1) // tm) * tm
    padded_end = jnp.cumsum(padded)
    spare_used = jnp.arange(tm, dtype=jnp.int32)[None, :] < (padded - counts)[:, None]
    spare_key = jnp.where(spare_used, 2 * experts[:, None] + 1, 2 * N_EXPERTS).reshape(-1)
    order = jnp.argsort(jnp.concatenate([2 * flat, spare_key]), stable=True).astype(jnp.int32)
    src_token = jnp.where(order < n_slots, order // TOP_K, 0)
    slot_row = jnp.argsort(order).astype(jnp.int32)[:n_slots]
    tile_start = jnp.arange(order.shape[0] // tm, dtype=jnp.int32) * tm
    tile_expert = jnp.minimum(jnp.sum((padded_end[None, :] <= tile_start[:, None]).astype(jnp.int32), axis=1),
                              N_EXPERTS - 1)
    tile_valid = (tile_start < padded_end[-1]).astype(jnp.int32)
    previous = jnp.concatenate([jnp.full((1,), -1, jnp.int32), tile_expert[:-1]])
    tile_first = tile_valid * (tile_expert != previous).astype(jnp.int32)
    return src_token, slot_row, tile_expert, tile_valid, tile_first


def _moe(h2, idx, probs, x, mod_l, w1, b1, w2, b2, layer, final_g, final):
    batch, seq, d = x.shape
    n_tok = batch * seq
    src_token, slot_row, tile_expert, tile_valid, tile_first = _dispatch_plan(idx)
    tokens = h2.reshape(-1, d)
    src_row = (src_token // seq) * (h2.shape[1] * seq) + src_token % seq
    n_tiles = tile_expert.shape[0]
    per_chunk = n_tiles // MOE_ROW_CHUNKS
    y = jnp.zeros((n_tiles * MOE_ROW_TILE, d), BF16)
    for k in range(MOE_ROW_CHUNKS):
        tiles = slice(k * per_chunk, (k + 1) * per_chunk)
        rows = slice(k * per_chunk * MOE_ROW_TILE, (k + 1) * per_chunk * MOE_ROW_TILE)
        first = tile_first[tiles].at[0].set(tile_valid[k * per_chunk])
        xg = tokens.at[src_row[rows]].get(mode="promise_in_bounds")
        y = _experts(xg, y, k * per_chunk, tile_expert[tiles], tile_valid[tiles], first, w1, b1, w2, b2, layer)
    rows_by_slot = slot_row.reshape(n_tok, TOP_K).T.reshape(-1)
    y4 = y.at[rows_by_slot].get(mode="promise_in_bounds").reshape(TOP_K, batch, seq, d)
    return _combine(y4, probs, x, mod_l, final_g, final)


def _t5_bucket(rel):
    nb = REL_BUCKETS // 2
    max_exact = nb // 2
    ret = jnp.where(rel > 0, nb, 0)
    n = jnp.abs(rel)
    nf = jnp.maximum(n, 1).astype(F32)
    large = max_exact + (jnp.log(nf / max_exact) / math.log(REL_MAX_DIST / max_exact)
                         * (nb - max_exact)).astype(jnp.int32)
    large = jnp.minimum(large, nb - 1)
    return ret + jnp.where(n < max_exact, n, large)


def _swa_bias_tiles(rel_bias, seq):
    half = SWA_HALF
    tiles = []
    for g, (_, dil) in enumerate(SWA_PATTERNS):
        qb = min(SWA_QUERY_BLOCK, seq // dil)
        win = qb + 2 * half
        n = win + qb
        offs = jnp.arange(-half, half + 1) * dil
        table = rel_bias[_t5_bucket(offs)][:, g * SWA_GROUP_HEADS:(g + 1) * SWA_GROUP_HEADS].T
        by_lag = jnp.concatenate(
            [table, jnp.full((SWA_GROUP_HEADS, n - table.shape[1]), NEG_INF, table.dtype)], axis=1)
        rows = jnp.broadcast_to(by_lag[:, None, :], (SWA_GROUP_HEADS, qb, n)).reshape(SWA_GROUP_HEADS, qb * n)
        tiles.append(rows[:, :qb * (n - 1)].reshape(SWA_GROUP_HEADS, qb, n - 1)[:, :, :win].astype(F32))
    return tiles


def _diff_bias_segments(rel_bias, seq):
    tq = min(DIFF_Q_TILE, seq)
    table = rel_bias[:, len(SWA_PATTERNS) * SWA_GROUP_HEADS:].T
    by_rel = table[:, _t5_bucket(jnp.arange(2 * seq) - seq)]
    segs = [by_rel[:, seq - (i + 1) * tq:2 * seq - i * tq] for i in range(seq // tq)]
    return jnp.stack(segs, axis=1)[:, :, None, :].astype(F32)


def _pack_w_in(w_in_l):
    a_w = 3 * SWA_WIDTH
    b_w = 3 * RWKV_WIDTH + RWKV_LORA
    c_w = 3 * DIFF_WIDTH
    a, b, c, gates = (w_in_l[:, :a_w], w_in_l[:, a_w:a_w + b_w], w_in_l[:, a_w + b_w:a_w + b_w + c_w],
                      w_in_l[:, a_w + b_w + c_w:])
    gw = SWA_GROUP_WIDTH

    def group(g):
        return [a[:, t * SWA_WIDTH + g * gw:t * SWA_WIDTH + (g + 1) * gw] for t in range(3)]

    pad = jnp.zeros((w_in_l.shape[0], PROJ_COLS_PADDED - PROJ_COLS), w_in_l.dtype)
    return jnp.concatenate([gates] + group(1) + group(2) + group(0) + [c, b, pad], axis=1).astype(BF16)


def _direction_padded(w_up):
    z = jnp.zeros_like(w_up[0])
    return jnp.stack([jnp.concatenate([w_up[0], z], axis=0), jnp.concatenate([z, w_up[1]], axis=0)])


def kernel(x, c, w_mod, b_mod, norm1_g, norm2_g, w_in, rwkv_mu, rwkv_w0, rwkv_w_up, rwkv_a0, rwkv_a_up, rwkv_g_up, rwkv_k_k, rwkv_k_a, rwkv_r_k, rwkv_ln_g, rwkv_ln_b, diff_lambda, diff_subln_g, rel_bias, w_branch_a, w_branch_b, w_branch_c, w_out, router_w, router_b, moe_w1, moe_b1, moe_w2, moe_b2, final_norm_g):
    batch, seq, d = x.shape
    depth = w_mod.shape[0]
    mod = _modulation(c, w_mod, b_mod)
    swa_tiles = _swa_bias_tiles(rel_bias, seq)
    bias_segs = _diff_bias_segments(rel_bias, seq)
    head_of = jnp.arange(RWKV_PACK * HEAD_DIM) // HEAD_DIM
    gsum = (head_of[:, None] == head_of[None, :]).astype(BF16)
    b1 = moe_b1[:, :, None, :]
    b2 = moe_b2[:, :, None, :]

    for l in range(depth):
        mod_l = mod[l]
        proj, res1, res2 = _norm_proj(x, norm1_g[l], mod_l, _pack_w_in(w_in[l]))
        o0, lse0 = _swa_group(proj.reshape(batch, 1, seq, -1), swa_tiles[0], COL_A0 // SWA_GROUP_WIDTH)
        swa_outs = [(o0.reshape(batch, seq, -1), lse0.reshape(batch, seq, -1)),
                    _swa_group(res1, swa_tiles[1], 0), _swa_group(res2, swa_tiles[2], 0)]
        r, v, g, bonus, kk, bb, kd, lw = _rwkv_prep(
            proj, rwkv_mu[l], rwkv_w0[l], _direction_padded(rwkv_w_up[l]), rwkv_a0[l],
            _direction_padded(rwkv_a_up[l]), rwkv_g_up[l], rwkv_k_k[l], rwkv_k_a[l], rwkv_r_k[l], gsum)
        y_fwd, y_bwd = _rwkv_scan(r, v, kk, bb, kd, lw)
        o_b = _rwkv_post(y_fwd, y_bwd, bonus, g, rwkv_ln_g[l], rwkv_ln_b[l], gsum)
        lambda_init = 0.8 - 0.6 * math.exp(-0.3 * l)
        o_c = _diff_attention(proj, bias_segs, diff_lambda[l], diff_subln_g[l], lambda_init)
        x, h2, idx, probs = _merge(
            proj, swa_outs, o_b, o_c, x, w_branch_a[l].astype(BF16), w_branch_b[l].astype(BF16),
            w_branch_c[l].astype(BF16), w_out[l].astype(BF16), mod_l, norm2_g[l], router_w[l], router_b[l])
        x = _moe(h2, idx, probs, x, mod_l, moe_w1, b1, moe_w2, b2, l, final_norm_g, l == depth - 1)
    return x
```
